```python
import jax, jax.numpy as jnp
from jax import lax
import numpy as np

D_MODEL = 1024
BATCH = 8
SEQ = 4096
DEPTH = 1

CHUNK = 64
Q_BLOCK = 128
MLA_HEADS = 8
QK_NOPE_DIM = 64
QK_ROPE_DIM = 32
V_HEAD_DIM = 64
Q_LORA_RANK = 384
KV_LORA_RANK = 256
MLA_WIDTH = MLA_HEADS * V_HEAD_DIM
ROPE_THETA = 10000.0
POOL_WINDOWS = (2, 4, 8, 16)
POOL_GROUPS = len(POOL_WINDOWS)
POOL_WIDTH = D_MODEL // 2
POOL_GROUP_DIM = POOL_WIDTH // POOL_GROUPS
N_BRANCHES = 2
EPS = 1e-6

IN_SIZES = (Q_LORA_RANK, KV_LORA_RANK, QK_ROPE_DIM, MLA_WIDTH, POOL_WIDTH, POOL_WIDTH, N_BRANCHES * D_MODEL)
IN_SPLITS = tuple(int(s) for s in np.cumsum(IN_SIZES)[:-1])
IN_TOTAL = int(sum(IN_SIZES))

kernel_name = "hybrid_mla_pool_gated_block"


def rms_norm(x, g):
    xf = x.astype(jnp.float32)
    y = xf * lax.rsqrt(jnp.mean(xf * xf, axis=-1, keepdims=True) + EPS)
    return (y * g.astype(jnp.float32)).astype(x.dtype)


def rope_tables(seq):
    half = QK_ROPE_DIM // 2
    inv_freq = ROPE_THETA ** (-jnp.arange(half, dtype=jnp.float32) / half)
    ang = jnp.arange(seq, dtype=jnp.float32)[:, None] * inv_freq[None, :]
    return jnp.cos(ang), jnp.sin(ang)


def apply_rope(x, cos, sin):
    xf = x.astype(jnp.float32)
    x1, x2 = jnp.split(xf, 2, axis=-1)
    out = jnp.concatenate([x1 * cos - x2 * sin, x1 * sin + x2 * cos], axis=-1)
    return out.astype(x.dtype)


def mla_attention(zq, zkv, zkr, q_norm, w_uq, kv_norm, w_ukv):
    b, s, _ = zq.shape
    cos, sin = rope_tables(s)
    c_q = rms_norm(zq, q_norm)
    q = jnp.einsum('bsr,rhd->bshd', c_q, w_uq)
    q_nope = q[..., :QK_NOPE_DIM]
    q_rope = apply_rope(q[..., QK_NOPE_DIM:], cos[None, :, None, :], sin[None, :, None, :])
    c_kv = rms_norm(zkv, kv_norm)
    kv = jnp.einsum('bsr,rhd->bshd', c_kv, w_ukv)
    k_nope = kv[..., :QK_NOPE_DIM]
    v = kv[..., QK_NOPE_DIM:]
    k_rope = apply_rope(zkr, cos[None], sin[None])
    scale = (QK_NOPE_DIM + QK_ROPE_DIM) ** -0.5
    key_chunk = jnp.arange(s) // CHUNK
    n_blocks = s // Q_BLOCK

    def one_block(i):
        start = i * Q_BLOCK
        qn = lax.dynamic_slice_in_dim(q_nope, start, Q_BLOCK, axis=1)
        qr = lax.dynamic_slice_in_dim(q_rope, start, Q_BLOCK, axis=1)
        sc = (jnp.einsum('bqhd,bkhd->bhqk', qn, k_nope).astype(jnp.float32)
              + jnp.einsum('bqhr,bkr->bhqk', qr, k_rope).astype(jnp.float32)) * scale
        q_chunk = (start + jnp.arange(Q_BLOCK)) // CHUNK
        mask = key_chunk[None, :] <= q_chunk[:, None]
        sc = jnp.where(mask[None, None], sc, -jnp.inf)
        p = jax.nn.softmax(sc, axis=-1).astype(v.dtype)
        return jnp.einsum('bhqk,bkhd->bqhd', p, v)

    o = lax.map(one_block, jnp.arange(n_blocks))
    o = jnp.transpose(o, (1, 0, 2, 3, 4)).reshape(b, s, MLA_WIDTH)
    return o


def multiscale_pool(u, pool_w, pool_scale):
    b, s, _ = u.shape
    uf = u.astype(jnp.float32)
    csum = jnp.cumsum(uf, axis=1)
    t = jnp.arange(s)
    outs = []
    for gi, w in enumerate(POOL_WINDOWS):
        sl = slice(gi * POOL_GROUP_DIM, (gi + 1) * POOL_GROUP_DIM)
        cg = csum[..., sl]
        shifted = jnp.pad(cg, ((0, 0), (w, 0), (0, 0)))[:, :s]
        count = jnp.minimum(t + 1, w).astype(jnp.float32)[None, :, None]
        d = (cg - shifted) / count - uf[..., sl]
        outs.append(jnp.einsum('bsc,cd->bsd', d, pool_w[gi].astype(jnp.float32)))
    y = jnp.concatenate(outs, axis=-1) * pool_scale.astype(jnp.float32)
    return y.astype(u.dtype)


def setup_inputs(seed: int = 0) -> dict:
    key = jax.random.key(seed)
    ks = jax.random.split(key, 14)
    f = jnp.float32
    def nrm(k, shape, fan_in):
        return jax.random.normal(k, shape, f) * (fan_in ** -0.5)
    return {
        "x": jax.random.normal(ks[0], (BATCH, SEQ, D_MODEL), f),
        "norm_in": 1.0 + 0.02 * jax.random.normal(ks[1], (D_MODEL,), f),
        "w_in": nrm(ks[2], (D_MODEL, IN_TOTAL), D_MODEL),
        "q_norm": 1.0 + 0.02 * jax.random.normal(ks[3], (Q_LORA_RANK,), f),
        "w_uq": nrm(ks[4], (Q_LORA_RANK, MLA_HEADS, QK_NOPE_DIM + QK_ROPE_DIM), Q_LORA_RANK),
        "kv_norm": 1.0 + 0.02 * jax.random.normal(ks[5], (KV_LORA_RANK,), f),
        "w_ukv": nrm(ks[6], (KV_LORA_RANK, MLA_HEADS, QK_NOPE_DIM + V_HEAD_DIM), KV_LORA_RANK),
        "pool_w": nrm(ks[7], (POOL_GROUPS, POOL_GROUP_DIM, POOL_GROUP_DIM), POOL_GROUP_DIM),
        "pool_scale": 1.0 + 0.02 * jax.random.normal(ks[8], (POOL_WIDTH,), f),
        "w_branch_attn": nrm(ks[9], (MLA_WIDTH, D_MODEL), MLA_WIDTH),
        "w_branch_pool": nrm(ks[10], (POOL_WIDTH, D_MODEL), POOL_WIDTH),
        "w_out": nrm(ks[11], (D_MODEL, D_MODEL), D_MODEL),
        "norm_final": 1.0 + 0.02 * jax.random.normal(ks[12], (D_MODEL,), f),
    }


def reference(x, norm_in, w_in, q_norm, w_uq, kv_norm, w_ukv, pool_w, pool_scale,
              w_branch_attn, w_branch_pool, w_out, norm_final):
    h = x
    for _ in range(DEPTH):
        hn = rms_norm(h, norm_in)
        z = jnp.einsum('bsd,de->bse', hn, w_in)
        zq, zkv, zkr, g_attn, u_pool, g_pool, g_merge = jnp.split(z, IN_SPLITS, axis=-1)
        y_attn = mla_attention(zq, zkv, zkr, q_norm, w_uq, kv_norm, w_ukv) * jax.nn.silu(g_attn)
        y_pool = multiscale_pool(u_pool, pool_w, pool_scale) * jax.nn.silu(g_pool)
        a = jnp.einsum('bsc,cd->bsd', y_attn, w_branch_attn)
        p = jnp.einsum('bsc,cd->bsd', y_pool, w_branch_pool)
        gate_a, gate_p = jnp.split(jax.nn.sigmoid(g_merge.astype(jnp.float32)).astype(h.dtype), 2, axis=-1)
        merged = gate_a * a + gate_p * p
        h = h + jnp.einsum('bsd,de->bse', merged, w_out)
    return rms_norm(h, norm_final)
```

```python
import functools

import jax
import jax.numpy as jnp
import numpy as np
from jax import lax
from jax.experimental import pallas as pl
from jax.experimental.pallas import tpu as pltpu

D_MODEL = 1024
CHUNK = 64
HEADS = 8
NOPE = 64
ROPE = 32
VDIM = 64
Q_RANK = 384
KV_RANK = 256
MLA_WIDTH = HEADS * VDIM
ROPE_THETA = 10000.0
POOL_WINDOWS = (2, 4, 8, 16)
POOL_WIDTH = D_MODEL // 2
POOL_GROUP = POOL_WIDTH // len(POOL_WINDOWS)
EPS = 1e-6

LANES = 128
HALO = 16
SLAB = LANES
PROJ_TILE = 256
ATTN_TQ = 512
ATTN_TK = 512
MERGE_TILE = 512
VMEM_LIMIT = 56 * 1024 * 1024

BF16 = jnp.bfloat16
F32 = jnp.float32


def _rms(x, g):
    return x * lax.rsqrt(jnp.mean(x * x, axis=-1, keepdims=True) + EPS) * g


def _sigmoid(x):
    return 1.0 / (1.0 + jnp.exp(-x))


def _rope_slab(slab, c, s1, s2):
    return (slab * c + pltpu.roll(slab, LANES - ROPE // 2, axis=1) * s1
            + pltpu.roll(slab, ROPE // 2, axis=1) * s2)


def _proj_kernel(x_ref, nin_ref, wa_ref, wg_ref, qn_ref, wuq_ref, kvn_ref, wukv_ref,
                 poolw_ref, pscale_ref, qc_ref, qs1_ref, qs2_ref, kc_ref, ks1_ref, ks2_ref,
                 q_out, k_out, v_out, sga_out, ypool_out, gates_out, carry_ref):
    t = PROJ_TILE
    si = pl.program_id(1)
    x = x_ref[0]
    hn = _rms(x, nin_ref[...]).astype(BF16)

    za = jnp.dot(hn, wa_ref[...], preferred_element_type=F32)
    zq = za[:, :Q_RANK]
    zkv = za[:, Q_RANK:Q_RANK + KV_RANK]
    zkr = za[:, Q_RANK + KV_RANK:]
    cq = _rms(zq, qn_ref[...]).astype(BF16)
    ckv = _rms(zkv, kvn_ref[...]).astype(BF16)
    qf = jnp.dot(cq, wuq_ref[...], preferred_element_type=F32)
    kvf = jnp.dot(ckv, wukv_ref[...], preferred_element_type=F32)
    qc, qs1, qs2 = qc_ref[...], qs1_ref[...], qs2_ref[...]
    kr = _rope_slab(zkr, kc_ref[...], ks1_ref[...], ks2_ref[...])
    for h in range(HEADS):
        sl = slice(h * SLAB, (h + 1) * SLAB)
        q_out[0, :, sl] = _rope_slab(qf[:, sl], qc, qs1, qs2).astype(BF16)
        k_out[0, :, sl] = (kvf[:, sl] + kr).astype(BF16)
    v_out[0] = kvf[:, HEADS * SLAB:].astype(BF16)

    zg = jnp.dot(hn, wg_ref[:, :3 * POOL_WIDTH], preferred_element_type=F32)
    g_attn = zg[:, :MLA_WIDTH]
    u = zg[:, MLA_WIDTH:MLA_WIDTH + POOL_WIDTH]
    g_pool = zg[:, MLA_WIDTH + POOL_WIDTH:]
    sga_out[0] = (g_attn * _sigmoid(g_attn)).astype(BF16)
    for c in range(4):
        lo = 3 * POOL_WIDTH + c * 512
        gm = jnp.dot(hn, wg_ref[:, lo:lo + 512], preferred_element_type=F32)
        gates_out[0, :, c * 512:(c + 1) * 512] = _sigmoid(gm).astype(BF16)

    @pl.when(si == 0)
    def _():
        carry_ref[...] = jnp.zeros_like(carry_ref)

    ue = jnp.concatenate([carry_ref[...], u], axis=0)
    carry_ref[...] = u[t - HALO:, :]
    pos = si * t + lax.broadcasted_iota(jnp.int32, (t, 1), 0) + 1
    outs = []
    for gi, w in enumerate(POOL_WINDOWS):
        sl = slice(gi * POOL_GROUP, (gi + 1) * POOL_GROUP)
        acc = ue[:, sl]
        step = 1
        while step < w:
            acc = acc + pltpu.roll(acc, step, axis=0)
            step *= 2
        inv = 1.0 / jnp.minimum(pos, w).astype(F32)
        d = acc[HALO:, :] * inv - u[:, sl]
        outs.append(jnp.dot(d.astype(BF16), poolw_ref[gi], preferred_element_type=F32))
    y = jnp.concatenate(outs, axis=-1) * pscale_ref[...]
    ypool_out[0] = (y * (g_pool * _sigmoid(g_pool))).astype(BF16)


def _attn_kernel(q_ref, k_ref, v_ref, o_ref, m_ref, l_ref, acc_ref):
    tq, tk = ATTN_TQ, ATTN_TK
    i = pl.program_id(2)
    v_lane = lax.broadcasted_iota(jnp.int32, (tq, LANES), 1)
    row_chunk = lax.broadcasted_iota(jnp.int32, (tq, tk), 0) // CHUNK
    col_chunk = lax.broadcasted_iota(jnp.int32, (tq, tk), 1) // CHUNK
    diag_mask = col_chunk <= row_chunk
    res = []
    for h in range(2):
        sl = slice(h * SLAB, (h + 1) * SLAB)
        q = q_ref[0, :, sl]
        m_ref[...] = jnp.full_like(m_ref, -jnp.inf)
        l_ref[...] = jnp.zeros_like(l_ref)
        acc_ref[...] = jnp.zeros_like(acc_ref)

        def step(j, masked):
            k = k_ref[0, pl.ds(j * tk, tk), sl]
            s = lax.dot_general(q, k, (((1,), (1,)), ((), ())), preferred_element_type=F32)
            if masked:
                s = jnp.where(diag_mask, s, -jnp.inf)
            m_old = m_ref[...]
            m_new = jnp.maximum(m_old, jnp.max(s, axis=-1, keepdims=True))
            p = jnp.exp(s - m_new)
            alpha = jnp.exp(m_old - m_new)
            l_ref[...] = alpha * l_ref[...] + jnp.sum(p, axis=-1, keepdims=True)
            v = v_ref[0, pl.ds(j * tk, tk), :]
            acc_ref[...] = alpha * acc_ref[...] + jnp.dot(p.astype(BF16), v, preferred_element_type=F32)
            m_ref[...] = m_new

        def body(j, carry):
            step(j, False)
            return carry

        lax.fori_loop(0, i, body, 0)
        step(i, True)
        res.append(acc_ref[...] / l_ref[...])
    o_ref[0] = jnp.where(v_lane < VDIM, res[0], res[1]).astype(BF16)


def _merge_kernel(x_ref, o_ref, sga_ref, ypool_ref, gates_ref, wa_ref, wp_ref, wo_ref, nf_ref, out_ref):
    ya = (o_ref[0].astype(F32) * sga_ref[0].astype(F32)).astype(BF16)
    a = jnp.dot(ya, wa_ref[...], preferred_element_type=F32)
    p = jnp.dot(ypool_ref[0], wp_ref[...], preferred_element_type=F32)
    ga = gates_ref[0, :, :D_MODEL].astype(F32)
    gp = gates_ref[0, :, D_MODEL:].astype(F32)
    merged = (ga * a + gp * p).astype(BF16)
    h = x_ref[0] + jnp.dot(merged, wo_ref[...], preferred_element_type=F32)
    out_ref[0] = _rms(h, nf_ref[...])


def _rope_tables(seq, scale):
    half = ROPE // 2
    inv_freq = ROPE_THETA ** (-jnp.arange(half, dtype=F32) / half)
    ang = jnp.arange(seq, dtype=F32)[:, None] * inv_freq[None, :]
    cos, sin = jnp.cos(ang), jnp.sin(ang)
    zeros_n = jnp.zeros((seq, NOPE), F32)
    zeros_h = jnp.zeros((seq, half), F32)
    zeros_p = jnp.zeros((seq, SLAB - NOPE - ROPE), F32)
    c = jnp.concatenate([jnp.ones((seq, NOPE), F32), cos, cos, zeros_p], axis=1) * scale
    s1 = jnp.concatenate([zeros_n, -sin, zeros_h, zeros_p], axis=1) * scale
    s2 = jnp.concatenate([zeros_n, zeros_h, sin, zeros_p], axis=1) * scale
    return c, s1, s2


def _const_spec(shape):
    return pl.BlockSpec(shape, lambda *_: (0,) * len(shape))


def kernel(x, norm_in, w_in, q_norm, w_uq, kv_norm, w_ukv, pool_w, pool_scale,
           w_branch_attn, w_branch_pool, w_out, norm_final):
    b, s, d = x.shape
    assert d == D_MODEL and s % ATTN_TQ == 0 and s % PROJ_TILE == 0 and s % MERGE_TILE == 0

    o_kv = Q_RANK
    o_kr = o_kv + KV_RANK
    o_ga = o_kr + ROPE
    w_zq = w_in[:, :o_kv]
    w_zkv = w_in[:, o_kv:o_kr]
    w_zkr = w_in[:, o_kr:o_ga]
    w_kr_slab = jnp.concatenate([jnp.zeros((d, NOPE), F32), w_zkr,
                                 jnp.zeros((d, SLAB - NOPE - ROPE), F32)], axis=1)
    w_a = jnp.concatenate([w_zq, w_zkv, w_kr_slab], axis=1).astype(BF16)
    w_g = w_in[:, o_ga:].astype(BF16)
    w_uq_p = jnp.pad(w_uq, ((0, 0), (0, 0), (0, SLAB - NOPE - ROPE)))
    w_uq_p = w_uq_p.reshape(Q_RANK, HEADS * SLAB).astype(BF16)
    w_uk_p = jnp.pad(w_ukv[:, :, :NOPE], ((0, 0), (0, 0), (0, SLAB - NOPE)))
    w_uk_p = w_uk_p.reshape(KV_RANK, HEADS * SLAB)
    w_uv = w_ukv[:, :, NOPE:].reshape(KV_RANK, MLA_WIDTH)
    w_ukv_p = jnp.concatenate([w_uk_p, w_uv], axis=1).astype(BF16)
    scale = (NOPE + ROPE) ** -0.5
    qc, qs1, qs2 = _rope_tables(s, scale)
    kc, ks1, ks2 = _rope_tables(s, 1.0)

    row = lambda v: v.reshape(1, -1).astype(F32)
    t1 = PROJ_TILE
    tok = lambda width: pl.BlockSpec((1, t1, width), lambda bi, si: (bi, si, 0))
    tab = pl.BlockSpec((t1, SLAB), lambda bi, si: (si, 0))
    q, k, v, sga, ypool, gates = pl.pallas_call(
        _proj_kernel,
        grid=(b, s // t1),
        in_specs=[tok(d), _const_spec((1, d)), _const_spec(w_a.shape), _const_spec(w_g.shape),
                  _const_spec((1, Q_RANK)), _const_spec(w_uq_p.shape), _const_spec((1, KV_RANK)),
                  _const_spec(w_ukv_p.shape), _const_spec(pool_w.shape), _const_spec((1, POOL_WIDTH)),
                  tab, tab, tab, tab, tab, tab],
        out_specs=[tok(HEADS * SLAB), tok(HEADS * SLAB), tok(MLA_WIDTH), tok(MLA_WIDTH),
                   tok(POOL_WIDTH), tok(2 * d)],
        out_shape=[jax.ShapeDtypeStruct((b, s, HEADS * SLAB), BF16),
                   jax.ShapeDtypeStruct((b, s, HEADS * SLAB), BF16),
                   jax.ShapeDtypeStruct((b, s, MLA_WIDTH), BF16),
                   jax.ShapeDtypeStruct((b, s, MLA_WIDTH), BF16),
                   jax.ShapeDtypeStruct((b, s, POOL_WIDTH), BF16),
                   jax.ShapeDtypeStruct((b, s, 2 * d), BF16)],
        scratch_shapes=[pltpu.VMEM((HALO, POOL_WIDTH), F32)],
        compiler_params=pltpu.CompilerParams(
            dimension_semantics=("arbitrary", "arbitrary"), vmem_limit_bytes=VMEM_LIMIT),
        name="proj",
    )(x, row(norm_in), w_a, w_g, row(q_norm), w_uq_p, row(kv_norm), w_ukv_p,
      pool_w.astype(BF16), row(pool_scale), qc, qs1, qs2, kc, ks1, ks2)

    tq = ATTN_TQ
    o = pl.pallas_call(
        _attn_kernel,
        grid=(b, HEADS // 2, s // tq),
        in_specs=[pl.BlockSpec((1, tq, 2 * SLAB), lambda bi, hp, i: (bi, i, hp)),
                  pl.BlockSpec((1, s, 2 * SLAB), lambda bi, hp, i: (bi, 0, hp)),
                  pl.BlockSpec((1, s, LANES), lambda bi, hp, i: (bi, 0, hp))],
        out_specs=pl.BlockSpec((1, tq, LANES), lambda bi, hp, i: (bi, i, hp)),
        out_shape=jax.ShapeDtypeStruct((b, s, MLA_WIDTH), BF16),
        scratch_shapes=[pltpu.VMEM((tq, 1), F32), pltpu.VMEM((tq, 1), F32),
                        pltpu.VMEM((tq, LANES), F32)],
        compiler_params=pltpu.CompilerParams(
            dimension_semantics=("arbitrary", "arbitrary", "arbitrary"), vmem_limit_bytes=VMEM_LIMIT),
        name="attn",
    )(q, k, v)

    t3 = MERGE_TILE
    tok3 = lambda width: pl.BlockSpec((1, t3, width), lambda bi, si: (bi, si, 0))
    out = pl.pallas_call(
        _merge_kernel,
        grid=(b, s // t3),
        in_specs=[tok3(d), tok3(MLA_WIDTH), tok3(MLA_WIDTH), tok3(POOL_WIDTH), tok3(2 * d),
                  _const_spec((MLA_WIDTH, d)), _const_spec((POOL_WIDTH, d)), _const_spec((d, d)),
                  _const_spec((1, d))],
        out_specs=tok3(d),
        out_shape=jax.ShapeDtypeStruct((b, s, d), x.dtype),
        compiler_params=pltpu.CompilerParams(
            dimension_semantics=("arbitrary", "arbitrary"), vmem_limit_bytes=VMEM_LIMIT),
        name="merge",
    )(x, o, sga, ypool, gates, w_branch_attn.astype(BF16), w_branch_pool.astype(BF16),
      w_out.astype(BF16), row(norm_final))
    return out
```

```python
import jax
import jax.numpy as jnp
from jax import lax
from jax.experimental import pallas as pl
from jax.experimental.pallas import tpu as pltpu

D_MODEL = 1024
CHUNK = 64
HEADS = 8
NOPE = 64
ROPE = 32
VDIM = 64
Q_RANK = 384
KV_RANK = 256
MLA_WIDTH = HEADS * VDIM
ROPE_THETA = 10000.0
POOL_WINDOWS = (2, 4, 8, 16)
POOL_WIDTH = D_MODEL // 2
POOL_GROUP = POOL_WIDTH // len(POOL_WINDOWS)
EPS = 1e-6
QK_SCALE = (NOPE + ROPE) ** -0.5 * 1.4426950408889634

LANES = 128
HALO = 16
SLAB = LANES
HEADS_PER_STEP = 2
PROJ_TILE = 512
ATTN_TQ = 256
ATTN_TK = 512
MERGE_TILE = 512
VMEM_LIMIT = 56 * 1024 * 1024

BF16 = jnp.bfloat16
F32 = jnp.float32
NT_DIMS = (((1,), (1,)), ((), ()))


def _rms(x, g):
    return x * lax.rsqrt(jnp.mean(x * x, axis=-1, keepdims=True) + EPS) * g


def _sigmoid(x):
    return 1.0 / (1.0 + jnp.exp(-x))


def _proj_kernel(x_ref, nin_ref, wa_ref, wg_ref, qn_ref, wuqt_ref, kvn_ref, wuk_ref, wuvt_ref,
                 poolw_ref, pscale_ref, qcos_ref, qsin_ref, kc_ref, ks1_ref, ks2_ref,
                 qt_out, k_out, vt_out, sga_out, ypool_out, gates_out, carry_ref):
    t = PROJ_TILE
    si = pl.program_id(1)
    x = x_ref[0]
    hn = _rms(x, nin_ref[...]).astype(BF16)

    za = jnp.dot(hn, wa_ref[...], preferred_element_type=F32)
    zq = za[:, :Q_RANK]
    zkv = za[:, Q_RANK:Q_RANK + KV_RANK]
    zkr = za[:, Q_RANK + KV_RANK:]
    cq = _rms(zq, qn_ref[...]).astype(BF16)
    ckv = _rms(zkv, kvn_ref[...]).astype(BF16)

    qt = lax.dot_general(wuqt_ref[...], cq, NT_DIMS, preferred_element_type=F32)
    scale = QK_SCALE
    cos, sin = qcos_ref[...], qsin_ref[...]
    half = ROPE // 2
    for h in range(HEADS):
        r0 = h * SLAB
        x1 = qt[r0 + NOPE:r0 + NOPE + half]
        x2 = qt[r0 + NOPE + half:r0 + NOPE + ROPE]
        qt_out[0, r0:r0 + NOPE, :] = (qt[r0:r0 + NOPE] * scale).astype(BF16)
        qt_out[0, r0 + NOPE:r0 + NOPE + half, :] = (x1 * cos - x2 * sin).astype(BF16)
        qt_out[0, r0 + NOPE + half:r0 + NOPE + ROPE, :] = (x1 * sin + x2 * cos).astype(BF16)
        qt_out[0, r0 + NOPE + ROPE:r0 + SLAB, :] = jnp.zeros((SLAB - NOPE - ROPE, t), BF16)

    kf = jnp.dot(ckv, wuk_ref[...], preferred_element_type=F32)
    kr = (zkr * kc_ref[...] + pltpu.roll(zkr, LANES - half, axis=1) * ks1_ref[...]
          + pltpu.roll(zkr, half, axis=1) * ks2_ref[...])
    for h in range(HEADS):
        sl = slice(h * SLAB, (h + 1) * SLAB)
        k_out[0, :, sl] = (kf[:, sl] + kr).astype(BF16)

    vt = lax.dot_general(wuvt_ref[...], ckv, NT_DIMS, preferred_element_type=F32)
    for c in range(t // ATTN_TK):
        vt_out[0, :, c] = vt[:, c * ATTN_TK:(c + 1) * ATTN_TK].reshape(
            HEADS // HEADS_PER_STEP, LANES, ATTN_TK).astype(BF16)

    zg = jnp.dot(hn, wg_ref[:, :3 * POOL_WIDTH], preferred_element_type=F32)
    g_attn = zg[:, :MLA_WIDTH]
    u = zg[:, MLA_WIDTH:MLA_WIDTH + POOL_WIDTH]
    g_pool = zg[:, MLA_WIDTH + POOL_WIDTH:]
    sga_out[0] = (g_attn * _sigmoid(g_attn)).astype(BF16)
    for c in range(4):
        lo = 3 * POOL_WIDTH + c * 512
        gm = jnp.dot(hn, wg_ref[:, lo:lo + 512], preferred_element_type=F32)
        gates_out[0, :, c * 512:(c + 1) * 512] = _sigmoid(gm).astype(BF16)

    @pl.when(si == 0)
    def _():
        carry_ref[...] = jnp.zeros_like(carry_ref)

    ue = jnp.concatenate([carry_ref[...], u], axis=0)
    carry_ref[...] = u[t - HALO:, :]
    pos = si * t + lax.broadcasted_iota(jnp.int32, (t, 1), 0) + 1
    outs = []
    for gi, w in enumerate(POOL_WINDOWS):
        sl = slice(gi * POOL_GROUP, (gi + 1) * POOL_GROUP)
        acc = ue[:, sl]
        step = 1
        while step < w:
            acc = acc + pltpu.roll(acc, step, axis=0)
            step *= 2
        inv = 1.0 / jnp.minimum(pos, w).astype(F32)
        d = acc[HALO:, :] * inv - u[:, sl]
        outs.append(jnp.dot(d.astype(BF16), poolw_ref[gi], preferred_element_type=F32))
    y = jnp.concatenate(outs, axis=-1) * pscale_ref[...]
    ypool_out[0] = (y * (g_pool * _sigmoid(g_pool))).astype(BF16)


def _attn_kernel(qt_ref, k_ref, vt_ref, o_ref, qbd_ref, sa_ref, sb_ref, m_ref, l_ref, acc_ref):
    tq, tk = ATTN_TQ, ATTN_TK
    nq = HEADS_PER_STEP * tq
    i = pl.program_id(2)
    last = ((i + 1) * tq + tk - 1) // tk - 1
    key_chunk = lax.shift_right_logical(lax.broadcasted_iota(jnp.int32, (tk, nq), 0), 6)
    qry_col = lax.broadcasted_iota(jnp.int32, (tk, nq), 1) & (tq - 1)
    qry_chunk = lax.shift_right_logical(qry_col, 6)
    diag_mask = key_chunk + (last * (tk // CHUNK) - i * (tq // CHUNK)) <= qry_chunk

    zero = jnp.zeros((SLAB, tq), BF16)
    qbd_ref[:SLAB, :tq] = qt_ref[0, :SLAB, :]
    qbd_ref[:SLAB, tq:] = zero
    qbd_ref[SLAB:, :tq] = zero
    qbd_ref[SLAB:, tq:] = qt_ref[0, SLAB:, :]
    m_ref[...] = jnp.full_like(m_ref, -jnp.inf)
    l_ref[...] = jnp.zeros_like(l_ref)
    acc_ref[...] = jnp.zeros_like(acc_ref)

    def scores(j):
        kt = k_ref[0, pl.ds(pl.multiple_of(j * tk, tk), tk), :]
        return jnp.dot(kt, qbd_ref[...], preferred_element_type=F32)

    def process(s_ref, j, masked):
        s = s_ref[...]
        if masked:
            s = jnp.where(diag_mask, s, -jnp.inf)
        m_old = m_ref[...]
        m_new = jnp.maximum(m_old, jnp.max(s, axis=0, keepdims=True))
        p = jnp.exp2(s - m_new)
        alpha = jnp.exp2(m_old - m_new)
        l_ref[...] = alpha * l_ref[...] + jnp.sum(p, axis=0, keepdims=True)
        m_ref[...] = m_new
        pv = jnp.dot(vt_ref[0, 0, j], p.astype(BF16), preferred_element_type=F32)
        for h in range(HEADS_PER_STEP):
            cols = slice(h * tq, (h + 1) * tq)
            acc_ref[h] = alpha[:, cols] * acc_ref[h] + pv[h * VDIM:(h + 1) * VDIM, cols]

    sa_ref[...] = scores(0)

    def pair(jj, carry):
        j = 2 * jj
        sb_ref[...] = scores(j + 1)
        process(sa_ref, j, False)
        sa_ref[...] = scores(j + 2)
        process(sb_ref, j + 1, False)
        return carry

    lax.fori_loop(0, last // 2, pair, 0)

    @pl.when(last % 2 == 1)
    def _():
        sb_ref[...] = scores(last)
        process(sa_ref, last - 1, False)
        process(sb_ref, last, True)

    @pl.when(last % 2 == 0)
    def _():
        process(sa_ref, last, True)

    inv_l = 1.0 / l_ref[...]
    ot = jnp.concatenate([acc_ref[h] * inv_l[:, h * tq:(h + 1) * tq] for h in range(HEADS_PER_STEP)],
                         axis=0)
    o_ref[0] = ot.T.astype(BF16)


def _merge_kernel(x_ref, o_ref, sga_ref, ypool_ref, gates_ref, wa_ref, wp_ref, wo_ref, nf_ref, out_ref):
    ya = (o_ref[0].astype(F32) * sga_ref[0].astype(F32)).astype(BF16)
    a = jnp.dot(ya, wa_ref[...], preferred_element_type=F32)
    p = jnp.dot(ypool_ref[0], wp_ref[...], preferred_element_type=F32)
    ga = gates_ref[0, :, :D_MODEL].astype(F32)
    gp = gates_ref[0, :, D_MODEL:].astype(F32)
    merged = (ga * a + gp * p).astype(BF16)
    h = x_ref[0] + jnp.dot(merged, wo_ref[...], preferred_element_type=F32)
    out_ref[0] = _rms(h, nf_ref[...])


def _rope_angles(seq):
    half = ROPE // 2
    inv_freq = ROPE_THETA ** (-jnp.arange(half, dtype=F32) / half)
    ang = jnp.arange(seq, dtype=F32)[:, None] * inv_freq[None, :]
    return jnp.cos(ang), jnp.sin(ang)


def _key_rope_tables(cos, sin):
    seq, half = cos.shape
    zeros_n = jnp.zeros((seq, NOPE), F32)
    zeros_h = jnp.zeros((seq, half), F32)
    zeros_p = jnp.zeros((seq, SLAB - NOPE - ROPE), F32)
    c = jnp.concatenate([zeros_n, cos, cos, zeros_p], axis=1)
    s1 = jnp.concatenate([zeros_n, -sin, zeros_h, zeros_p], axis=1)
    s2 = jnp.concatenate([zeros_n, zeros_h, sin, zeros_p], axis=1)
    return c, s1, s2


def _const_spec(shape):
    return pl.BlockSpec(shape, lambda *_: (0,) * len(shape), pipeline_mode=pl.Buffered(1))


def kernel(x, norm_in, w_in, q_norm, w_uq, kv_norm, w_ukv, pool_w, pool_scale,
           w_branch_attn, w_branch_pool, w_out, norm_final):
    b, s, d = x.shape
    tq, tk, t1, t3 = ATTN_TQ, ATTN_TK, PROJ_TILE, MERGE_TILE
    assert d == D_MODEL and s % t1 == 0 and t1 % tk == 0 and tk % tq == 0 and s % t3 == 0
    hp = HEADS // HEADS_PER_STEP

    o_kv = Q_RANK
    o_kr = o_kv + KV_RANK
    o_ga = o_kr + ROPE
    w_kr_slab = jnp.concatenate([jnp.zeros((d, NOPE), F32), w_in[:, o_kr:o_ga],
                                 jnp.zeros((d, SLAB - NOPE - ROPE), F32)], axis=1)
    w_a = jnp.concatenate([w_in[:, :o_kr], w_kr_slab], axis=1).astype(BF16)
    w_g = w_in[:, o_ga:].astype(BF16)
    w_uq_p = jnp.pad(w_uq, ((0, 0), (0, 0), (0, SLAB - NOPE - ROPE)))
    w_uqt = w_uq_p.reshape(Q_RANK, HEADS * SLAB).T.astype(BF16)
    w_uk_p = jnp.pad(w_ukv[:, :, :NOPE], ((0, 0), (0, 0), (0, SLAB - NOPE)))
    w_uk = w_uk_p.reshape(KV_RANK, HEADS * SLAB).astype(BF16)
    w_uvt = w_ukv[:, :, NOPE:].reshape(KV_RANK, MLA_WIDTH).T.astype(BF16)
    cos, sin = _rope_angles(s)
    qcos, qsin = (cos * QK_SCALE).T, (sin * QK_SCALE).T
    kc, ks1, ks2 = _key_rope_tables(cos, sin)

    row = lambda v: v.reshape(1, -1).astype(F32)
    tok = lambda width: pl.BlockSpec((1, t1, width), lambda bi, si: (bi, si, 0))
    tab = pl.BlockSpec((t1, SLAB), lambda bi, si: (si, 0))
    tabt = pl.BlockSpec((ROPE // 2, t1), lambda bi, si: (0, si))
    qt, k, vt, sga, ypool, gates = pl.pallas_call(
        _proj_kernel,
        grid=(b, s // t1),
        in_specs=[tok(d), _const_spec((1, d)), _const_spec(w_a.shape), _const_spec(w_g.shape),
                  _const_spec((1, Q_RANK)), _const_spec(w_uqt.shape), _const_spec((1, KV_RANK)),
                  _const_spec(w_uk.shape), _const_spec(w_uvt.shape),
                  _const_spec(pool_w.shape), _const_spec((1, POOL_WIDTH)),
                  tabt, tabt, tab, tab, tab],
        out_specs=[pl.BlockSpec((1, HEADS * SLAB, t1), lambda bi, si: (bi, 0, si)),
                   tok(HEADS * SLAB),
                   pl.BlockSpec((1, hp, t1 // tk, LANES, tk), lambda bi, si: (bi, 0, si, 0, 0)),
                   tok(MLA_WIDTH), tok(POOL_WIDTH), tok(2 * d)],
        out_shape=[jax.ShapeDtypeStruct((b, HEADS * SLAB, s), BF16),
                   jax.ShapeDtypeStruct((b, s, HEADS * SLAB), BF16),
                   jax.ShapeDtypeStruct((b, hp, s // tk, LANES, tk), BF16),
                   jax.ShapeDtypeStruct((b, s, MLA_WIDTH), BF16),
                   jax.ShapeDtypeStruct((b, s, POOL_WIDTH), BF16),
                   jax.ShapeDtypeStruct((b, s, 2 * d), BF16)],
        scratch_shapes=[pltpu.VMEM((HALO, POOL_WIDTH), F32)],
        compiler_params=pltpu.CompilerParams(
            dimension_semantics=("arbitrary", "arbitrary"), vmem_limit_bytes=VMEM_LIMIT),
        name="proj",
    )(x, row(norm_in), w_a, w_g, row(q_norm), w_uqt, row(kv_norm), w_uk, w_uvt,
      pool_w.astype(BF16), row(pool_scale), qcos, qsin, kc, ks1, ks2)

    o = pl.pallas_call(
        _attn_kernel,
        grid=(b, hp, s // tq),
        in_specs=[pl.BlockSpec((1, HEADS_PER_STEP * SLAB, tq), lambda bi, pi, i: (bi, pi, i)),
                  pl.BlockSpec((1, s, HEADS_PER_STEP * SLAB), lambda bi, pi, i: (bi, 0, pi)),
                  pl.BlockSpec((1, 1, s // tk, LANES, tk), lambda bi, pi, i: (bi, pi, 0, 0, 0))],
        out_specs=pl.BlockSpec((1, tq, LANES), lambda bi, pi, i: (bi, i, pi)),
        out_shape=jax.ShapeDtypeStruct((b, s, MLA_WIDTH), BF16),
        scratch_shapes=[pltpu.VMEM((HEADS_PER_STEP * SLAB, HEADS_PER_STEP * tq), BF16),
                        pltpu.VMEM((tk, HEADS_PER_STEP * tq), F32),
                        pltpu.VMEM((tk, HEADS_PER_STEP * tq), F32),
                        pltpu.VMEM((1, HEADS_PER_STEP * tq), F32),
                        pltpu.VMEM((1, HEADS_PER_STEP * tq), F32),
                        pltpu.VMEM((HEADS_PER_STEP, VDIM, tq), F32)],
        compiler_params=pltpu.CompilerParams(
            dimension_semantics=("arbitrary", "arbitrary", "arbitrary"), vmem_limit_bytes=VMEM_LIMIT),
        name="attn",
    )(qt, k, vt)

    tok3 = lambda width: pl.BlockSpec((1, t3, width), lambda bi, si: (bi, si, 0))
    out = pl.pallas_call(
        _merge_kernel,
        grid=(b, s // t3),
        in_specs=[tok3(d), tok3(MLA_WIDTH), tok3(MLA_WIDTH), tok3(POOL_WIDTH), tok3(2 * d),
                  _const_spec((MLA_WIDTH, d)), _const_spec((POOL_WIDTH, d)), _const_spec((d, d)),
                  _const_spec((1, d))],
        out_specs=tok3(d),
        out_shape=jax.ShapeDtypeStruct((b, s, d), x.dtype),
        compiler_params=pltpu.CompilerParams(
            dimension_semantics=("arbitrary", "arbitrary"), vmem_limit_bytes=VMEM_LIMIT),
        name="merge",
    )(x, o, sga, ypool, gates, w_branch_attn.astype(BF16), w_branch_pool.astype(BF16),
      w_out.astype(BF16), row(norm_final))
    return out
```

```python
import jax
import jax.numpy as jnp
from jax import lax
from jax.experimental import pallas as pl
from jax.experimental.pallas import tpu as pltpu

D_MODEL = 1024
CHUNK = 64
HEADS = 8
NOPE = 64
ROPE = 32
VDIM = 64
Q_RANK = 384
KV_RANK = 256
MLA_WIDTH = HEADS * VDIM
ROPE_THETA = 10000.0
POOL_WINDOWS = (2, 4, 8, 16)
POOL_WIDTH = D_MODEL // 2
POOL_GROUP = POOL_WIDTH // len(POOL_WINDOWS)
EPS = 1e-6
QK_SCALE = (NOPE + ROPE) ** -0.5 * 1.4426950408889634

LANES = 128
HALO = 16
SLAB = LANES
HEADS_PER_STEP = 2
ONES_ROWS = 16
VT_ROWS = HEADS_PER_STEP * VDIM + ONES_ROWS
PROJ_TILE = 512
ATTN_TQ = 256
ATTN_TK = 512
MERGE_TILE = 512
VMEM_LIMIT = 56 * 1024 * 1024

BF16 = jnp.bfloat16
F32 = jnp.float32
NT_DIMS = (((1,), (1,)), ((), ()))


def _rms(x, g):
    return x * lax.rsqrt(jnp.mean(x * x, axis=-1, keepdims=True) + EPS) * g


def _sigmoid(x):
    return 1.0 / (1.0 + jnp.exp(-x))


def _proj_kernel(x_ref, nin_ref, wa_ref, wg_ref, qn_ref, wuqt_ref, kvn_ref, wuk_ref, wuvt_ref,
                 poolw_ref, pscale_ref, qcos_ref, qsin_ref, kc_ref, ks1_ref, ks2_ref,
                 qt_out, k_out, vt_out, sga_out, ypool_out, gates_out, carry_ref):
    t = PROJ_TILE
    si = pl.program_id(1)
    x = x_ref[0]
    hn = _rms(x, nin_ref[...]).astype(BF16)

    za = jnp.dot(hn, wa_ref[...], preferred_element_type=F32)
    zq = za[:, :Q_RANK]
    zkv = za[:, Q_RANK:Q_RANK + KV_RANK]
    zkr = za[:, Q_RANK + KV_RANK:]
    cq = _rms(zq, qn_ref[...]).astype(BF16)
    ckv = _rms(zkv, kvn_ref[...]).astype(BF16)

    qt = lax.dot_general(wuqt_ref[...], cq, NT_DIMS, preferred_element_type=F32)
    scale = QK_SCALE
    cos, sin = qcos_ref[...], qsin_ref[...]
    half = ROPE // 2
    tq = ATTN_TQ
    for h in range(HEADS):
        r0 = h * SLAB
        x1 = qt[r0 + NOPE:r0 + NOPE + half]
        x2 = qt[r0 + NOPE + half:r0 + NOPE + ROPE]
        slab = jnp.concatenate([qt[r0:r0 + NOPE] * scale, x1 * cos - x2 * sin, x1 * sin + x2 * cos,
                                jnp.zeros((SLAB - NOPE - ROPE, t), F32)], axis=0).astype(BF16)
        pr, hr = h // HEADS_PER_STEP, (h % HEADS_PER_STEP) * SLAB
        for c in range(t // tq):
            qt_out[0, pr, c, hr:hr + SLAB, :] = slab[:, c * tq:(c + 1) * tq]

    kf = jnp.dot(ckv, wuk_ref[...], preferred_element_type=F32)
    kr = (zkr * kc_ref[...] + pltpu.roll(zkr, LANES - half, axis=1) * ks1_ref[...]
          + pltpu.roll(zkr, half, axis=1) * ks2_ref[...])
    for h in range(HEADS):
        sl = slice(h * SLAB, (h + 1) * SLAB)
        k_out[0, :, sl] = (kf[:, sl] + kr).astype(BF16)

    vt = lax.dot_general(wuvt_ref[...], ckv, NT_DIMS, preferred_element_type=F32)
    for c in range(t // ATTN_TK):
        vt_out[0, :, c, :LANES, :] = vt[:, c * ATTN_TK:(c + 1) * ATTN_TK].reshape(
            HEADS // HEADS_PER_STEP, LANES, ATTN_TK).astype(BF16)
        vt_out[0, :, c, LANES:, :] = jnp.ones((HEADS // HEADS_PER_STEP, ONES_ROWS, ATTN_TK), BF16)

    zg = jnp.dot(hn, wg_ref[:, :3 * POOL_WIDTH], preferred_element_type=F32)
    g_attn = zg[:, :MLA_WIDTH]
    u = zg[:, MLA_WIDTH:MLA_WIDTH + POOL_WIDTH]
    g_pool = zg[:, MLA_WIDTH + POOL_WIDTH:]
    sga_out[0] = (g_attn * _sigmoid(g_attn)).astype(BF16)
    for c in range(4):
        lo = 3 * POOL_WIDTH + c * 512
        gm = jnp.dot(hn, wg_ref[:, lo:lo + 512], preferred_element_type=F32)
        gates_out[0, :, c * 512:(c + 1) * 512] = _sigmoid(gm).astype(BF16)

    @pl.when(si == 0)
    def _():
        carry_ref[...] = jnp.zeros_like(carry_ref)

    ue = jnp.concatenate([carry_ref[...], u], axis=0)
    carry_ref[...] = u[t - HALO:, :]
    pos = si * t + lax.broadcasted_iota(jnp.int32, (t, 1), 0) + 1
    outs = []
    for gi, w in enumerate(POOL_WINDOWS):
        sl = slice(gi * POOL_GROUP, (gi + 1) * POOL_GROUP)
        acc = ue[:, sl]
        step = 1
        while step < w:
            acc = acc + pltpu.roll(acc, step, axis=0)
            step *= 2
        inv = 1.0 / jnp.minimum(pos, w).astype(F32)
        d = acc[HALO:, :] * inv - u[:, sl]
        outs.append(jnp.dot(d.astype(BF16), poolw_ref[gi], preferred_element_type=F32))
    y = jnp.concatenate(outs, axis=-1) * pscale_ref[...]
    ypool_out[0] = (y * (g_pool * _sigmoid(g_pool))).astype(BF16)


def _attn_kernel(qt_ref, k_ref, vt_ref, o_ref, qcur_ref, qnxt_ref, sa_ref, sb_ref, m_ref, l_ref, acc_ref):
    tq, tk = ATTN_TQ, ATTN_TK
    nq = HEADS_PER_STEP * tq
    n_qt = qt_ref.shape[2]
    kblocks = tk // CHUNK
    qry_chunk = lax.shift_right_logical(lax.broadcasted_iota(jnp.int32, (1, 1, nq), 2) & (tq - 1), 6)
    key_block = lax.broadcasted_iota(jnp.int32, (kblocks, 1, 1), 0)

    def build_qbd(dst_ref, i):
        zero = jnp.zeros((SLAB, tq), BF16)
        dst_ref[:SLAB, :tq] = qt_ref[0, 0, i, :SLAB, :]
        dst_ref[:SLAB, tq:] = zero
        dst_ref[SLAB:, :tq] = zero
        dst_ref[SLAB:, tq:] = qt_ref[0, 0, i, SLAB:, :]

    def scores(qbd_ref, j):
        kt = k_ref[0, pl.ds(pl.multiple_of(j * tk, tk), tk), :]
        return jnp.dot(kt, qbd_ref[...], preferred_element_type=F32)

    def process(s_ref, j, bias):
        s = s_ref[...]
        if bias is not None:
            s = (s.reshape(kblocks, CHUNK, nq) + bias).reshape(tk, nq)
        m_old = m_ref[...]
        m_new = jnp.maximum(m_old, jnp.max(s, axis=0, keepdims=True))
        p = jnp.exp2(s - m_new).astype(BF16)
        alpha = jnp.exp2(m_old - m_new)
        m_ref[...] = m_new
        pv = jnp.dot(vt_ref[0, 0, j], p, preferred_element_type=F32)
        l_ref[...] = alpha * l_ref[...] + pv[LANES:LANES + 1, :]
        for h in range(HEADS_PER_STEP):
            cols = slice(h * tq, (h + 1) * tq)
            acc_ref[h] = alpha[:, cols] * acc_ref[h] + pv[h * VDIM:(h + 1) * VDIM, cols]

    build_qbd(qnxt_ref, 0)
    sa_ref[...] = scores(qnxt_ref, 0)

    def query_tile(i, carry):
        last = ((i + 1) * tq + tk - 1) // tk - 1
        build_qbd(qcur_ref, i)
        build_qbd(qnxt_ref, jnp.minimum(i + 1, n_qt - 1))
        m_ref[...] = jnp.full_like(m_ref, -jnp.inf)
        l_ref[...] = jnp.zeros_like(l_ref)
        acc_ref[...] = jnp.zeros_like(acc_ref)
        allowed = key_block + (last * kblocks - i * (tq // CHUNK)) <= qry_chunk
        bias = jnp.where(allowed, 0.0, -jnp.inf).astype(F32)

        def pair(jj, c):
            j = 2 * jj
            sb_ref[...] = scores(qcur_ref, j + 1)
            process(sa_ref, j, None)
            sa_ref[...] = scores(qcur_ref, j + 2)
            process(sb_ref, j + 1, None)
            return c

        lax.fori_loop(0, last // 2, pair, 0)

        @pl.when(last % 2 == 1)
        def _():
            sb_ref[...] = scores(qcur_ref, last)
            process(sa_ref, last - 1, None)
            sa_ref[...] = scores(qnxt_ref, 0)
            process(sb_ref, last, bias)

        @pl.when(last % 2 == 0)
        def _():
            process(sa_ref, last, bias)
            sa_ref[...] = scores(qnxt_ref, 0)

        inv_l = 1.0 / l_ref[...]
        ot = jnp.concatenate([acc_ref[h] * inv_l[:, h * tq:(h + 1) * tq] for h in range(HEADS_PER_STEP)],
                             axis=0)
        o_ref[0, pl.ds(pl.multiple_of(i * tq, tq), tq), :] = ot.T.astype(BF16)
        return carry

    lax.fori_loop(0, n_qt, query_tile, 0)


def _merge_kernel(x_ref, o_ref, sga_ref, ypool_ref, gates_ref, wa_ref, wp_ref, wo_ref, nf_ref, out_ref):
    ya = (o_ref[0].astype(F32) * sga_ref[0].astype(F32)).astype(BF16)
    a = jnp.dot(ya, wa_ref[...], preferred_element_type=F32)
    p = jnp.dot(ypool_ref[0], wp_ref[...], preferred_element_type=F32)
    ga = gates_ref[0, :, :D_MODEL].astype(F32)
    gp = gates_ref[0, :, D_MODEL:].astype(F32)
    merged = (ga * a + gp * p).astype(BF16)
    h = x_ref[0] + jnp.dot(merged, wo_ref[...], preferred_element_type=F32)
    out_ref[0] = _rms(h, nf_ref[...])


def _rope_angles(seq):
    half = ROPE // 2
    inv_freq = ROPE_THETA ** (-jnp.arange(half, dtype=F32) / half)
    ang = jnp.arange(seq, dtype=F32)[:, None] * inv_freq[None, :]
    return jnp.cos(ang), jnp.sin(ang)


def _key_rope_tables(cos, sin):
    seq, half = cos.shape
    zeros_n = jnp.zeros((seq, NOPE), F32)
    zeros_h = jnp.zeros((seq, half), F32)
    zeros_p = jnp.zeros((seq, SLAB - NOPE - ROPE), F32)
    c = jnp.concatenate([zeros_n, cos, cos, zeros_p], axis=1)
    s1 = jnp.concatenate([zeros_n, -sin, zeros_h, zeros_p], axis=1)
    s2 = jnp.concatenate([zeros_n, zeros_h, sin, zeros_p], axis=1)
    return c, s1, s2


def _const_spec(shape):
    return pl.BlockSpec(shape, lambda *_: (0,) * len(shape), pipeline_mode=pl.Buffered(1))


def kernel(x, norm_in, w_in, q_norm, w_uq, kv_norm, w_ukv, pool_w, pool_scale,
           w_branch_attn, w_branch_pool, w_out, norm_final):
    b, s, d = x.shape
    tq, tk, t1, t3 = ATTN_TQ, ATTN_TK, PROJ_TILE, MERGE_TILE
    assert d == D_MODEL and s % t1 == 0 and t1 % tk == 0 and tk % tq == 0 and s % t3 == 0
    hp = HEADS // HEADS_PER_STEP

    o_kv = Q_RANK
    o_kr = o_kv + KV_RANK
    o_ga = o_kr + ROPE
    w_kr_slab = jnp.concatenate([jnp.zeros((d, NOPE), F32), w_in[:, o_kr:o_ga],
                                 jnp.zeros((d, SLAB - NOPE - ROPE), F32)], axis=1)
    w_a = jnp.concatenate([w_in[:, :o_kr], w_kr_slab], axis=1).astype(BF16)
    w_g = w_in[:, o_ga:].astype(BF16)
    w_uq_p = jnp.pad(w_uq, ((0, 0), (0, 0), (0, SLAB - NOPE - ROPE)))
    w_uqt = w_uq_p.reshape(Q_RANK, HEADS * SLAB).T.astype(BF16)
    w_uk_p = jnp.pad(w_ukv[:, :, :NOPE], ((0, 0), (0, 0), (0, SLAB - NOPE)))
    w_uk = w_uk_p.reshape(KV_RANK, HEADS * SLAB).astype(BF16)
    w_uvt = w_ukv[:, :, NOPE:].reshape(KV_RANK, MLA_WIDTH).T.astype(BF16)
    cos, sin = _rope_angles(s)
    qcos, qsin = (cos * QK_SCALE).T, (sin * QK_SCALE).T
    kc, ks1, ks2 = _key_rope_tables(cos, sin)

    row = lambda v: v.reshape(1, -1).astype(F32)
    tok = lambda width: pl.BlockSpec((1, t1, width), lambda bi, si: (bi, si, 0))
    tab = pl.BlockSpec((t1, SLAB), lambda bi, si: (si, 0))
    tabt = pl.BlockSpec((ROPE // 2, t1), lambda bi, si: (0, si))
    qt, k, vt, sga, ypool, gates = pl.pallas_call(
        _proj_kernel,
        grid=(b, s // t1),
        in_specs=[tok(d), _const_spec((1, d)), _const_spec(w_a.shape), _const_spec(w_g.shape),
                  _const_spec((1, Q_RANK)), _const_spec(w_uqt.shape), _const_spec((1, KV_RANK)),
                  _const_spec(w_uk.shape), _const_spec(w_uvt.shape),
                  _const_spec(pool_w.shape), _const_spec((1, POOL_WIDTH)),
                  tabt, tabt, tab, tab, tab],
        out_specs=[pl.BlockSpec((1, hp, t1 // tq, HEADS_PER_STEP * SLAB, tq),
                                lambda bi, si: (bi, 0, si, 0, 0)),
                   tok(HEADS * SLAB),
                   pl.BlockSpec((1, hp, t1 // tk, VT_ROWS, tk), lambda bi, si: (bi, 0, si, 0, 0)),
                   tok(MLA_WIDTH), tok(POOL_WIDTH), tok(2 * d)],
        out_shape=[jax.ShapeDtypeStruct((b, hp, s // tq, HEADS_PER_STEP * SLAB, tq), BF16),
                   jax.ShapeDtypeStruct((b, s, HEADS * SLAB), BF16),
                   jax.ShapeDtypeStruct((b, hp, s // tk, VT_ROWS, tk), BF16),
                   jax.ShapeDtypeStruct((b, s, MLA_WIDTH), BF16),
                   jax.ShapeDtypeStruct((b, s, POOL_WIDTH), BF16),
                   jax.ShapeDtypeStruct((b, s, 2 * d), BF16)],
        scratch_shapes=[pltpu.VMEM((HALO, POOL_WIDTH), F32)],
        compiler_params=pltpu.CompilerParams(
            dimension_semantics=("arbitrary", "arbitrary"), vmem_limit_bytes=VMEM_LIMIT),
        name="proj",
    )(x, row(norm_in), w_a, w_g, row(q_norm), w_uqt, row(kv_norm), w_uk, w_uvt,
      pool_w.astype(BF16), row(pool_scale), qcos, qsin, kc, ks1, ks2)

    o = pl.pallas_call(
        _attn_kernel,
        grid=(b, hp),
        in_specs=[pl.BlockSpec((1, 1, s // tq, HEADS_PER_STEP * SLAB, tq), lambda bi, pi: (bi, pi, 0, 0, 0)),
                  pl.BlockSpec((1, s, HEADS_PER_STEP * SLAB), lambda bi, pi: (bi, 0, pi)),
                  pl.BlockSpec((1, 1, s // tk, VT_ROWS, tk), lambda bi, pi: (bi, pi, 0, 0, 0))],
        out_specs=pl.BlockSpec((1, s, LANES), lambda bi, pi: (bi, 0, pi)),
        out_shape=jax.ShapeDtypeStruct((b, s, MLA_WIDTH), BF16),
        scratch_shapes=[pltpu.VMEM((HEADS_PER_STEP * SLAB, HEADS_PER_STEP * tq), BF16),
                        pltpu.VMEM((HEADS_PER_STEP * SLAB, HEADS_PER_STEP * tq), BF16),
                        pltpu.VMEM((tk, HEADS_PER_STEP * tq), F32),
                        pltpu.VMEM((tk, HEADS_PER_STEP * tq), F32),
                        pltpu.VMEM((1, HEADS_PER_STEP * tq), F32),
                        pltpu.VMEM((1, HEADS_PER_STEP * tq), F32),
                        pltpu.VMEM((HEADS_PER_STEP, VDIM, tq), F32)],
        compiler_params=pltpu.CompilerParams(
            dimension_semantics=("arbitrary", "arbitrary"), vmem_limit_bytes=VMEM_LIMIT),
        name="attn",
    )(qt, k, vt)

    tok3 = lambda width: pl.BlockSpec((1, t3, width), lambda bi, si: (bi, si, 0))
    out = pl.pallas_call(
        _merge_kernel,
        grid=(b, s // t3),
        in_specs=[tok3(d), tok3(MLA_WIDTH), tok3(MLA_WIDTH), tok3(POOL_WIDTH), tok3(2 * d),
                  _const_spec((MLA_WIDTH, d)), _const_spec((POOL_WIDTH, d)), _const_spec((d, d)),
                  _const_spec((1, d))],
        out_specs=tok3(d),
        out_shape=jax.ShapeDtypeStruct((b, s, d), x.dtype),
        compiler_params=pltpu.CompilerParams(
            dimension_semantics=("arbitrary", "arbitrary"), vmem_limit_bytes=VMEM_LIMIT),
        name="merge",
    )(x, o, sga, ypool, gates, w_branch_attn.astype(BF16), w_branch_pool.astype(BF16),
      w_out.astype(BF16), row(norm_final))
    return out
```

```python
import jax
import jax.numpy as jnp
from jax import lax
from jax.experimental import pallas as pl
from jax.experimental.pallas import tpu as pltpu

D_MODEL = 1024
CHUNK = 64
HEADS = 8
NOPE = 64
ROPE = 32
VDIM = 64
Q_RANK = 384
KV_RANK = 256
MLA_WIDTH = HEADS * VDIM
ROPE_THETA = 10000.0
POOL_WINDOWS = (2, 4, 8, 16)
POOL_WIDTH = D_MODEL // 2
POOL_GROUP = POOL_WIDTH // len(POOL_WINDOWS)
EPS = 1e-6
QK_SCALE = (NOPE + ROPE) ** -0.5 * 1.4426950408889634

LANES = 128
HALO = 16
SLAB = LANES
HEADS_PER_STEP = 2
ONES_ROWS = 16
VT_ROWS = HEADS_PER_STEP * VDIM + ONES_ROWS
PROJ_TILE = 512
ATTN_TQ = 512
ATTN_TK = 512
MERGE_TILE = 512
VMEM_LIMIT = 56 * 1024 * 1024

BF16 = jnp.bfloat16
F32 = jnp.float32
NT_DIMS = (((1,), (1,)), ((), ()))


def _rms(x, g):
    return x * lax.rsqrt(jnp.mean(x * x, axis=-1, keepdims=True) + EPS) * g


def _sigmoid(x):
    return 1.0 / (1.0 + jnp.exp(-x))


def _proj_kernel(x_ref, nin_ref, wa_ref, wg_ref, qn_ref, wuqt_ref, kvn_ref, wuk_ref, wuvt_ref,
                 poolw_ref, pscale_ref, qcos_ref, qsin_ref, kc_ref, ks1_ref, ks2_ref,
                 qt_out, k_out, vt_out, sga_out, ypool_out, gates_out, carry_ref):
    t = PROJ_TILE
    si = pl.program_id(1)
    x = x_ref[0]
    hn = _rms(x, nin_ref[...]).astype(BF16)

    za = jnp.dot(hn, wa_ref[...], preferred_element_type=F32)
    zq = za[:, :Q_RANK]
    zkv = za[:, Q_RANK:Q_RANK + KV_RANK]
    zkr = za[:, Q_RANK + KV_RANK:]
    cq = _rms(zq, qn_ref[...]).astype(BF16)
    ckv = _rms(zkv, kvn_ref[...]).astype(BF16)

    qt = lax.dot_general(wuqt_ref[...], cq, NT_DIMS, preferred_element_type=F32)
    scale = QK_SCALE
    cos, sin = qcos_ref[...], qsin_ref[...]
    half = ROPE // 2
    tq = ATTN_TQ
    for h in range(HEADS):
        r0 = h * SLAB
        x1 = qt[r0 + NOPE:r0 + NOPE + half]
        x2 = qt[r0 + NOPE + half:r0 + NOPE + ROPE]
        slab = jnp.concatenate([qt[r0:r0 + NOPE] * scale, x1 * cos - x2 * sin, x1 * sin + x2 * cos,
                                jnp.zeros((SLAB - NOPE - ROPE, t), F32)], axis=0).astype(BF16)
        pr, hr = h // HEADS_PER_STEP, (h % HEADS_PER_STEP) * SLAB
        for c in range(t // tq):
            qt_out[0, pr, c, hr:hr + SLAB, :] = slab[:, c * tq:(c + 1) * tq]

    kf = jnp.dot(ckv, wuk_ref[...], preferred_element_type=F32)
    kr = (zkr * kc_ref[...] + pltpu.roll(zkr, LANES - half, axis=1) * ks1_ref[...]
          + pltpu.roll(zkr, half, axis=1) * ks2_ref[...])
    for h in range(HEADS):
        sl = slice(h * SLAB, (h + 1) * SLAB)
        k_out[0, :, sl] = (kf[:, sl] + kr).astype(BF16)

    vt = lax.dot_general(wuvt_ref[...], ckv, NT_DIMS, preferred_element_type=F32)
    for c in range(t // ATTN_TK):
        vt_out[0, :, c, :LANES, :] = vt[:, c * ATTN_TK:(c + 1) * ATTN_TK].reshape(
            HEADS // HEADS_PER_STEP, LANES, ATTN_TK).astype(BF16)
        vt_out[0, :, c, LANES:, :] = jnp.ones((HEADS // HEADS_PER_STEP, ONES_ROWS, ATTN_TK), BF16)

    zg = jnp.dot(hn, wg_ref[:, :3 * POOL_WIDTH], preferred_element_type=F32)
    g_attn = zg[:, :MLA_WIDTH]
    u = zg[:, MLA_WIDTH:MLA_WIDTH + POOL_WIDTH]
    g_pool = zg[:, MLA_WIDTH + POOL_WIDTH:]
    sga_out[0] = (g_attn * _sigmoid(g_attn)).astype(BF16)
    for c in range(4):
        lo = 3 * POOL_WIDTH + c * 512
        gm = jnp.dot(hn, wg_ref[:, lo:lo + 512], preferred_element_type=F32)
        gates_out[0, :, c * 512:(c + 1) * 512] = _sigmoid(gm).astype(BF16)

    @pl.when(si == 0)
    def _():
        carry_ref[...] = jnp.zeros_like(carry_ref)

    ue = jnp.concatenate([carry_ref[...], u], axis=0)
    carry_ref[...] = u[t - HALO:, :]
    pos = si * t + lax.broadcasted_iota(jnp.int32, (t, 1), 0) + 1
    outs = []
    for gi, w in enumerate(POOL_WINDOWS):
        sl = slice(gi * POOL_GROUP, (gi + 1) * POOL_GROUP)
        acc = ue[:, sl]
        step = 1
        while step < w:
            acc = acc + pltpu.roll(acc, step, axis=0)
            step *= 2
        inv = 1.0 / jnp.minimum(pos, w).astype(F32)
        d = acc[HALO:, :] * inv - u[:, sl]
        outs.append(jnp.dot(d.astype(BF16), poolw_ref[gi], preferred_element_type=F32))
    y = jnp.concatenate(outs, axis=-1) * pscale_ref[...]
    ypool_out[0] = (y * (g_pool * _sigmoid(g_pool))).astype(BF16)


def _attn_kernel(qt_ref, k_ref, vt_ref, o_ref, qcur_ref, qnxt_ref, sa_ref, sb_ref, m_ref, l_ref, acc_ref):
    tq, tk = ATTN_TQ, ATTN_TK
    nq = HEADS_PER_STEP * tq
    n_qt = qt_ref.shape[2]
    kblocks = tk // CHUNK
    qry_chunk = lax.shift_right_logical(lax.broadcasted_iota(jnp.int32, (1, 1, nq), 2) & (tq - 1), 6)
    key_block = lax.broadcasted_iota(jnp.int32, (kblocks, 1, 1), 0)

    def build_qbd(dst_ref, i):
        zero = jnp.zeros((SLAB, tq), BF16)
        dst_ref[:SLAB, :tq] = qt_ref[0, 0, i, :SLAB, :]
        dst_ref[:SLAB, tq:] = zero
        dst_ref[SLAB:, :tq] = zero
        dst_ref[SLAB:, tq:] = qt_ref[0, 0, i, SLAB:, :]

    def scores(qbd_ref, j):
        kt = k_ref[0, pl.ds(pl.multiple_of(j * tk, tk), tk), :]
        return jnp.dot(kt, qbd_ref[...], preferred_element_type=F32)

    def process(s_ref, j, bias):
        s = s_ref[...]
        if bias is not None:
            s = (s.reshape(kblocks, CHUNK, nq) + bias).reshape(tk, nq)
        m_old = m_ref[...]
        m_new = jnp.maximum(m_old, jnp.max(s, axis=0, keepdims=True))
        p = jnp.exp2(s - m_new).astype(BF16)
        alpha = jnp.exp2(m_old - m_new)
        m_ref[...] = m_new
        pv = jnp.dot(vt_ref[0, 0, j], p, preferred_element_type=F32)
        l_ref[...] = alpha * l_ref[...] + pv[LANES:LANES + 1, :]
        for h in range(HEADS_PER_STEP):
            cols = slice(h * tq, (h + 1) * tq)
            acc_ref[h] = alpha[:, cols] * acc_ref[h] + pv[h * VDIM:(h + 1) * VDIM, cols]

    build_qbd(qnxt_ref, 0)
    sa_ref[...] = scores(qnxt_ref, 0)

    def query_tile(i, carry):
        last = ((i + 1) * tq + tk - 1) // tk - 1
        build_qbd(qcur_ref, i)
        build_qbd(qnxt_ref, jnp.minimum(i + 1, n_qt - 1))
        m_ref[...] = jnp.full_like(m_ref, -jnp.inf)
        l_ref[...] = jnp.zeros_like(l_ref)
        acc_ref[...] = jnp.zeros_like(acc_ref)
        allowed = key_block + (last * kblocks - i * (tq // CHUNK)) <= qry_chunk
        bias = jnp.where(allowed, 0.0, -jnp.inf).astype(F32)

        def pair(jj, c):
            j = 2 * jj
            sb_ref[...] = scores(qcur_ref, j + 1)
            process(sa_ref, j, None)
            sa_ref[...] = scores(qcur_ref, j + 2)
            process(sb_ref, j + 1, None)
            return c

        lax.fori_loop(0, last // 2, pair, 0)

        @pl.when(last % 2 == 1)
        def _():
            sb_ref[...] = scores(qcur_ref, last)
            process(sa_ref, last - 1, None)
            sa_ref[...] = scores(qnxt_ref, 0)
            process(sb_ref, last, bias)

        @pl.when(last % 2 == 0)
        def _():
            process(sa_ref, last, bias)
            sa_ref[...] = scores(qnxt_ref, 0)

        inv_l = 1.0 / l_ref[...]
        ot = jnp.concatenate([acc_ref[h] * inv_l[:, h * tq:(h + 1) * tq] for h in range(HEADS_PER_STEP)],
                             axis=0)
        o_ref[0, pl.ds(pl.multiple_of(i * tq, tq), tq), :] = ot.T.astype(BF16)
        return carry

    lax.fori_loop(0, n_qt, query_tile, 0)


def _merge_kernel(x_ref, o_ref, sga_ref, ypool_ref, gates_ref, wa_ref, wp_ref, wo_ref, nf_ref, out_ref):
    ya = (o_ref[0].astype(F32) * sga_ref[0].astype(F32)).astype(BF16)
    a = jnp.dot(ya, wa_ref[...], preferred_element_type=F32)
    p = jnp.dot(ypool_ref[0], wp_ref[...], preferred_element_type=F32)
    ga = gates_ref[0, :, :D_MODEL].astype(F32)
    gp = gates_ref[0, :, D_MODEL:].astype(F32)
    merged = (ga * a + gp * p).astype(BF16)
    h = x_ref[0] + jnp.dot(merged, wo_ref[...], preferred_element_type=F32)
    out_ref[0] = _rms(h, nf_ref[...])


def _rope_angles(seq):
    half = ROPE // 2
    inv_freq = ROPE_THETA ** (-jnp.arange(half, dtype=F32) / half)
    ang = jnp.arange(seq, dtype=F32)[:, None] * inv_freq[None, :]
    return jnp.cos(ang), jnp.sin(ang)


def _key_rope_tables(cos, sin):
    seq, half = cos.shape
    zeros_n = jnp.zeros((seq, NOPE), F32)
    zeros_h = jnp.zeros((seq, half), F32)
    zeros_p = jnp.zeros((seq, SLAB - NOPE - ROPE), F32)
    c = jnp.concatenate([zeros_n, cos, cos, zeros_p], axis=1)
    s1 = jnp.concatenate([zeros_n, -sin, zeros_h, zeros_p], axis=1)
    s2 = jnp.concatenate([zeros_n, zeros_h, sin, zeros_p], axis=1)
    return c, s1, s2


def _const_spec(shape):
    return pl.BlockSpec(shape, lambda *_: (0,) * len(shape), pipeline_mode=pl.Buffered(1))


def kernel(x, norm_in, w_in, q_norm, w_uq, kv_norm, w_ukv, pool_w, pool_scale,
           w_branch_attn, w_branch_pool, w_out, norm_final):
    b, s, d = x.shape
    tq, tk, t1, t3 = ATTN_TQ, ATTN_TK, PROJ_TILE, MERGE_TILE
    assert d == D_MODEL and s % t1 == 0 and t1 % tk == 0 and tk % tq == 0 and s % t3 == 0
    hp = HEADS // HEADS_PER_STEP

    o_kv = Q_RANK
    o_kr = o_kv + KV_RANK
    o_ga = o_kr + ROPE
    w_kr_slab = jnp.concatenate([jnp.zeros((d, NOPE), F32), w_in[:, o_kr:o_ga],
                                 jnp.zeros((d, SLAB - NOPE - ROPE), F32)], axis=1)
    w_a = jnp.concatenate([w_in[:, :o_kr], w_kr_slab], axis=1).astype(BF16)
    w_g = w_in[:, o_ga:].astype(BF16)
    w_uq_p = jnp.pad(w_uq, ((0, 0), (0, 0), (0, SLAB - NOPE - ROPE)))
    w_uqt = w_uq_p.reshape(Q_RANK, HEADS * SLAB).T.astype(BF16)
    w_uk_p = jnp.pad(w_ukv[:, :, :NOPE], ((0, 0), (0, 0), (0, SLAB - NOPE)))
    w_uk = w_uk_p.reshape(KV_RANK, HEADS * SLAB).astype(BF16)
    w_uvt = w_ukv[:, :, NOPE:].reshape(KV_RANK, MLA_WIDTH).T.astype(BF16)
    cos, sin = _rope_angles(s)
    qcos, qsin = (cos * QK_SCALE).T, (sin * QK_SCALE).T
    kc, ks1, ks2 = _key_rope_tables(cos, sin)

    row = lambda v: v.reshape(1, -1).astype(F32)
    tok = lambda width: pl.BlockSpec((1, t1, width), lambda bi, si: (bi, si, 0))
    tab = pl.BlockSpec((t1, SLAB), lambda bi, si: (si, 0))
    tabt = pl.BlockSpec((ROPE // 2, t1), lambda bi, si: (0, si))
    qt, k, vt, sga, ypool, gates = pl.pallas_call(
        _proj_kernel,
        grid=(b, s // t1),
        in_specs=[tok(d), _const_spec((1, d)), _const_spec(w_a.shape), _const_spec(w_g.shape),
                  _const_spec((1, Q_RANK)), _const_spec(w_uqt.shape), _const_spec((1, KV_RANK)),
                  _const_spec(w_uk.shape), _const_spec(w_uvt.shape),
                  _const_spec(pool_w.shape), _const_spec((1, POOL_WIDTH)),
                  tabt, tabt, tab, tab, tab],
        out_specs=[pl.BlockSpec((1, hp, t1 // tq, HEADS_PER_STEP * SLAB, tq),
                                lambda bi, si: (bi, 0, si, 0, 0)),
                   tok(HEADS * SLAB),
                   pl.BlockSpec((1, hp, t1 // tk, VT_ROWS, tk), lambda bi, si: (bi, 0, si, 0, 0)),
                   tok(MLA_WIDTH), tok(POOL_WIDTH), tok(2 * d)],
        out_shape=[jax.ShapeDtypeStruct((b, hp, s // tq, HEADS_PER_STEP * SLAB, tq), BF16),
                   jax.ShapeDtypeStruct((b, s, HEADS * SLAB), BF16),
                   jax.ShapeDtypeStruct((b, hp, s // tk, VT_ROWS, tk), BF16),
                   jax.ShapeDtypeStruct((b, s, MLA_WIDTH), BF16),
                   jax.ShapeDtypeStruct((b, s, POOL_WIDTH), BF16),
                   jax.ShapeDtypeStruct((b, s, 2 * d), BF16)],
        scratch_shapes=[pltpu.VMEM((HALO, POOL_WIDTH), F32)],
        compiler_params=pltpu.CompilerParams(
            dimension_semantics=("arbitrary", "arbitrary"), vmem_limit_bytes=VMEM_LIMIT),
        name="proj",
    )(x, row(norm_in), w_a, w_g, row(q_norm), w_uqt, row(kv_norm), w_uk, w_uvt,
      pool_w.astype(BF16), row(pool_scale), qcos, qsin, kc, ks1, ks2)

    o = pl.pallas_call(
        _attn_kernel,
        grid=(b, hp),
        in_specs=[pl.BlockSpec((1, 1, s // tq, HEADS_PER_STEP * SLAB, tq), lambda bi, pi: (bi, pi, 0, 0, 0)),
                  pl.BlockSpec((1, s, HEADS_PER_STEP * SLAB), lambda bi, pi: (bi, 0, pi)),
                  pl.BlockSpec((1, 1, s // tk, VT_ROWS, tk), lambda bi, pi: (bi, pi, 0, 0, 0))],
        out_specs=pl.BlockSpec((1, s, LANES), lambda bi, pi: (bi, 0, pi)),
        out_shape=jax.ShapeDtypeStruct((b, s, MLA_WIDTH), BF16),
        scratch_shapes=[pltpu.VMEM((HEADS_PER_STEP * SLAB, HEADS_PER_STEP * tq), BF16),
                        pltpu.VMEM((HEADS_PER_STEP * SLAB, HEADS_PER_STEP * tq), BF16),
                        pltpu.VMEM((tk, HEADS_PER_STEP * tq), F32),
                        pltpu.VMEM((tk, HEADS_PER_STEP * tq), F32),
                        pltpu.VMEM((1, HEADS_PER_STEP * tq), F32),
                        pltpu.VMEM((1, HEADS_PER_STEP * tq), F32),
                        pltpu.VMEM((HEADS_PER_STEP, VDIM, tq), F32)],
        compiler_params=pltpu.CompilerParams(
            dimension_semantics=("arbitrary", "arbitrary"), vmem_limit_bytes=VMEM_LIMIT),
        name="attn",
    )(qt, k, vt)

    tok3 = lambda width: pl.BlockSpec((1, t3, width), lambda bi, si: (bi, si, 0))
    out = pl.pallas_call(
        _merge_kernel,
        grid=(b, s // t3),
        in_specs=[tok3(d), tok3(MLA_WIDTH), tok3(MLA_WIDTH), tok3(POOL_WIDTH), tok3(2 * d),
                  _const_spec((MLA_WIDTH, d)), _const_spec((POOL_WIDTH, d)), _const_spec((d, d)),
                  _const_spec((1, d))],
        out_specs=tok3(d),
        out_shape=jax.ShapeDtypeStruct((b, s, d), x.dtype),
        compiler_params=pltpu.CompilerParams(
            dimension_semantics=("arbitrary", "arbitrary"), vmem_limit_bytes=VMEM_LIMIT),
        name="merge",
    )(x, o, sga, ypool, gates, w_branch_attn.astype(BF16), w_branch_pool.astype(BF16),
      w_out.astype(BF16), row(norm_final))
    return out
```

```python
import jax
import jax.numpy as jnp
from jax import lax
from jax.experimental import pallas as pl
from jax.experimental.pallas import tpu as pltpu

D_MODEL = 1024
CHUNK = 64
HEADS = 8
NOPE = 64
ROPE = 32
VDIM = 64
Q_RANK = 384
KV_RANK = 256
MLA_WIDTH = HEADS * VDIM
ROPE_THETA = 10000.0
POOL_WINDOWS = (2, 4, 8, 16)
POOL_WIDTH = D_MODEL // 2
POOL_GROUP = POOL_WIDTH // len(POOL_WINDOWS)
EPS = 1e-6
QK_SCALE = (NOPE + ROPE) ** -0.5 * 1.4426950408889634

LANES = 128
HALO = 16
SLAB = LANES
HEADS_PER_STEP = 2
ONES_ROWS = 16
VT_ROWS = HEADS_PER_STEP * VDIM + ONES_ROWS
PROJ_TILE = 512
ATTN_TQ = 512
ATTN_TK = 512
MERGE_TILE = 1024
VMEM_LIMIT = 56 * 1024 * 1024

BF16 = jnp.bfloat16
F32 = jnp.float32
NT_DIMS = (((1,), (1,)), ((), ()))


def _rms(x, g):
    return x * lax.rsqrt(jnp.mean(x * x, axis=-1, keepdims=True) + EPS) * g


def _sigmoid(x):
    return 1.0 / (1.0 + jnp.exp(-x))


def _proj_kernel(x_ref, nin_ref, wa_ref, wg_ref, qn_ref, wuqt_ref, kvn_ref, wuk_ref, wuvt_ref,
                 poolw_ref, pscale_ref, qcos_ref, qsin_ref, kc_ref, ks1_ref, ks2_ref,
                 qt_out, k_out, vt_out, sga_out, ypool_out, gates_out, carry_ref):
    t = PROJ_TILE
    si = pl.program_id(1)
    x = x_ref[0]
    hn = _rms(x, nin_ref[...]).astype(BF16)

    za = jnp.dot(hn, wa_ref[...], preferred_element_type=F32)
    zq = za[:, :Q_RANK]
    zkv = za[:, Q_RANK:Q_RANK + KV_RANK]
    zkr = za[:, Q_RANK + KV_RANK:]
    cq = _rms(zq, qn_ref[...]).astype(BF16)
    ckv = _rms(zkv, kvn_ref[...]).astype(BF16)

    qt = lax.dot_general(wuqt_ref[...], cq, NT_DIMS, preferred_element_type=F32)
    scale = QK_SCALE
    cos, sin = qcos_ref[...], qsin_ref[...]
    half = ROPE // 2
    tq = ATTN_TQ
    for h in range(HEADS):
        r0 = h * SLAB
        x1 = qt[r0 + NOPE:r0 + NOPE + half]
        x2 = qt[r0 + NOPE + half:r0 + NOPE + ROPE]
        slab = jnp.concatenate([qt[r0:r0 + NOPE] * scale, x1 * cos - x2 * sin, x1 * sin + x2 * cos,
                                jnp.zeros((SLAB - NOPE - ROPE, t), F32)], axis=0).astype(BF16)
        pr, hr = h // HEADS_PER_STEP, (h % HEADS_PER_STEP) * SLAB
        for c in range(t // tq):
            qt_out[0, pr, c, hr:hr + SLAB, :] = slab[:, c * tq:(c + 1) * tq]

    kf = jnp.dot(ckv, wuk_ref[...], preferred_element_type=F32)
    kr = (zkr * kc_ref[...] + pltpu.roll(zkr, LANES - half, axis=1) * ks1_ref[...]
          + pltpu.roll(zkr, half, axis=1) * ks2_ref[...])
    for h in range(HEADS):
        sl = slice(h * SLAB, (h + 1) * SLAB)
        k_out[0, :, sl] = (kf[:, sl] + kr).astype(BF16)

    vt = lax.dot_general(wuvt_ref[...], ckv, NT_DIMS, preferred_element_type=F32)
    for c in range(t // ATTN_TK):
        vt_out[0, :, c, :LANES, :] = vt[:, c * ATTN_TK:(c + 1) * ATTN_TK].reshape(
            HEADS // HEADS_PER_STEP, LANES, ATTN_TK).astype(BF16)
        vt_out[0, :, c, LANES:, :] = jnp.ones((HEADS // HEADS_PER_STEP, ONES_ROWS, ATTN_TK), BF16)

    zg = jnp.dot(hn, wg_ref[:, :3 * POOL_WIDTH], preferred_element_type=F32)
    g_attn = zg[:, :MLA_WIDTH]
    u = zg[:, MLA_WIDTH:MLA_WIDTH + POOL_WIDTH]
    g_pool = zg[:, MLA_WIDTH + POOL_WIDTH:]
    sga_out[0] = (g_attn * _sigmoid(g_attn)).astype(BF16)
    for c in range(4):
        lo = 3 * POOL_WIDTH + c * 512
        gm = jnp.dot(hn, wg_ref[:, lo:lo + 512], preferred_element_type=F32)
        gates_out[0, :, c * 512:(c + 1) * 512] = _sigmoid(gm).astype(BF16)

    @pl.when(si == 0)
    def _():
        carry_ref[...] = jnp.zeros_like(carry_ref)

    ue = jnp.concatenate([carry_ref[...], u], axis=0)
    carry_ref[...] = u[t - HALO:, :]
    pos = si * t + lax.broadcasted_iota(jnp.int32, (t, 1), 0) + 1
    outs = []
    for gi, w in enumerate(POOL_WINDOWS):
        sl = slice(gi * POOL_GROUP, (gi + 1) * POOL_GROUP)
        acc = ue[:, sl]
        step = 1
        while step < w:
            acc = acc + pltpu.roll(acc, step, axis=0)
            step *= 2
        inv = 1.0 / jnp.minimum(pos, w).astype(F32)
        d = acc[HALO:, :] * inv - u[:, sl]
        outs.append(jnp.dot(d.astype(BF16), poolw_ref[gi], preferred_element_type=F32))
    y = jnp.concatenate(outs, axis=-1) * pscale_ref[...]
    ypool_out[0] = (y * (g_pool * _sigmoid(g_pool))).astype(BF16)


def _attn_kernel(qt_ref, k_ref, vt_ref, sga_ref, o_ref, qcur_ref, qnxt_ref, sa_ref, sb_ref, mxa_ref, mxb_ref,
                 m_ref, l_ref, acc_ref):
    tq, tk = ATTN_TQ, ATTN_TK
    nq = HEADS_PER_STEP * tq
    n_qt = qt_ref.shape[2]
    kblocks = tk // CHUNK
    qry_chunk = lax.shift_right_logical(lax.broadcasted_iota(jnp.int32, (1, 1, nq), 2) & (tq - 1), 6)
    key_block = lax.broadcasted_iota(jnp.int32, (kblocks, 1, 1), 0)

    def build_qbd(dst_ref, i):
        zero = jnp.zeros((SLAB, tq), BF16)
        dst_ref[:SLAB, :tq] = qt_ref[0, 0, i, :SLAB, :]
        dst_ref[:SLAB, tq:] = zero
        dst_ref[SLAB:, :tq] = zero
        dst_ref[SLAB:, tq:] = qt_ref[0, 0, i, SLAB:, :]

    def score_tile(s_ref, mx_ref, qbd_ref, j, mask_for):
        kt = k_ref[0, pl.ds(pl.multiple_of(j * tk, tk), tk), :]
        s = jnp.dot(kt, qbd_ref[...], preferred_element_type=F32)
        if mask_for is not None:
            allowed = key_block + (j * kblocks - mask_for * (tq // CHUNK)) <= qry_chunk
            bias = jnp.where(allowed, 0.0, -jnp.inf).astype(F32)
            s = (s.reshape(kblocks, CHUNK, nq) + bias).reshape(tk, nq)
        s_ref[...] = s
        mx_ref[...] = jnp.max(s, axis=0, keepdims=True)

    def process(s_ref, mx_ref, j):
        m_old = m_ref[...]
        m_new = jnp.maximum(m_old, mx_ref[...])
        p = jnp.exp2(s_ref[...] - m_new).astype(BF16)
        alpha = jnp.exp2(m_old - m_new)
        m_ref[...] = m_new
        pv = jnp.dot(vt_ref[0, 0, j], p, preferred_element_type=F32)
        l_ref[...] = alpha * l_ref[...] + pv[LANES:LANES + 1, :]
        for h in range(HEADS_PER_STEP):
            cols = slice(h * tq, (h + 1) * tq)
            acc_ref[h] = alpha[:, cols] * acc_ref[h] + pv[h * VDIM:(h + 1) * VDIM, cols]

    build_qbd(qnxt_ref, 0)
    score_tile(sa_ref, mxa_ref, qnxt_ref, 0, 0)

    def query_tile(i, carry):
        last = ((i + 1) * tq + tk - 1) // tk - 1
        i_next = jnp.minimum(i + 1, n_qt - 1)
        build_qbd(qcur_ref, i)
        build_qbd(qnxt_ref, i_next)
        m_ref[...] = jnp.full_like(m_ref, -jnp.inf)
        l_ref[...] = jnp.zeros_like(l_ref)
        acc_ref[...] = jnp.zeros_like(acc_ref)

        def pair(jj, c):
            j = 2 * jj
            score_tile(sb_ref, mxb_ref, qcur_ref, j + 1, None)
            process(sa_ref, mxa_ref, j)
            score_tile(sa_ref, mxa_ref, qcur_ref, j + 2, i)
            process(sb_ref, mxb_ref, j + 1)
            return c

        lax.fori_loop(0, last // 2, pair, 0)

        @pl.when(last % 2 == 1)
        def _():
            score_tile(sb_ref, mxb_ref, qcur_ref, last, i)
            process(sa_ref, mxa_ref, last - 1)
            score_tile(sa_ref, mxa_ref, qnxt_ref, 0, i_next)
            process(sb_ref, mxb_ref, last)

        @pl.when(last % 2 == 0)
        def _():
            process(sa_ref, mxa_ref, last)
            score_tile(sa_ref, mxa_ref, qnxt_ref, 0, i_next)

        inv_l = 1.0 / l_ref[...]
        ot = jnp.concatenate([acc_ref[h] * inv_l[:, h * tq:(h + 1) * tq] for h in range(HEADS_PER_STEP)],
                             axis=0)
        rows = pl.ds(pl.multiple_of(i * tq, tq), tq)
        o_ref[0, rows, :] = (ot.T * sga_ref[0, rows, :].astype(F32)).astype(BF16)
        return carry

    lax.fori_loop(0, n_qt, query_tile, 0)


def _merge_kernel(x_ref, ya_ref, ypool_ref, gates_ref, wa_ref, wp_ref, wo_ref, nf_ref, out_ref):
    a = jnp.dot(ya_ref[0], wa_ref[...], preferred_element_type=F32)
    p = jnp.dot(ypool_ref[0], wp_ref[...], preferred_element_type=F32)
    ga = gates_ref[0, :, :D_MODEL].astype(F32)
    gp = gates_ref[0, :, D_MODEL:].astype(F32)
    merged = (ga * a + gp * p).astype(BF16)
    h = x_ref[0] + jnp.dot(merged, wo_ref[...], preferred_element_type=F32)
    out_ref[0] = _rms(h, nf_ref[...])


def _rope_angles(seq):
    half = ROPE // 2
    inv_freq = ROPE_THETA ** (-jnp.arange(half, dtype=F32) / half)
    ang = jnp.arange(seq, dtype=F32)[:, None] * inv_freq[None, :]
    return jnp.cos(ang), jnp.sin(ang)


def _key_rope_tables(cos, sin):
    seq, half = cos.shape
    zeros_n = jnp.zeros((seq, NOPE), F32)
    zeros_h = jnp.zeros((seq, half), F32)
    zeros_p = jnp.zeros((seq, SLAB - NOPE - ROPE), F32)
    c = jnp.concatenate([zeros_n, cos, cos, zeros_p], axis=1)
    s1 = jnp.concatenate([zeros_n, -sin, zeros_h, zeros_p], axis=1)
    s2 = jnp.concatenate([zeros_n, zeros_h, sin, zeros_p], axis=1)
    return c, s1, s2


def _const_spec(shape):
    return pl.BlockSpec(shape, lambda *_: (0,) * len(shape), pipeline_mode=pl.Buffered(1))


def kernel(x, norm_in, w_in, q_norm, w_uq, kv_norm, w_ukv, pool_w, pool_scale,
           w_branch_attn, w_branch_pool, w_out, norm_final):
    b, s, d = x.shape
    tq, tk, t1, t3 = ATTN_TQ, ATTN_TK, PROJ_TILE, MERGE_TILE
    assert d == D_MODEL and s % t1 == 0 and t1 % tk == 0 and tk % tq == 0 and s % t3 == 0
    hp = HEADS // HEADS_PER_STEP

    o_kv = Q_RANK
    o_kr = o_kv + KV_RANK
    o_ga = o_kr + ROPE
    w_kr_slab = jnp.concatenate([jnp.zeros((d, NOPE), F32), w_in[:, o_kr:o_ga],
                                 jnp.zeros((d, SLAB - NOPE - ROPE), F32)], axis=1)
    w_a = jnp.concatenate([w_in[:, :o_kr], w_kr_slab], axis=1).astype(BF16)
    w_g = w_in[:, o_ga:].astype(BF16)
    w_uq_p = jnp.pad(w_uq, ((0, 0), (0, 0), (0, SLAB - NOPE - ROPE)))
    w_uqt = w_uq_p.reshape(Q_RANK, HEADS * SLAB).T.astype(BF16)
    w_uk_p = jnp.pad(w_ukv[:, :, :NOPE], ((0, 0), (0, 0), (0, SLAB - NOPE)))
    w_uk = w_uk_p.reshape(KV_RANK, HEADS * SLAB).astype(BF16)
    w_uvt = w_ukv[:, :, NOPE:].reshape(KV_RANK, MLA_WIDTH).T.astype(BF16)
    cos, sin = _rope_angles(s)
    qcos, qsin = (cos * QK_SCALE).T, (sin * QK_SCALE).T
    kc, ks1, ks2 = _key_rope_tables(cos, sin)

    row = lambda v: v.reshape(1, -1).astype(F32)
    tok = lambda width: pl.BlockSpec((1, t1, width), lambda bi, si: (bi, si, 0))
    tab = pl.BlockSpec((t1, SLAB), lambda bi, si: (si, 0))
    tabt = pl.BlockSpec((ROPE // 2, t1), lambda bi, si: (0, si))
    qt, k, vt, sga, ypool, gates = pl.pallas_call(
        _proj_kernel,
        grid=(b, s // t1),
        in_specs=[tok(d), _const_spec((1, d)), _const_spec(w_a.shape), _const_spec(w_g.shape),
                  _const_spec((1, Q_RANK)), _const_spec(w_uqt.shape), _const_spec((1, KV_RANK)),
                  _const_spec(w_uk.shape), _const_spec(w_uvt.shape),
                  _const_spec(pool_w.shape), _const_spec((1, POOL_WIDTH)),
                  tabt, tabt, tab, tab, tab],
        out_specs=[pl.BlockSpec((1, hp, t1 // tq, HEADS_PER_STEP * SLAB, tq),
                                lambda bi, si: (bi, 0, si, 0, 0)),
                   tok(HEADS * SLAB),
                   pl.BlockSpec((1, hp, t1 // tk, VT_ROWS, tk), lambda bi, si: (bi, 0, si, 0, 0)),
                   tok(MLA_WIDTH), tok(POOL_WIDTH), tok(2 * d)],
        out_shape=[jax.ShapeDtypeStruct((b, hp, s // tq, HEADS_PER_STEP * SLAB, tq), BF16),
                   jax.ShapeDtypeStruct((b, s, HEADS * SLAB), BF16),
                   jax.ShapeDtypeStruct((b, hp, s // tk, VT_ROWS, tk), BF16),
                   jax.ShapeDtypeStruct((b, s, MLA_WIDTH), BF16),
                   jax.ShapeDtypeStruct((b, s, POOL_WIDTH), BF16),
                   jax.ShapeDtypeStruct((b, s, 2 * d), BF16)],
        scratch_shapes=[pltpu.VMEM((HALO, POOL_WIDTH), F32)],
        compiler_params=pltpu.CompilerParams(
            dimension_semantics=("arbitrary", "arbitrary"), vmem_limit_bytes=VMEM_LIMIT),
        name="proj",
    )(x, row(norm_in), w_a, w_g, row(q_norm), w_uqt, row(kv_norm), w_uk, w_uvt,
      pool_w.astype(BF16), row(pool_scale), qcos, qsin, kc, ks1, ks2)

    o = pl.pallas_call(
        _attn_kernel,
        grid=(b, hp),
        in_specs=[pl.BlockSpec((1, 1, s // tq, HEADS_PER_STEP * SLAB, tq), lambda bi, pi: (bi, pi, 0, 0, 0)),
                  pl.BlockSpec((1, s, HEADS_PER_STEP * SLAB), lambda bi, pi: (bi, 0, pi)),
                  pl.BlockSpec((1, 1, s // tk, VT_ROWS, tk), lambda bi, pi: (bi, pi, 0, 0, 0)),
                  pl.BlockSpec((1, s, LANES), lambda bi, pi: (bi, 0, pi))],
        out_specs=pl.BlockSpec((1, s, LANES), lambda bi, pi: (bi, 0, pi)),
        out_shape=jax.ShapeDtypeStruct((b, s, MLA_WIDTH), BF16),
        scratch_shapes=[pltpu.VMEM((HEADS_PER_STEP * SLAB, HEADS_PER_STEP * tq), BF16),
                        pltpu.VMEM((HEADS_PER_STEP * SLAB, HEADS_PER_STEP * tq), BF16),
                        pltpu.VMEM((tk, HEADS_PER_STEP * tq), F32),
                        pltpu.VMEM((tk, HEADS_PER_STEP * tq), F32),
                        pltpu.VMEM((1, HEADS_PER_STEP * tq), F32),
                        pltpu.VMEM((1, HEADS_PER_STEP * tq), F32),
                        pltpu.VMEM((1, HEADS_PER_STEP * tq), F32),
                        pltpu.VMEM((1, HEADS_PER_STEP * tq), F32),
                        pltpu.VMEM((HEADS_PER_STEP, VDIM, tq), F32)],
        compiler_params=pltpu.CompilerParams(
            dimension_semantics=("arbitrary", "arbitrary"), vmem_limit_bytes=VMEM_LIMIT),
        name="attn",
    )(qt, k, vt, sga)

    tok3 = lambda width: pl.BlockSpec((1, t3, width), lambda bi, si: (bi, si, 0))
    out = pl.pallas_call(
        _merge_kernel,
        grid=(b, s // t3),
        in_specs=[tok3(d), tok3(MLA_WIDTH), tok3(POOL_WIDTH), tok3(2 * d),
                  _const_spec((MLA_WIDTH, d)), _const_spec((POOL_WIDTH, d)), _const_spec((d, d)),
                  _const_spec((1, d))],
        out_specs=tok3(d),
        out_shape=jax.ShapeDtypeStruct((b, s, d), x.dtype),
        compiler_params=pltpu.CompilerParams(
            dimension_semantics=("arbitrary", "arbitrary"), vmem_limit_bytes=VMEM_LIMIT),
        name="merge",
    )(x, o, ypool, gates, w_branch_attn.astype(BF16), w_branch_pool.astype(BF16),
      w_out.astype(BF16), row(norm_final))
    return out
```

```python
import jax
import jax.numpy as jnp
from jax import lax
from jax.experimental import pallas as pl
from jax.experimental.pallas import tpu as pltpu

D_MODEL = 1024
CHUNK = 64
HEADS = 8
NOPE = 64
ROPE = 32
VDIM = 64
Q_RANK = 384
KV_RANK = 256
MLA_WIDTH = HEADS * VDIM
ROPE_THETA = 10000.0
POOL_WINDOWS = (2, 4, 8, 16)
POOL_WIDTH = D_MODEL // 2
POOL_GROUP = POOL_WIDTH // len(POOL_WINDOWS)
EPS = 1e-6
QK_SCALE = (NOPE + ROPE) ** -0.5 * 1.4426950408889634

LANES = 128
HALO = 16
SLAB = LANES
HEADS_PER_STEP = 2
ONES_ROWS = 16
VT_ROWS = HEADS_PER_STEP * VDIM + ONES_ROWS
PROJ_TILE = 512
ATTN_TQ = 512
ATTN_TK = 512
MERGE_TILE = 1024
VMEM_LIMIT = 56 * 1024 * 1024

BF16 = jnp.bfloat16
F32 = jnp.float32
NT_DIMS = (((1,), (1,)), ((), ()))


def _rms(x, g):
    return x * lax.rsqrt(jnp.mean(x * x, axis=-1, keepdims=True) + EPS) * g


def _sigmoid(x):
    return 1.0 / (1.0 + jnp.exp(-x))


def _proj_kernel(x_ref, nin_ref, win_ref, qn_ref, wuqt_ref, kvn_ref, wuk_ref, wuvt_ref,
                 poolw_ref, pscale_ref, qcos_ref, qsin_ref, kc_ref, ks1_ref, ks2_ref,
                 qt_out, k_out, vt_out, sga_out, ypool_out, gates_out, carry_ref):
    t = PROJ_TILE
    si = pl.program_id(1)

    @pl.when(si == 0)
    def _():
        carry_ref[...] = jnp.zeros_like(carry_ref)

    x = x_ref[0]
    hn = _rms(x, nin_ref[...]).astype(BF16)

    n_a = Q_RANK + KV_RANK + SLAB
    za = jnp.dot(hn, win_ref[:, :n_a], preferred_element_type=F32)
    zq = za[:, :Q_RANK]
    zkv = za[:, Q_RANK:Q_RANK + KV_RANK]
    zkr = za[:, Q_RANK + KV_RANK:]
    cq = _rms(zq, qn_ref[...]).astype(BF16)
    ckv = _rms(zkv, kvn_ref[...]).astype(BF16)

    qt = lax.dot_general(wuqt_ref[...], cq, NT_DIMS, preferred_element_type=F32)
    scale = QK_SCALE
    cos, sin = qcos_ref[...], qsin_ref[...]
    half = ROPE // 2
    tq = ATTN_TQ
    for h in range(HEADS):
        r0 = h * SLAB
        x1 = qt[r0 + NOPE:r0 + NOPE + half]
        x2 = qt[r0 + NOPE + half:r0 + NOPE + ROPE]
        slab = jnp.concatenate([qt[r0:r0 + NOPE] * scale, x1 * cos - x2 * sin, x1 * sin + x2 * cos,
                                jnp.zeros((SLAB - NOPE - ROPE, t), F32)], axis=0).astype(BF16)
        pr, hs = h // HEADS_PER_STEP, h % HEADS_PER_STEP
        for c in range(t // tq):
            for cs in range(HEADS_PER_STEP):
                blk = slab[:, c * tq:(c + 1) * tq] if cs == hs else jnp.zeros((SLAB, tq), BF16)
                qt_out[0, pr, c, hs * SLAB:(hs + 1) * SLAB, cs * tq:(cs + 1) * tq] = blk

    kf = jnp.dot(ckv, wuk_ref[...], preferred_element_type=F32)
    kr = (zkr * kc_ref[...] + pltpu.roll(zkr, LANES - half, axis=1) * ks1_ref[...]
          + pltpu.roll(zkr, half, axis=1) * ks2_ref[...])
    for h in range(HEADS):
        sl = slice(h * SLAB, (h + 1) * SLAB)
        k_out[0, :, sl] = (kf[:, sl] + kr).astype(BF16)

    vt = lax.dot_general(wuvt_ref[...], ckv, NT_DIMS, preferred_element_type=F32)
    for c in range(t // ATTN_TK):
        vt_out[0, :, c, :LANES, :] = vt[:, c * ATTN_TK:(c + 1) * ATTN_TK].reshape(
            HEADS // HEADS_PER_STEP, LANES, ATTN_TK).astype(BF16)
        vt_out[0, :, c, LANES:, :] = jnp.ones((HEADS // HEADS_PER_STEP, ONES_ROWS, ATTN_TK), BF16)

    zg = jnp.dot(hn, win_ref[:, n_a:n_a + 3 * POOL_WIDTH], preferred_element_type=F32)
    g_attn = zg[:, :MLA_WIDTH]
    u = zg[:, MLA_WIDTH:MLA_WIDTH + POOL_WIDTH]
    g_pool = zg[:, MLA_WIDTH + POOL_WIDTH:]
    sga_out[0] = (g_attn * _sigmoid(g_attn)).astype(BF16)
    for c in range(4):
        lo = n_a + 3 * POOL_WIDTH + c * 512
        gm = jnp.dot(hn, win_ref[:, lo:lo + 512], preferred_element_type=F32)
        gates_out[0, :, c * 512:(c + 1) * 512] = _sigmoid(gm).astype(BF16)

    ue = jnp.concatenate([carry_ref[...], u], axis=0)
    carry_ref[...] = u[t - HALO:, :]
    pos = si * t + lax.broadcasted_iota(jnp.int32, (t, 1), 0) + 1
    outs = []
    for gi, w in enumerate(POOL_WINDOWS):
        sl = slice(gi * POOL_GROUP, (gi + 1) * POOL_GROUP)
        acc = ue[:, sl]
        step = 1
        while step < w:
            acc = acc + pltpu.roll(acc, step, axis=0)
            step *= 2
        inv = 1.0 / jnp.minimum(pos, w).astype(F32)
        d = acc[HALO:, :] * inv - u[:, sl]
        outs.append(jnp.dot(d.astype(BF16), poolw_ref[gi], preferred_element_type=F32))
    y = jnp.concatenate(outs, axis=-1) * pscale_ref[...]
    ypool_out[0] = (y * (g_pool * _sigmoid(g_pool))).astype(BF16)


def _attn_kernel(qbd_ref, k_ref, vt_ref, sga_ref, o_ref, sa_ref, sb_ref, mxa_ref, mxb_ref,
                 m_ref, l_ref, acc_ref):
    tq, tk = ATTN_TQ, ATTN_TK
    nq = HEADS_PER_STEP * tq
    n_qt = qbd_ref.shape[2]
    kblocks = tk // CHUNK
    qry_chunk = lax.shift_right_logical(lax.broadcasted_iota(jnp.int32, (1, 1, nq), 2) & (tq - 1), 6)
    key_block = lax.broadcasted_iota(jnp.int32, (kblocks, 1, 1), 0)

    def score_tile(s_ref, mx_ref, qi, j, mask_for):
        kt = k_ref[0, pl.ds(pl.multiple_of(j * tk, tk), tk), :]
        s = jnp.dot(kt, qbd_ref[0, 0, qi], preferred_element_type=F32)
        if mask_for is not None:
            allowed = key_block + (j * kblocks - mask_for * (tq // CHUNK)) <= qry_chunk
            bias = jnp.where(allowed, 0.0, -jnp.inf).astype(F32)
            s = (s.reshape(kblocks, CHUNK, nq) + bias).reshape(tk, nq)
        s_ref[...] = s
        mx_ref[...] = jnp.max(s, axis=0, keepdims=True)

    def process(s_ref, mx_ref, j):
        m_old = m_ref[...]
        m_new = jnp.maximum(m_old, mx_ref[...])
        p = jnp.exp2(s_ref[...] - m_new).astype(BF16)
        alpha = jnp.exp2(m_old - m_new)
        m_ref[...] = m_new
        pv = jnp.dot(vt_ref[0, 0, j], p, preferred_element_type=F32)
        l_ref[...] = alpha * l_ref[...] + pv[LANES:LANES + 1, :]
        for h in range(HEADS_PER_STEP):
            cols = slice(h * tq, (h + 1) * tq)
            acc_ref[h] = alpha[:, cols] * acc_ref[h] + pv[h * VDIM:(h + 1) * VDIM, cols]

    score_tile(sa_ref, mxa_ref, 0, 0, 0)

    def query_tile(i, carry):
        last = ((i + 1) * tq + tk - 1) // tk - 1
        i_next = jnp.minimum(i + 1, n_qt - 1)
        m_ref[...] = jnp.full_like(m_ref, -jnp.inf)
        l_ref[...] = jnp.zeros_like(l_ref)
        acc_ref[...] = jnp.zeros_like(acc_ref)

        def pair(jj, c):
            j = 2 * jj
            score_tile(sb_ref, mxb_ref, i, j + 1, None)
            process(sa_ref, mxa_ref, j)
            score_tile(sa_ref, mxa_ref, i, j + 2, i)
            process(sb_ref, mxb_ref, j + 1)
            return c

        lax.fori_loop(0, last // 2, pair, 0)

        @pl.when(last % 2 == 1)
        def _():
            score_tile(sb_ref, mxb_ref, i, last, i)
            process(sa_ref, mxa_ref, last - 1)
            score_tile(sa_ref, mxa_ref, i_next, 0, i_next)
            process(sb_ref, mxb_ref, last)

        @pl.when(last % 2 == 0)
        def _():
            process(sa_ref, mxa_ref, last)
            score_tile(sa_ref, mxa_ref, i_next, 0, i_next)

        inv_l = 1.0 / l_ref[...]
        ot = jnp.concatenate([acc_ref[h] * inv_l[:, h * tq:(h + 1) * tq] for h in range(HEADS_PER_STEP)],
                             axis=0)
        rows = pl.ds(pl.multiple_of(i * tq, tq), tq)
        o_ref[0, rows, :] = (ot.T * sga_ref[0, rows, :].astype(F32)).astype(BF16)
        return carry

    lax.fori_loop(0, n_qt, query_tile, 0)


def _merge_kernel(x_ref, ya_ref, ypool_ref, gates_ref, wa_ref, wp_ref, wo_ref, nf_ref, out_ref):
    a = jnp.dot(ya_ref[0], wa_ref[...], preferred_element_type=F32)
    p = jnp.dot(ypool_ref[0], wp_ref[...], preferred_element_type=F32)
    ga = gates_ref[0, :, :D_MODEL].astype(F32)
    gp = gates_ref[0, :, D_MODEL:].astype(F32)
    merged = (ga * a + gp * p).astype(BF16)
    h = x_ref[0] + jnp.dot(merged, wo_ref[...], preferred_element_type=F32)
    out_ref[0] = _rms(h, nf_ref[...])


def _rope_angles(seq):
    half = ROPE // 2
    inv_freq = ROPE_THETA ** (-jnp.arange(half, dtype=F32) / half)
    ang = jnp.arange(seq, dtype=F32)[:, None] * inv_freq[None, :]
    return jnp.cos(ang), jnp.sin(ang)


def _key_rope_tables(cos, sin):
    seq, half = cos.shape
    zeros_n = jnp.zeros((seq, NOPE), F32)
    zeros_h = jnp.zeros((seq, half), F32)
    zeros_p = jnp.zeros((seq, SLAB - NOPE - ROPE), F32)
    c = jnp.concatenate([zeros_n, cos, cos, zeros_p], axis=1)
    s1 = jnp.concatenate([zeros_n, -sin, zeros_h, zeros_p], axis=1)
    s2 = jnp.concatenate([zeros_n, zeros_h, sin, zeros_p], axis=1)
    return c, s1, s2


def _const_spec(shape):
    return pl.BlockSpec(shape, lambda *_: (0,) * len(shape), pipeline_mode=pl.Buffered(1))


def kernel(x, norm_in, w_in, q_norm, w_uq, kv_norm, w_ukv, pool_w, pool_scale,
           w_branch_attn, w_branch_pool, w_out, norm_final):
    b, s, d = x.shape
    tq, tk, t1, t3 = ATTN_TQ, ATTN_TK, PROJ_TILE, MERGE_TILE
    assert d == D_MODEL and s % t1 == 0 and t1 % tk == 0 and t1 % tq == 0 and s % t3 == 0
    hp = HEADS // HEADS_PER_STEP
    nq = HEADS_PER_STEP * tq

    o_kv = Q_RANK
    o_kr = o_kv + KV_RANK
    o_ga = o_kr + ROPE
    w_in16 = w_in.astype(BF16)
    w_all = jnp.concatenate([w_in16[:, :o_kr], jnp.zeros((d, NOPE), BF16), w_in16[:, o_kr:o_ga],
                             jnp.zeros((d, SLAB - NOPE - ROPE), BF16), w_in16[:, o_ga:]],
                            axis=1)
    w_uq_p = jnp.pad(w_uq, ((0, 0), (0, 0), (0, SLAB - NOPE - ROPE)))
    w_uqt = w_uq_p.reshape(Q_RANK, HEADS * SLAB).T.astype(BF16)
    w_uk_p = jnp.pad(w_ukv[:, :, :NOPE], ((0, 0), (0, 0), (0, SLAB - NOPE)))
    w_uk = w_uk_p.reshape(KV_RANK, HEADS * SLAB).astype(BF16)
    w_uvt = w_ukv[:, :, NOPE:].reshape(KV_RANK, MLA_WIDTH).T.astype(BF16)
    cos, sin = _rope_angles(s)
    qcos, qsin = (cos * QK_SCALE).T, (sin * QK_SCALE).T
    kc, ks1, ks2 = _key_rope_tables(cos, sin)

    row = lambda v: v.reshape(1, -1).astype(F32)
    tok = lambda width: pl.BlockSpec((1, t1, width), lambda bi, si: (bi, si, 0))
    tab = pl.BlockSpec((t1, SLAB), lambda bi, si: (si, 0))
    tabt = pl.BlockSpec((ROPE // 2, t1), lambda bi, si: (0, si))
    qt, k, vt, sga, ypool, gates = pl.pallas_call(
        _proj_kernel,
        grid=(b, s // t1),
        in_specs=[tok(d), _const_spec((1, d)), _const_spec(w_all.shape),
                  _const_spec((1, Q_RANK)), _const_spec(w_uqt.shape), _const_spec((1, KV_RANK)),
                  _const_spec(w_uk.shape), _const_spec(w_uvt.shape),
                  _const_spec(pool_w.shape), _const_spec((1, POOL_WIDTH)),
                  tabt, tabt, tab, tab, tab],
        out_specs=[pl.BlockSpec((1, hp, t1 // tq, HEADS_PER_STEP * SLAB, nq),
                                lambda bi, si: (bi, 0, si, 0, 0)),
                   tok(HEADS * SLAB),
                   pl.BlockSpec((1, hp, t1 // tk, VT_ROWS, tk), lambda bi, si: (bi, 0, si, 0, 0)),
                   tok(MLA_WIDTH), tok(POOL_WIDTH), tok(2 * d)],
        out_shape=[jax.ShapeDtypeStruct((b, hp, s // tq, HEADS_PER_STEP * SLAB, nq), BF16),
                   jax.ShapeDtypeStruct((b, s, HEADS * SLAB), BF16),
                   jax.ShapeDtypeStruct((b, hp, s // tk, VT_ROWS, tk), BF16),
                   jax.ShapeDtypeStruct((b, s, MLA_WIDTH), BF16),
                   jax.ShapeDtypeStruct((b, s, POOL_WIDTH), BF16),
                   jax.ShapeDtypeStruct((b, s, 2 * d), BF16)],
        scratch_shapes=[pltpu.VMEM((HALO, POOL_WIDTH), F32)],
        compiler_params=pltpu.CompilerParams(
            dimension_semantics=("arbitrary", "arbitrary"), vmem_limit_bytes=VMEM_LIMIT),
        name="proj",
    )(x, row(norm_in), w_all, row(q_norm), w_uqt, row(kv_norm), w_uk, w_uvt,
      pool_w.astype(BF16), row(pool_scale), qcos, qsin, kc, ks1, ks2)

    o = pl.pallas_call(
        _attn_kernel,
        grid=(b, hp),
        in_specs=[pl.BlockSpec((1, 1, s // tq, HEADS_PER_STEP * SLAB, nq), lambda bi, pi: (bi, pi, 0, 0, 0)),
                  pl.BlockSpec((1, s, HEADS_PER_STEP * SLAB), lambda bi, pi: (bi, 0, pi)),
                  pl.BlockSpec((1, 1, s // tk, VT_ROWS, tk), lambda bi, pi: (bi, pi, 0, 0, 0)),
                  pl.BlockSpec((1, s, LANES), lambda bi, pi: (bi, 0, pi))],
        out_specs=pl.BlockSpec((1, s, LANES), lambda bi, pi: (bi, 0, pi)),
        out_shape=jax.ShapeDtypeStruct((b, s, MLA_WIDTH), BF16),
        scratch_shapes=[pltpu.VMEM((tk, nq), F32), pltpu.VMEM((tk, nq), F32),
                        pltpu.VMEM((1, nq), F32), pltpu.VMEM((1, nq), F32),
                        pltpu.VMEM((1, nq), F32), pltpu.VMEM((1, nq), F32),
                        pltpu.VMEM((HEADS_PER_STEP, VDIM, tq), F32)],
        compiler_params=pltpu.CompilerParams(
            dimension_semantics=("arbitrary", "arbitrary"), vmem_limit_bytes=VMEM_LIMIT),
        name="attn",
    )(qt, k, vt, sga)

    tok3 = lambda width: pl.BlockSpec((1, t3, width), lambda bi, si: (bi, si, 0))
    out = pl.pallas_call(
        _merge_kernel,
        grid=(b, s // t3),
        in_specs=[tok3(d), tok3(MLA_WIDTH), tok3(POOL_WIDTH), tok3(2 * d),
                  _const_spec((MLA_WIDTH, d)), _const_spec((POOL_WIDTH, d)), _const_spec((d, d)),
                  _const_spec((1, d))],
        out_specs=tok3(d),
        out_shape=jax.ShapeDtypeStruct((b, s, d), x.dtype),
        compiler_params=pltpu.CompilerParams(
            dimension_semantics=("arbitrary", "arbitrary"), vmem_limit_bytes=VMEM_LIMIT),
        name="merge",
    )(x, o, ypool, gates, w_branch_attn.astype(BF16), w_branch_pool.astype(BF16),
      w_out.astype(BF16), row(norm_final))
    return out
```

```python
import jax
import jax.numpy as jnp
from jax import lax
from jax.experimental import pallas as pl
from jax.experimental.pallas import tpu as pltpu

D_MODEL = 1024
CHUNK = 64
HEADS = 8
NOPE = 64
ROPE = 32
VDIM = 64
Q_RANK = 384
KV_RANK = 256
MLA_WIDTH = HEADS * VDIM
ROPE_THETA = 10000.0
POOL_WINDOWS = (2, 4, 8, 16)
POOL_WIDTH = D_MODEL // 2
POOL_GROUP = POOL_WIDTH // len(POOL_WINDOWS)
EPS = 1e-6
QK_SCALE = (NOPE + ROPE) ** -0.5 * 1.4426950408889634

LANES = 128
HALO = 16
SLAB = LANES
LATENT_COLS = Q_RANK + KV_RANK + SLAB
HEADS_PER_STEP = 2
ONES_ROWS = 16
VT_ROWS = HEADS_PER_STEP * VDIM + ONES_ROWS
PROJ_TILE = 512
ATTN_TQ = 512
ATTN_TK = 512
MERGE_TILE = 1024
VMEM_LIMIT = 56 * 1024 * 1024

BF16 = jnp.bfloat16
F32 = jnp.float32
NT_DIMS = (((1,), (1,)), ((), ()))


def _rms(x, g):
    return x * lax.rsqrt(jnp.mean(x * x, axis=-1, keepdims=True) + EPS) * g


def _sigmoid(x):
    return 1.0 / (1.0 + jnp.exp(-x))


def _proj_kernel(x_ref, nin_ref, wa_ref, wg_ref, qn_ref, wuqt_ref, kvn_ref, wuk_ref, wuvt_ref,
                 poolw_ref, pscale_ref, qcos_ref, qsin_ref, kc_ref, ks1_ref, ks2_ref,
                 qt_out, k_out, vt_out, sga_out, ypool_out, gates_out, carry_ref):
    t = PROJ_TILE
    si = pl.program_id(1)

    @pl.when(si == 0)
    def _():
        carry_ref[...] = jnp.zeros_like(carry_ref)

    x = x_ref[0]
    hn = _rms(x, nin_ref[...]).astype(BF16)

    za = jnp.dot(hn, wa_ref[...], preferred_element_type=F32)
    zq = za[:, :Q_RANK]
    zkv = za[:, Q_RANK:Q_RANK + KV_RANK]
    zkr = pltpu.roll(za[:, Q_RANK + KV_RANK:], NOPE, axis=1)
    cq = _rms(zq, qn_ref[...]).astype(BF16)
    ckv = _rms(zkv, kvn_ref[...]).astype(BF16)

    qt = lax.dot_general(wuqt_ref[...], cq, NT_DIMS, preferred_element_type=F32)
    scale = QK_SCALE
    cos, sin = qcos_ref[...], qsin_ref[...]
    half = ROPE // 2
    tq = ATTN_TQ
    for h in range(HEADS):
        r0 = h * SLAB
        x1 = qt[r0 + NOPE:r0 + NOPE + half]
        x2 = qt[r0 + NOPE + half:r0 + NOPE + ROPE]
        slab = jnp.concatenate([qt[r0:r0 + NOPE] * scale, x1 * cos - x2 * sin, x1 * sin + x2 * cos,
                                jnp.zeros((SLAB - NOPE - ROPE, t), F32)], axis=0).astype(BF16)
        pr, hs = h // HEADS_PER_STEP, h % HEADS_PER_STEP
        for c in range(t // tq):
            for cs in range(HEADS_PER_STEP):
                blk = slab[:, c * tq:(c + 1) * tq] if cs == hs else jnp.zeros((SLAB, tq), BF16)
                qt_out[0, pr, c, hs * SLAB:(hs + 1) * SLAB, cs * tq:(cs + 1) * tq] = blk

    kf = jnp.dot(ckv, wuk_ref[...], preferred_element_type=F32)
    kr = (zkr * kc_ref[...] + pltpu.roll(zkr, LANES - half, axis=1) * ks1_ref[...]
          + pltpu.roll(zkr, half, axis=1) * ks2_ref[...])
    for h in range(HEADS):
        sl = slice(h * SLAB, (h + 1) * SLAB)
        k_out[0, :, sl] = (kf[:, sl] + kr).astype(BF16)

    vt = lax.dot_general(wuvt_ref[...], ckv, NT_DIMS, preferred_element_type=F32)
    for c in range(t // ATTN_TK):
        vt_out[0, :, c, :LANES, :] = vt[:, c * ATTN_TK:(c + 1) * ATTN_TK].reshape(
            HEADS // HEADS_PER_STEP, LANES, ATTN_TK).astype(BF16)
        vt_out[0, :, c, LANES:, :] = jnp.ones((HEADS // HEADS_PER_STEP, ONES_ROWS, ATTN_TK), BF16)

    zg = jnp.dot(hn, wg_ref[:, :3 * POOL_WIDTH], preferred_element_type=F32)
    g_attn = zg[:, :MLA_WIDTH]
    u = zg[:, MLA_WIDTH:MLA_WIDTH + POOL_WIDTH]
    g_pool = zg[:, MLA_WIDTH + POOL_WIDTH:]
    sga_out[0] = (g_attn * _sigmoid(g_attn)).astype(BF16)
    for c in range(4):
        lo = 3 * POOL_WIDTH + c * 512
        gm = jnp.dot(hn, wg_ref[:, lo:lo + 512], preferred_element_type=F32)
        gates_out[0, :, c * 512:(c + 1) * 512] = _sigmoid(gm).astype(BF16)

    ue = jnp.concatenate([carry_ref[...], u], axis=0)
    carry_ref[...] = u[t - HALO:, :]
    pos = si * t + lax.broadcasted_iota(jnp.int32, (t, 1), 0) + 1
    outs = []
    for gi, w in enumerate(POOL_WINDOWS):
        sl = slice(gi * POOL_GROUP, (gi + 1) * POOL_GROUP)
        acc = ue[:, sl]
        step = 1
        while step < w:
            acc = acc + pltpu.roll(acc, step, axis=0)
            step *= 2
        inv = 1.0 / jnp.minimum(pos, w).astype(F32)
        d = acc[HALO:, :] * inv - u[:, sl]
        outs.append(jnp.dot(d.astype(BF16), poolw_ref[gi], preferred_element_type=F32))
    y = jnp.concatenate(outs, axis=-1) * pscale_ref[...]
    ypool_out[0] = (y * (g_pool * _sigmoid(g_pool))).astype(BF16)


def _attn_kernel(qbd_ref, k_ref, vt_ref, sga_ref, o_ref, sa_ref, sb_ref, mxa_ref, mxb_ref,
                 m_ref, l_ref, acc_ref):
    tq, tk = ATTN_TQ, ATTN_TK
    nq = HEADS_PER_STEP * tq
    n_qt = qbd_ref.shape[2]
    kblocks = tk // CHUNK
    qry_chunk = lax.shift_right_logical(lax.broadcasted_iota(jnp.int32, (1, 1, nq), 2) & (tq - 1), 6)
    key_block = lax.broadcasted_iota(jnp.int32, (kblocks, 1, 1), 0)

    def score_tile(s_ref, mx_ref, qi, j, mask_for):
        kt = k_ref[0, pl.ds(pl.multiple_of(j * tk, tk), tk), :]
        s = jnp.dot(kt, qbd_ref[0, 0, qi], preferred_element_type=F32)
        if mask_for is not None:
            allowed = key_block + (j * kblocks - mask_for * (tq // CHUNK)) <= qry_chunk
            bias = jnp.where(allowed, 0.0, -jnp.inf).astype(F32)
            s = (s.reshape(kblocks, CHUNK, nq) + bias).reshape(tk, nq)
        s_ref[...] = s
        mx_ref[...] = jnp.max(s, axis=0, keepdims=True)

    def process(s_ref, mx_ref, j):
        m_old = m_ref[...]
        m_new = jnp.maximum(m_old, mx_ref[...])
        p = jnp.exp2(s_ref[...] - m_new).astype(BF16)
        alpha = jnp.exp2(m_old - m_new)
        m_ref[...] = m_new
        pv = jnp.dot(vt_ref[0, 0, j], p, preferred_element_type=F32)
        l_ref[...] = alpha * l_ref[...] + pv[LANES:LANES + 1, :]
        for h in range(HEADS_PER_STEP):
            cols = slice(h * tq, (h + 1) * tq)
            acc_ref[h] = alpha[:, cols] * acc_ref[h] + pv[h * VDIM:(h + 1) * VDIM, cols]

    score_tile(sa_ref, mxa_ref, 0, 0, 0)

    def query_tile(i, carry):
        last = ((i + 1) * tq + tk - 1) // tk - 1
        i_next = jnp.minimum(i + 1, n_qt - 1)
        m_ref[...] = jnp.full_like(m_ref, -jnp.inf)
        l_ref[...] = jnp.zeros_like(l_ref)
        acc_ref[...] = jnp.zeros_like(acc_ref)

        def pair(jj, c):
            j = 2 * jj
            score_tile(sb_ref, mxb_ref, i, j + 1, None)
            process(sa_ref, mxa_ref, j)
            score_tile(sa_ref, mxa_ref, i, j + 2, i)
            process(sb_ref, mxb_ref, j + 1)
            return c

        lax.fori_loop(0, last // 2, pair, 0)

        @pl.when(last % 2 == 1)
        def _():
            score_tile(sb_ref, mxb_ref, i, last, i)
            process(sa_ref, mxa_ref, last - 1)
            score_tile(sa_ref, mxa_ref, i_next, 0, i_next)
            process(sb_ref, mxb_ref, last)

        @pl.when(last % 2 == 0)
        def _():
            process(sa_ref, mxa_ref, last)
            score_tile(sa_ref, mxa_ref, i_next, 0, i_next)

        inv_l = 1.0 / l_ref[...]
        ot = jnp.concatenate([acc_ref[h] * inv_l[:, h * tq:(h + 1) * tq] for h in range(HEADS_PER_STEP)],
                             axis=0)
        rows = pl.ds(pl.multiple_of(i * tq, tq), tq)
        o_ref[0, rows, :] = (ot.T * sga_ref[0, rows, :].astype(F32)).astype(BF16)
        return carry

    lax.fori_loop(0, n_qt, query_tile, 0)


def _merge_kernel(x_ref, ya_ref, ypool_ref, gates_ref, wa_ref, wp_ref, wo_ref, nf_ref, out_ref):
    a = jnp.dot(ya_ref[0], wa_ref[...], preferred_element_type=F32)
    p = jnp.dot(ypool_ref[0], wp_ref[...], preferred_element_type=F32)
    ga = gates_ref[0, :, :D_MODEL].astype(F32)
    gp = gates_ref[0, :, D_MODEL:].astype(F32)
    merged = (ga * a + gp * p).astype(BF16)
    h = x_ref[0] + jnp.dot(merged, wo_ref[...], preferred_element_type=F32)
    out_ref[0] = _rms(h, nf_ref[...])


def _rope_angles(seq):
    half = ROPE // 2
    inv_freq = ROPE_THETA ** (-jnp.arange(half, dtype=F32) / half)
    ang = jnp.arange(seq, dtype=F32)[:, None] * inv_freq[None, :]
    return jnp.cos(ang), jnp.sin(ang)


def _key_rope_tables(cos, sin):
    seq, half = cos.shape
    zeros_n = jnp.zeros((seq, NOPE), F32)
    zeros_h = jnp.zeros((seq, half), F32)
    zeros_p = jnp.zeros((seq, SLAB - NOPE - ROPE), F32)
    c = jnp.concatenate([zeros_n, cos, cos, zeros_p], axis=1)
    s1 = jnp.concatenate([zeros_n, -sin, zeros_h, zeros_p], axis=1)
    s2 = jnp.concatenate([zeros_n, zeros_h, sin, zeros_p], axis=1)
    return c, s1, s2


def _const_spec(shape):
    return pl.BlockSpec(shape, lambda *_: (0,) * len(shape), pipeline_mode=pl.Buffered(1))


def kernel(x, norm_in, w_in, q_norm, w_uq, kv_norm, w_ukv, pool_w, pool_scale,
           w_branch_attn, w_branch_pool, w_out, norm_final):
    b, s, d = x.shape
    tq, tk, t1, t3 = ATTN_TQ, ATTN_TK, PROJ_TILE, MERGE_TILE
    assert d == D_MODEL and s % t1 == 0 and t1 % tk == 0 and t1 % tq == 0 and s % t3 == 0
    hp = HEADS // HEADS_PER_STEP
    nq = HEADS_PER_STEP * tq

    o_kv = Q_RANK
    o_kr = o_kv + KV_RANK
    o_ga = o_kr + ROPE
    w_in16 = w_in.astype(BF16)
    w_g = w_in16[:, o_ga:]
    w_uq_p = jnp.pad(w_uq, ((0, 0), (0, 0), (0, SLAB - NOPE - ROPE)))
    w_uqt = w_uq_p.reshape(Q_RANK, HEADS * SLAB).T.astype(BF16)
    w_uk_p = jnp.pad(w_ukv[:, :, :NOPE], ((0, 0), (0, 0), (0, SLAB - NOPE)))
    w_uk = w_uk_p.reshape(KV_RANK, HEADS * SLAB).astype(BF16)
    w_uvt = w_ukv[:, :, NOPE:].reshape(KV_RANK, MLA_WIDTH).T.astype(BF16)
    cos, sin = _rope_angles(s)
    qcos, qsin = (cos * QK_SCALE).T, (sin * QK_SCALE).T
    kc, ks1, ks2 = _key_rope_tables(cos, sin)

    row = lambda v: v.reshape(1, -1).astype(F32)
    tok = lambda width: pl.BlockSpec((1, t1, width), lambda bi, si: (bi, si, 0))
    tab = pl.BlockSpec((t1, SLAB), lambda bi, si: (si, 0))
    tabt = pl.BlockSpec((ROPE // 2, t1), lambda bi, si: (0, si))
    qt, k, vt, sga, ypool, gates = pl.pallas_call(
        _proj_kernel,
        grid=(b, s // t1),
        in_specs=[tok(d), _const_spec((1, d)), _const_spec((d, LATENT_COLS)), _const_spec(w_g.shape),
                  _const_spec((1, Q_RANK)), _const_spec(w_uqt.shape), _const_spec((1, KV_RANK)),
                  _const_spec(w_uk.shape), _const_spec(w_uvt.shape),
                  _const_spec(pool_w.shape), _const_spec((1, POOL_WIDTH)),
                  tabt, tabt, tab, tab, tab],
        out_specs=[pl.BlockSpec((1, hp, t1 // tq, HEADS_PER_STEP * SLAB, nq),
                                lambda bi, si: (bi, 0, si, 0, 0)),
                   tok(HEADS * SLAB),
                   pl.BlockSpec((1, hp, t1 // tk, VT_ROWS, tk), lambda bi, si: (bi, 0, si, 0, 0)),
                   tok(MLA_WIDTH), tok(POOL_WIDTH), tok(2 * d)],
        out_shape=[jax.ShapeDtypeStruct((b, hp, s // tq, HEADS_PER_STEP * SLAB, nq), BF16),
                   jax.ShapeDtypeStruct((b, s, HEADS * SLAB), BF16),
                   jax.ShapeDtypeStruct((b, hp, s // tk, VT_ROWS, tk), BF16),
                   jax.ShapeDtypeStruct((b, s, MLA_WIDTH), BF16),
                   jax.ShapeDtypeStruct((b, s, POOL_WIDTH), BF16),
                   jax.ShapeDtypeStruct((b, s, 2 * d), BF16)],
        scratch_shapes=[pltpu.VMEM((HALO, POOL_WIDTH), F32)],
        compiler_params=pltpu.CompilerParams(
            dimension_semantics=("arbitrary", "arbitrary"), vmem_limit_bytes=VMEM_LIMIT),
        name="proj",
    )(x, row(norm_in), w_in16, w_g, row(q_norm), w_uqt, row(kv_norm), w_uk, w_uvt,
      pool_w.astype(BF16), row(pool_scale), qcos, qsin, kc, ks1, ks2)

    o = pl.pallas_call(
        _attn_kernel,
        grid=(b, hp),
        in_specs=[pl.BlockSpec((1, 1, s // tq, HEADS_PER_STEP * SLAB, nq), lambda bi, pi: (bi, pi, 0, 0, 0)),
                  pl.BlockSpec((1, s, HEADS_PER_STEP * SLAB), lambda bi, pi: (bi, 0, pi)),
                  pl.BlockSpec((1, 1, s // tk, VT_ROWS, tk), lambda bi, pi: (bi, pi, 0, 0, 0)),
                  pl.BlockSpec((1, s, LANES), lambda bi, pi: (bi, 0, pi))],
        out_specs=pl.BlockSpec((1, s, LANES), lambda bi, pi: (bi, 0, pi)),
        out_shape=jax.ShapeDtypeStruct((b, s, MLA_WIDTH), BF16),
        scratch_shapes=[pltpu.VMEM((tk, nq), F32), pltpu.VMEM((tk, nq), F32),
                        pltpu.VMEM((1, nq), F32), pltpu.VMEM((1, nq), F32),
                        pltpu.VMEM((1, nq), F32), pltpu.VMEM((1, nq), F32),
                        pltpu.VMEM((HEADS_PER_STEP, VDIM, tq), F32)],
        compiler_params=pltpu.CompilerParams(
            dimension_semantics=("arbitrary", "arbitrary"), vmem_limit_bytes=VMEM_LIMIT),
        name="attn",
    )(qt, k, vt, sga)

    tok3 = lambda width: pl.BlockSpec((1, t3, width), lambda bi, si: (bi, si, 0))
    out = pl.pallas_call(
        _merge_kernel,
        grid=(b, s // t3),
        in_specs=[tok3(d), tok3(MLA_WIDTH), tok3(POOL_WIDTH), tok3(2 * d),
                  _const_spec((MLA_WIDTH, d)), _const_spec((POOL_WIDTH, d)), _const_spec((d, d)),
                  _const_spec((1, d))],
        out_specs=tok3(d),
        out_shape=jax.ShapeDtypeStruct((b, s, d), x.dtype),
        compiler_params=pltpu.CompilerParams(
            dimension_semantics=("arbitrary", "arbitrary"), vmem_limit_bytes=VMEM_LIMIT),
        name="merge",
    )(x, o, ypool, gates, w_branch_attn.astype(BF16), w_branch_pool.astype(BF16),
      w_out.astype(BF16), row(norm_final))
    return out
```

```python
import jax
import jax.numpy as jnp
from jax import lax
from jax.experimental import pallas as pl
from jax.experimental.pallas import tpu as pltpu

D_MODEL = 1024
CHUNK = 64
HEADS = 8
NOPE = 64
ROPE = 32
VDIM = 64
Q_RANK = 384
KV_RANK = 256
MLA_WIDTH = HEADS * VDIM
ROPE_THETA = 10000.0
POOL_WINDOWS = (2, 4, 8, 16)
POOL_WIDTH = D_MODEL // 2
POOL_GROUP = POOL_WIDTH // len(POOL_WINDOWS)
EPS = 1e-6
QK_SCALE = (NOPE + ROPE) ** -0.5 * 1.4426950408889634

LANES = 128
HALO = 16
SLAB = LANES
LATENT_COLS = Q_RANK + KV_RANK + SLAB
HEADS_PER_STEP = 2
PAIRS_PER_STEP = 2
ONES_ROWS = 16
VT_ROWS = HEADS_PER_STEP * VDIM + ONES_ROWS
PROJ_TILE = 512
ATTN_TQ = 512
ATTN_TK = 512
MERGE_TILE = 1024
VMEM_LIMIT = 56 * 1024 * 1024

BF16 = jnp.bfloat16
F32 = jnp.float32
NT_DIMS = (((1,), (1,)), ((), ()))


def _rms(x, g):
    return x * lax.rsqrt(jnp.mean(x * x, axis=-1, keepdims=True) + EPS) * g


def _sigmoid(x):
    return 1.0 / (1.0 + jnp.exp(-x))


def _proj_kernel(x_ref, nin_ref, wa_ref, wg_ref, qn_ref, wuqt_ref, kvn_ref, wuk_ref, wuvt_ref,
                 poolw_ref, pscale_ref, qcos_ref, qsin_ref, kc_ref, ks1_ref, ks2_ref,
                 qt_out, k_out, vt_out, sga_out, ypool_out, gates_out, carry_ref):
    t = PROJ_TILE
    si = pl.program_id(1)

    @pl.when(si == 0)
    def _():
        carry_ref[...] = jnp.zeros_like(carry_ref)

    x = x_ref[0]
    hn = _rms(x, nin_ref[...]).astype(BF16)

    za = jnp.dot(hn, wa_ref[...], preferred_element_type=F32)
    zq = za[:, :Q_RANK]
    zkv = za[:, Q_RANK:Q_RANK + KV_RANK]
    zkr = pltpu.roll(za[:, Q_RANK + KV_RANK:], NOPE, axis=1)
    cq = _rms(zq, qn_ref[...]).astype(BF16)
    ckv = _rms(zkv, kvn_ref[...]).astype(BF16)

    qt = lax.dot_general(wuqt_ref[...], cq, NT_DIMS, preferred_element_type=F32)
    scale = QK_SCALE
    cos, sin = qcos_ref[...], qsin_ref[...]
    half = ROPE // 2
    tq = ATTN_TQ
    for h in range(HEADS):
        r0 = h * SLAB
        x1 = qt[r0 + NOPE:r0 + NOPE + half]
        x2 = qt[r0 + NOPE + half:r0 + NOPE + ROPE]
        slab = jnp.concatenate([qt[r0:r0 + NOPE] * scale, x1 * cos - x2 * sin, x1 * sin + x2 * cos,
                                jnp.zeros((SLAB - NOPE - ROPE, t), F32)], axis=0).astype(BF16)
        pr, hs = h // HEADS_PER_STEP, h % HEADS_PER_STEP
        for c in range(t // tq):
            for cs in range(HEADS_PER_STEP):
                blk = slab[:, c * tq:(c + 1) * tq] if cs == hs else jnp.zeros((SLAB, tq), BF16)
                qt_out[0, pr, c, hs * SLAB:(hs + 1) * SLAB, cs * tq:(cs + 1) * tq] = blk

    kf = jnp.dot(ckv, wuk_ref[...], preferred_element_type=F32)
    kr = (zkr * kc_ref[...] + pltpu.roll(zkr, LANES - half, axis=1) * ks1_ref[...]
          + pltpu.roll(zkr, half, axis=1) * ks2_ref[...])
    for h in range(HEADS):
        sl = slice(h * SLAB, (h + 1) * SLAB)
        k_out[0, :, sl] = (kf[:, sl] + kr).astype(BF16)

    vt = lax.dot_general(wuvt_ref[...], ckv, NT_DIMS, preferred_element_type=F32)
    for c in range(t // ATTN_TK):
        vt_out[0, :, c, :LANES, :] = vt[:, c * ATTN_TK:(c + 1) * ATTN_TK].reshape(
            HEADS // HEADS_PER_STEP, LANES, ATTN_TK).astype(BF16)
        vt_out[0, :, c, LANES:, :] = jnp.ones((HEADS // HEADS_PER_STEP, ONES_ROWS, ATTN_TK), BF16)

    zg = jnp.dot(hn, wg_ref[:, :3 * POOL_WIDTH], preferred_element_type=F32)
    g_attn = zg[:, :MLA_WIDTH]
    u = zg[:, MLA_WIDTH:MLA_WIDTH + POOL_WIDTH]
    g_pool = zg[:, MLA_WIDTH + POOL_WIDTH:]
    sga_out[0] = (g_attn * _sigmoid(g_attn)).astype(BF16)
    for c in range(4):
        lo = 3 * POOL_WIDTH + c * 512
        gm = jnp.dot(hn, wg_ref[:, lo:lo + 512], preferred_element_type=F32)
        gates_out[0, :, c * 512:(c + 1) * 512] = _sigmoid(gm).astype(BF16)

    ue = jnp.concatenate([carry_ref[...], u], axis=0)
    carry_ref[...] = u[t - HALO:, :]
    pos = si * t + lax.broadcasted_iota(jnp.int32, (t, 1), 0) + 1
    outs = []
    for gi, w in enumerate(POOL_WINDOWS):
        sl = slice(gi * POOL_GROUP, (gi + 1) * POOL_GROUP)
        acc = ue[:, sl]
        step = 1
        while step < w:
            acc = acc + pltpu.roll(acc, step, axis=0)
            step *= 2
        inv = 1.0 / jnp.minimum(pos, w).astype(F32)
        d = acc[HALO:, :] * inv - u[:, sl]
        outs.append(jnp.dot(d.astype(BF16), poolw_ref[gi], preferred_element_type=F32))
    y = jnp.concatenate(outs, axis=-1) * pscale_ref[...]
    ypool_out[0] = (y * (g_pool * _sigmoid(g_pool))).astype(BF16)


def _attn_kernel(qbd_ref, k_ref, vt_ref, sga_ref, o_ref, sa_ref, sb_ref, mxa_ref, mxb_ref,
                 m_ref, l_ref, acc_ref):
    tq, tk = ATTN_TQ, ATTN_TK
    nq = HEADS_PER_STEP * tq
    n_qt = qbd_ref.shape[2]
    kblocks = tk // CHUNK
    qry_chunk = lax.shift_right_logical(lax.broadcasted_iota(jnp.int32, (1, 1, nq), 2) & (tq - 1), 6)
    key_block = lax.broadcasted_iota(jnp.int32, (kblocks, 1, 1), 0)

    def score_tile(s_ref, mx_ref, qi, j, mask_for):
        if mask_for is not None:
            allowed = key_block + (j * kblocks - mask_for * (tq // CHUNK)) <= qry_chunk
            bias = jnp.where(allowed, 0.0, -jnp.inf).astype(F32)
        for g in range(PAIRS_PER_STEP):
            kt = k_ref[0, pl.ds(pl.multiple_of(j * tk, tk), tk),
                       g * HEADS_PER_STEP * SLAB:(g + 1) * HEADS_PER_STEP * SLAB]
            s = jnp.dot(kt, qbd_ref[0, g, qi], preferred_element_type=F32)
            if mask_for is not None:
                s = (s.reshape(kblocks, CHUNK, nq) + bias).reshape(tk, nq)
            s_ref[g] = s
            mx_ref[g] = jnp.max(s, axis=0, keepdims=True)

    def process(s_ref, mx_ref, j):
        for g in range(PAIRS_PER_STEP):
            m_old = m_ref[g]
            m_new = jnp.maximum(m_old, mx_ref[g])
            p = jnp.exp2(s_ref[g] - m_new).astype(BF16)
            alpha = jnp.exp2(m_old - m_new)
            m_ref[g] = m_new
            pv = jnp.dot(vt_ref[0, g, j], p, preferred_element_type=F32)
            l_ref[g] = alpha * l_ref[g] + pv[LANES:LANES + 1, :]
            for h in range(HEADS_PER_STEP):
                cols = slice(h * tq, (h + 1) * tq)
                acc_ref[g, h] = alpha[:, cols] * acc_ref[g, h] + pv[h * VDIM:(h + 1) * VDIM, cols]

    score_tile(sa_ref, mxa_ref, 0, 0, 0)

    def query_tile(i, carry):
        last = ((i + 1) * tq + tk - 1) // tk - 1
        i_next = jnp.minimum(i + 1, n_qt - 1)
        m_ref[...] = jnp.full_like(m_ref, -jnp.inf)
        l_ref[...] = jnp.zeros_like(l_ref)
        acc_ref[...] = jnp.zeros_like(acc_ref)

        def pair(jj, c):
            j = 2 * jj
            score_tile(sb_ref, mxb_ref, i, j + 1, None)
            process(sa_ref, mxa_ref, j)
            score_tile(sa_ref, mxa_ref, i, j + 2, i)
            process(sb_ref, mxb_ref, j + 1)
            return c

        lax.fori_loop(0, last // 2, pair, 0)

        @pl.when(last % 2 == 1)
        def _():
            score_tile(sb_ref, mxb_ref, i, last, i)
            process(sa_ref, mxa_ref, last - 1)
            score_tile(sa_ref, mxa_ref, i_next, 0, i_next)
            process(sb_ref, mxb_ref, last)

        @pl.when(last % 2 == 0)
        def _():
            process(sa_ref, mxa_ref, last)
            score_tile(sa_ref, mxa_ref, i_next, 0, i_next)

        rows = pl.ds(pl.multiple_of(i * tq, tq), tq)
        for g in range(PAIRS_PER_STEP):
            inv_l = 1.0 / l_ref[g]
            ot = jnp.concatenate([acc_ref[g, h] * inv_l[:, h * tq:(h + 1) * tq]
                                  for h in range(HEADS_PER_STEP)], axis=0)
            cols = slice(g * LANES, (g + 1) * LANES)
            o_ref[0, rows, cols] = (ot.T * sga_ref[0, rows, cols].astype(F32)).astype(BF16)
        return carry

    lax.fori_loop(0, n_qt, query_tile, 0)


def _merge_kernel(x_ref, ya_ref, ypool_ref, gates_ref, wa_ref, wp_ref, wo_ref, nf_ref, out_ref):
    a = jnp.dot(ya_ref[0], wa_ref[...], preferred_element_type=F32)
    p = jnp.dot(ypool_ref[0], wp_ref[...], preferred_element_type=F32)
    ga = gates_ref[0, :, :D_MODEL].astype(F32)
    gp = gates_ref[0, :, D_MODEL:].astype(F32)
    merged = (ga * a + gp * p).astype(BF16)
    h = x_ref[0] + jnp.dot(merged, wo_ref[...], preferred_element_type=F32)
    out_ref[0] = _rms(h, nf_ref[...])


def _rope_angles(seq):
    half = ROPE // 2
    inv_freq = ROPE_THETA ** (-jnp.arange(half, dtype=F32) / half)
    ang = jnp.arange(seq, dtype=F32)[:, None] * inv_freq[None, :]
    return jnp.cos(ang), jnp.sin(ang)


def _key_rope_tables(cos, sin):
    seq, half = cos.shape
    zeros_n = jnp.zeros((seq, NOPE), F32)
    zeros_h = jnp.zeros((seq, half), F32)
    zeros_p = jnp.zeros((seq, SLAB - NOPE - ROPE), F32)
    c = jnp.concatenate([zeros_n, cos, cos, zeros_p], axis=1)
    s1 = jnp.concatenate([zeros_n, -sin, zeros_h, zeros_p], axis=1)
    s2 = jnp.concatenate([zeros_n, zeros_h, sin, zeros_p], axis=1)
    return c, s1, s2


def _const_spec(shape):
    return pl.BlockSpec(shape, lambda *_: (0,) * len(shape), pipeline_mode=pl.Buffered(1))


def kernel(x, norm_in, w_in, q_norm, w_uq, kv_norm, w_ukv, pool_w, pool_scale,
           w_branch_attn, w_branch_pool, w_out, norm_final):
    b, s, d = x.shape
    tq, tk, t1, t3 = ATTN_TQ, ATTN_TK, PROJ_TILE, MERGE_TILE
    assert d == D_MODEL and s % t1 == 0 and t1 % tk == 0 and t1 % tq == 0 and s % t3 == 0
    hp = HEADS // HEADS_PER_STEP
    nq = HEADS_PER_STEP * tq
    g2 = PAIRS_PER_STEP

    o_kv = Q_RANK
    o_kr = o_kv + KV_RANK
    o_ga = o_kr + ROPE
    w_in16 = w_in.astype(BF16)
    w_g = w_in16[:, o_ga:]
    w_uq_p = jnp.pad(w_uq, ((0, 0), (0, 0), (0, SLAB - NOPE - ROPE)))
    w_uqt = w_uq_p.reshape(Q_RANK, HEADS * SLAB).T.astype(BF16)
    w_uk_p = jnp.pad(w_ukv[:, :, :NOPE], ((0, 0), (0, 0), (0, SLAB - NOPE)))
    w_uk = w_uk_p.reshape(KV_RANK, HEADS * SLAB).astype(BF16)
    w_uvt = w_ukv[:, :, NOPE:].reshape(KV_RANK, MLA_WIDTH).T.astype(BF16)
    cos, sin = _rope_angles(s)
    qcos, qsin = (cos * QK_SCALE).T, (sin * QK_SCALE).T
    kc, ks1, ks2 = _key_rope_tables(cos, sin)

    row = lambda v: v.reshape(1, -1).astype(F32)
    tok = lambda width: pl.BlockSpec((1, t1, width), lambda bi, si: (bi, si, 0))
    tab = pl.BlockSpec((t1, SLAB), lambda bi, si: (si, 0))
    tabt = pl.BlockSpec((ROPE // 2, t1), lambda bi, si: (0, si))
    qt, k, vt, sga, ypool, gates = pl.pallas_call(
        _proj_kernel,
        grid=(b, s // t1),
        in_specs=[tok(d), _const_spec((1, d)), _const_spec((d, LATENT_COLS)), _const_spec(w_g.shape),
                  _const_spec((1, Q_RANK)), _const_spec(w_uqt.shape), _const_spec((1, KV_RANK)),
                  _const_spec(w_uk.shape), _const_spec(w_uvt.shape),
                  _const_spec(pool_w.shape), _const_spec((1, POOL_WIDTH)),
                  tabt, tabt, tab, tab, tab],
        out_specs=[pl.BlockSpec((1, hp, t1 // tq, HEADS_PER_STEP * SLAB, nq),
                                lambda bi, si: (bi, 0, si, 0, 0)),
                   tok(HEADS * SLAB),
                   pl.BlockSpec((1, hp, t1 // tk, VT_ROWS, tk), lambda bi, si: (bi, 0, si, 0, 0)),
                   tok(MLA_WIDTH), tok(POOL_WIDTH), tok(2 * d)],
        out_shape=[jax.ShapeDtypeStruct((b, hp, s // tq, HEADS_PER_STEP * SLAB, nq), BF16),
                   jax.ShapeDtypeStruct((b, s, HEADS * SLAB), BF16),
                   jax.ShapeDtypeStruct((b, hp, s // tk, VT_ROWS, tk), BF16),
                   jax.ShapeDtypeStruct((b, s, MLA_WIDTH), BF16),
                   jax.ShapeDtypeStruct((b, s, POOL_WIDTH), BF16),
                   jax.ShapeDtypeStruct((b, s, 2 * d), BF16)],
        scratch_shapes=[pltpu.VMEM((HALO, POOL_WIDTH), F32)],
        compiler_params=pltpu.CompilerParams(
            dimension_semantics=("arbitrary", "arbitrary"), vmem_limit_bytes=VMEM_LIMIT),
        name="proj",
    )(x, row(norm_in), w_in16, w_g, row(q_norm), w_uqt, row(kv_norm), w_uk, w_uvt,
      pool_w.astype(BF16), row(pool_scale), qcos, qsin, kc, ks1, ks2)

    o = pl.pallas_call(
        _attn_kernel,
        grid=(b, hp // g2),
        in_specs=[pl.BlockSpec((1, g2, s // tq, HEADS_PER_STEP * SLAB, nq), lambda bi, pi: (bi, pi, 0, 0, 0)),
                  pl.BlockSpec((1, s, g2 * HEADS_PER_STEP * SLAB), lambda bi, pi: (bi, 0, pi)),
                  pl.BlockSpec((1, g2, s // tk, VT_ROWS, tk), lambda bi, pi: (bi, pi, 0, 0, 0)),
                  pl.BlockSpec((1, s, g2 * LANES), lambda bi, pi: (bi, 0, pi))],
        out_specs=pl.BlockSpec((1, s, g2 * LANES), lambda bi, pi: (bi, 0, pi)),
        out_shape=jax.ShapeDtypeStruct((b, s, MLA_WIDTH), BF16),
        scratch_shapes=[pltpu.VMEM((g2, tk, nq), F32), pltpu.VMEM((g2, tk, nq), F32),
                        pltpu.VMEM((g2, 1, nq), F32), pltpu.VMEM((g2, 1, nq), F32),
                        pltpu.VMEM((g2, 1, nq), F32), pltpu.VMEM((g2, 1, nq), F32),
                        pltpu.VMEM((g2, HEADS_PER_STEP, VDIM, tq), F32)],
        compiler_params=pltpu.CompilerParams(
            dimension_semantics=("arbitrary", "arbitrary"), vmem_limit_bytes=VMEM_LIMIT),
        name="attn",
    )(qt, k, vt, sga)

    tok3 = lambda width: pl.BlockSpec((1, t3, width), lambda bi, si: (bi, si, 0))
    out = pl.pallas_call(
        _merge_kernel,
        grid=(b, s // t3),
        in_specs=[tok3(d), tok3(MLA_WIDTH), tok3(POOL_WIDTH), tok3(2 * d),
                  _const_spec((MLA_WIDTH, d)), _const_spec((POOL_WIDTH, d)), _const_spec((d, d)),
                  _const_spec((1, d))],
        out_specs=tok3(d),
        out_shape=jax.ShapeDtypeStruct((b, s, d), x.dtype),
        compiler_params=pltpu.CompilerParams(
            dimension_semantics=("arbitrary", "arbitrary"), vmem_limit_bytes=VMEM_LIMIT),
        name="merge",
    )(x, o, ypool, gates, w_branch_attn.astype(BF16), w_branch_pool.astype(BF16),
      w_out.astype(BF16), row(norm_final))
    return out
```

```python
import jax
import jax.numpy as jnp
from jax import lax
from jax.experimental import pallas as pl
from jax.experimental.pallas import tpu as pltpu

D_MODEL = 1024
CHUNK = 64
HEADS = 8
NOPE = 64
ROPE = 32
VDIM = 64
Q_RANK = 384
KV_RANK = 256
MLA_WIDTH = HEADS * VDIM
ROPE_THETA = 10000.0
POOL_WINDOWS = (2, 4, 8, 16)
POOL_WIDTH = D_MODEL // 2
POOL_GROUP = POOL_WIDTH // len(POOL_WINDOWS)
EPS = 1e-6
QK_SCALE = (NOPE + ROPE) ** -0.5 * 1.4426950408889634

LANES = 128
HALO = 16
SLAB = LANES
LATENT_COLS = Q_RANK + KV_RANK + SLAB
HEADS_PER_STEP = 2
PAIRS_PER_STEP = 2
ONES_ROWS = 16
VT_ROWS = HEADS_PER_STEP * VDIM + ONES_ROWS
PROJ_TILE = 512
ATTN_TQ = 512
ATTN_TK = 512
MERGE_TILE = 1024
MERGE_CHUNK = 256
VMEM_LIMIT = 56 * 1024 * 1024

BF16 = jnp.bfloat16
F32 = jnp.float32
NT_DIMS = (((1,), (1,)), ((), ()))


def _rms(x, g):
    return x * lax.rsqrt(jnp.mean(x * x, axis=-1, keepdims=True) + EPS) * g


def _sigmoid(x):
    return 1.0 / (1.0 + jnp.exp(-x))


def _proj_kernel(x_ref, nin_ref, wa_ref, wg_ref, qn_ref, wuqt_ref, kvn_ref, wuk_ref, wuvt_ref,
                 poolw_ref, pscale_ref, qcos_ref, qsin_ref, kc_ref, ks1_ref, ks2_ref,
                 qt_out, k_out, vt_out, sga_out, ypool_out, gates_out, carry_ref):
    t = PROJ_TILE
    si = pl.program_id(1)

    @pl.when(si == 0)
    def _():
        carry_ref[...] = jnp.zeros_like(carry_ref)

    x = x_ref[0]
    hn = _rms(x, nin_ref[...]).astype(BF16)

    za = jnp.dot(hn, wa_ref[...], preferred_element_type=F32)
    zq = za[:, :Q_RANK]
    zkv = za[:, Q_RANK:Q_RANK + KV_RANK]
    zkr = pltpu.roll(za[:, Q_RANK + KV_RANK:], NOPE, axis=1)
    cq = _rms(zq, qn_ref[...]).astype(BF16)
    ckv = _rms(zkv, kvn_ref[...]).astype(BF16)

    qt = lax.dot_general(wuqt_ref[...], cq, NT_DIMS, preferred_element_type=F32)
    scale = QK_SCALE
    cos, sin = qcos_ref[...], qsin_ref[...]
    half = ROPE // 2
    tq = ATTN_TQ
    for h in range(HEADS):
        r0 = h * SLAB
        x1 = qt[r0 + NOPE:r0 + NOPE + half]
        x2 = qt[r0 + NOPE + half:r0 + NOPE + ROPE]
        slab = jnp.concatenate([qt[r0:r0 + NOPE] * scale, x1 * cos - x2 * sin, x1 * sin + x2 * cos,
                                jnp.zeros((SLAB - NOPE - ROPE, t), F32)], axis=0).astype(BF16)
        pr, hs = h // HEADS_PER_STEP, h % HEADS_PER_STEP
        for c in range(t // tq):
            for cs in range(HEADS_PER_STEP):
                blk = slab[:, c * tq:(c + 1) * tq] if cs == hs else jnp.zeros((SLAB, tq), BF16)
                qt_out[0, pr, c, hs * SLAB:(hs + 1) * SLAB, cs * tq:(cs + 1) * tq] = blk

    kf = jnp.dot(ckv, wuk_ref[...], preferred_element_type=F32)
    kr = (zkr * kc_ref[...] + pltpu.roll(zkr, LANES - half, axis=1) * ks1_ref[...]
          + pltpu.roll(zkr, half, axis=1) * ks2_ref[...])
    for h in range(HEADS):
        sl = slice(h * SLAB, (h + 1) * SLAB)
        k_out[0, :, sl] = (kf[:, sl] + kr).astype(BF16)

    vt = lax.dot_general(wuvt_ref[...], ckv, NT_DIMS, preferred_element_type=F32)
    for c in range(t // ATTN_TK):
        vt_out[0, :, c, :LANES, :] = vt[:, c * ATTN_TK:(c + 1) * ATTN_TK].reshape(
            HEADS // HEADS_PER_STEP, LANES, ATTN_TK).astype(BF16)
        vt_out[0, :, c, LANES:, :] = jnp.ones((HEADS // HEADS_PER_STEP, ONES_ROWS, ATTN_TK), BF16)

    zg = jnp.dot(hn, wg_ref[:, :3 * POOL_WIDTH], preferred_element_type=F32)
    g_attn = zg[:, :MLA_WIDTH]
    u = zg[:, MLA_WIDTH:MLA_WIDTH + POOL_WIDTH]
    g_pool = zg[:, MLA_WIDTH + POOL_WIDTH:]
    sga_out[0] = (g_attn * _sigmoid(g_attn)).astype(BF16)
    for c in range(4):
        lo = 3 * POOL_WIDTH + c * 512
        gm = jnp.dot(hn, wg_ref[:, lo:lo + 512], preferred_element_type=F32)
        gates_out[0, :, c * 512:(c + 1) * 512] = _sigmoid(gm).astype(BF16)

    ue = jnp.concatenate([carry_ref[...], u], axis=0)
    carry_ref[...] = u[t - HALO:, :]
    pos = si * t + lax.broadcasted_iota(jnp.int32, (t, 1), 0) + 1
    outs = []
    for gi, w in enumerate(POOL_WINDOWS):
        sl = slice(gi * POOL_GROUP, (gi + 1) * POOL_GROUP)
        acc = ue[:, sl]
        step = 1
        while step < w:
            acc = acc + pltpu.roll(acc, step, axis=0)
            step *= 2
        inv = 1.0 / jnp.minimum(pos, w).astype(F32)
        d = acc[HALO:, :] * inv - u[:, sl]
        outs.append(jnp.dot(d.astype(BF16), poolw_ref[gi], preferred_element_type=F32))
    y = jnp.concatenate(outs, axis=-1) * pscale_ref[...]
    ypool_out[0] = (y * (g_pool * _sigmoid(g_pool))).astype(BF16)


def _attn_kernel(qbd_ref, k_ref, vt_ref, sga_ref, o_ref, sa_ref, sb_ref, mxa_ref, mxb_ref,
                 m_ref, l_ref, acc_ref):
    tq, tk = ATTN_TQ, ATTN_TK
    nq = HEADS_PER_STEP * tq
    n_qt = qbd_ref.shape[2]
    kblocks = tk // CHUNK
    qry_chunk = lax.shift_right_logical(lax.broadcasted_iota(jnp.int32, (1, 1, nq), 2) & (tq - 1), 6)
    key_block = lax.broadcasted_iota(jnp.int32, (kblocks, 1, 1), 0)

    def score_tile(s_ref, mx_ref, qi, j, mask_for):
        if mask_for is not None:
            allowed = key_block + (j * kblocks - mask_for * (tq // CHUNK)) <= qry_chunk
            bias = jnp.where(allowed, 0.0, -jnp.inf).astype(F32)
        for g in range(PAIRS_PER_STEP):
            kt = k_ref[0, pl.ds(pl.multiple_of(j * tk, tk), tk),
                       g * HEADS_PER_STEP * SLAB:(g + 1) * HEADS_PER_STEP * SLAB]
            s = jnp.dot(kt, qbd_ref[0, g, qi], preferred_element_type=F32)
            if mask_for is not None:
                s = (s.reshape(kblocks, CHUNK, nq) + bias).reshape(tk, nq)
            s_ref[g] = s
            mx_ref[g] = jnp.max(s, axis=0, keepdims=True)

    def process(s_ref, mx_ref, j):
        for g in range(PAIRS_PER_STEP):
            m_old = m_ref[g]
            m_new = jnp.maximum(m_old, mx_ref[g])
            p = jnp.exp2(s_ref[g] - m_new).astype(BF16)
            alpha = jnp.exp2(m_old - m_new)
            m_ref[g] = m_new
            pv = jnp.dot(vt_ref[0, g, j], p, preferred_element_type=F32)
            l_ref[g] = alpha * l_ref[g] + pv[LANES:LANES + 1, :]
            for h in range(HEADS_PER_STEP):
                cols = slice(h * tq, (h + 1) * tq)
                acc_ref[g, h] = alpha[:, cols] * acc_ref[g, h] + pv[h * VDIM:(h + 1) * VDIM, cols]

    score_tile(sa_ref, mxa_ref, 0, 0, 0)

    def query_tile(i, carry):
        last = ((i + 1) * tq + tk - 1) // tk - 1
        i_next = jnp.minimum(i + 1, n_qt - 1)
        m_ref[...] = jnp.full_like(m_ref, -jnp.inf)
        l_ref[...] = jnp.zeros_like(l_ref)
        acc_ref[...] = jnp.zeros_like(acc_ref)

        def pair(jj, c):
            j = 2 * jj
            score_tile(sb_ref, mxb_ref, i, j + 1, None)
            process(sa_ref, mxa_ref, j)
            score_tile(sa_ref, mxa_ref, i, j + 2, i)
            process(sb_ref, mxb_ref, j + 1)
            return c

        lax.fori_loop(0, last // 2, pair, 0)

        @pl.when(last % 2 == 1)
        def _():
            score_tile(sb_ref, mxb_ref, i, last, i)
            process(sa_ref, mxa_ref, last - 1)
            score_tile(sa_ref, mxa_ref, i_next, 0, i_next)
            process(sb_ref, mxb_ref, last)

        @pl.when(last % 2 == 0)
        def _():
            process(sa_ref, mxa_ref, last)
            score_tile(sa_ref, mxa_ref, i_next, 0, i_next)

        rows = pl.ds(pl.multiple_of(i * tq, tq), tq)
        for g in range(PAIRS_PER_STEP):
            inv_l = 1.0 / l_ref[g]
            ot = jnp.concatenate([acc_ref[g, h] * inv_l[:, h * tq:(h + 1) * tq]
                                  for h in range(HEADS_PER_STEP)], axis=0)
            cols = slice(g * LANES, (g + 1) * LANES)
            o_ref[0, rows, cols] = (ot.T * sga_ref[0, rows, cols].astype(F32)).astype(BF16)
        return carry

    lax.fori_loop(0, n_qt, query_tile, 0)


def _merge_kernel(x_ref, ya_ref, ypool_ref, gates_ref, wa_ref, wp_ref, wo_ref, nf_ref, out_ref):
    for c in range(MERGE_TILE // MERGE_CHUNK):
        rows = slice(c * MERGE_CHUNK, (c + 1) * MERGE_CHUNK)
        a = jnp.dot(ya_ref[0, rows, :], wa_ref[...], preferred_element_type=F32)
        p = jnp.dot(ypool_ref[0, rows, :], wp_ref[...], preferred_element_type=F32)
        ga = gates_ref[0, rows, :D_MODEL].astype(F32)
        gp = gates_ref[0, rows, D_MODEL:].astype(F32)
        merged = (ga * a + gp * p).astype(BF16)
        h = x_ref[0, rows, :] + jnp.dot(merged, wo_ref[...], preferred_element_type=F32)
        out_ref[0, rows, :] = _rms(h, nf_ref[...])


def _rope_angles(seq):
    half = ROPE // 2
    inv_freq = ROPE_THETA ** (-jnp.arange(half, dtype=F32) / half)
    ang = jnp.arange(seq, dtype=F32)[:, None] * inv_freq[None, :]
    return jnp.cos(ang), jnp.sin(ang)


def _key_rope_tables(cos, sin):
    seq, half = cos.shape
    zeros_n = jnp.zeros((seq, NOPE), F32)
    zeros_h = jnp.zeros((seq, half), F32)
    zeros_p = jnp.zeros((seq, SLAB - NOPE - ROPE), F32)
    c = jnp.concatenate([zeros_n, cos, cos, zeros_p], axis=1)
    s1 = jnp.concatenate([zeros_n, -sin, zeros_h, zeros_p], axis=1)
    s2 = jnp.concatenate([zeros_n, zeros_h, sin, zeros_p], axis=1)
    return c, s1, s2


def _const_spec(shape):
    return pl.BlockSpec(shape, lambda *_: (0,) * len(shape), pipeline_mode=pl.Buffered(1))


def kernel(x, norm_in, w_in, q_norm, w_uq, kv_norm, w_ukv, pool_w, pool_scale,
           w_branch_attn, w_branch_pool, w_out, norm_final):
    b, s, d = x.shape
    tq, tk, t1, t3 = ATTN_TQ, ATTN_TK, PROJ_TILE, MERGE_TILE
    assert d == D_MODEL and s % t1 == 0 and t1 % tk == 0 and t1 % tq == 0 and s % t3 == 0
    hp = HEADS // HEADS_PER_STEP
    nq = HEADS_PER_STEP * tq
    g2 = PAIRS_PER_STEP

    o_kv = Q_RANK
    o_kr = o_kv + KV_RANK
    o_ga = o_kr + ROPE
    w_in16 = w_in.astype(BF16)
    w_g = w_in16[:, o_ga:]
    w_uq_p = jnp.pad(w_uq, ((0, 0), (0, 0), (0, SLAB - NOPE - ROPE)))
    w_uqt = w_uq_p.reshape(Q_RANK, HEADS * SLAB).T.astype(BF16)
    w_uk_p = jnp.pad(w_ukv[:, :, :NOPE], ((0, 0), (0, 0), (0, SLAB - NOPE)))
    w_uk = w_uk_p.reshape(KV_RANK, HEADS * SLAB).astype(BF16)
    w_uvt = w_ukv[:, :, NOPE:].reshape(KV_RANK, MLA_WIDTH).T.astype(BF16)
    cos, sin = _rope_angles(s)
    qcos, qsin = (cos * QK_SCALE).T, (sin * QK_SCALE).T
    kc, ks1, ks2 = _key_rope_tables(cos, sin)

    row = lambda v: v.reshape(1, -1).astype(F32)
    tok = lambda width: pl.BlockSpec((1, t1, width), lambda bi, si: (bi, si, 0))
    tab = pl.BlockSpec((t1, SLAB), lambda bi, si: (si, 0))
    tabt = pl.BlockSpec((ROPE // 2, t1), lambda bi, si: (0, si))
    qt, k, vt, sga, ypool, gates = pl.pallas_call(
        _proj_kernel,
        grid=(b, s // t1),
        in_specs=[tok(d), _const_spec((1, d)), _const_spec((d, LATENT_COLS)), _const_spec(w_g.shape),
                  _const_spec((1, Q_RANK)), _const_spec(w_uqt.shape), _const_spec((1, KV_RANK)),
                  _const_spec(w_uk.shape), _const_spec(w_uvt.shape),
                  _const_spec(pool_w.shape), _const_spec((1, POOL_WIDTH)),
                  tabt, tabt, tab, tab, tab],
        out_specs=[pl.BlockSpec((1, hp, t1 // tq, HEADS_PER_STEP * SLAB, nq),
                                lambda bi, si: (bi, 0, si, 0, 0)),
                   tok(HEADS * SLAB),
                   pl.BlockSpec((1, hp, t1 // tk, VT_ROWS, tk), lambda bi, si: (bi, 0, si, 0, 0)),
                   tok(MLA_WIDTH), tok(POOL_WIDTH), tok(2 * d)],
        out_shape=[jax.ShapeDtypeStruct((b, hp, s // tq, HEADS_PER_STEP * SLAB, nq), BF16),
                   jax.ShapeDtypeStruct((b, s, HEADS * SLAB), BF16),
                   jax.ShapeDtypeStruct((b, hp, s // tk, VT_ROWS, tk), BF16),
                   jax.ShapeDtypeStruct((b, s, MLA_WIDTH), BF16),
                   jax.ShapeDtypeStruct((b, s, POOL_WIDTH), BF16),
                   jax.ShapeDtypeStruct((b, s, 2 * d), BF16)],
        scratch_shapes=[pltpu.VMEM((HALO, POOL_WIDTH), F32)],
        compiler_params=pltpu.CompilerParams(
            dimension_semantics=("arbitrary", "arbitrary"), vmem_limit_bytes=VMEM_LIMIT),
        name="proj",
    )(x, row(norm_in), w_in16, w_g, row(q_norm), w_uqt, row(kv_norm), w_uk, w_uvt,
      pool_w.astype(BF16), row(pool_scale), qcos, qsin, kc, ks1, ks2)

    o = pl.pallas_call(
        _attn_kernel,
        grid=(b, hp // g2),
        in_specs=[pl.BlockSpec((1, g2, s // tq, HEADS_PER_STEP * SLAB, nq), lambda bi, pi: (bi, pi, 0, 0, 0)),
                  pl.BlockSpec((1, s, g2 * HEADS_PER_STEP * SLAB), lambda bi, pi: (bi, 0, pi)),
                  pl.BlockSpec((1, g2, s // tk, VT_ROWS, tk), lambda bi, pi: (bi, pi, 0, 0, 0)),
                  pl.BlockSpec((1, s, g2 * LANES), lambda bi, pi: (bi, 0, pi))],
        out_specs=pl.BlockSpec((1, s, g2 * LANES), lambda bi, pi: (bi, 0, pi)),
        out_shape=jax.ShapeDtypeStruct((b, s, MLA_WIDTH), BF16),
        scratch_shapes=[pltpu.VMEM((g2, tk, nq), F32), pltpu.VMEM((g2, tk, nq), F32),
                        pltpu.VMEM((g2, 1, nq), F32), pltpu.VMEM((g2, 1, nq), F32),
                        pltpu.VMEM((g2, 1, nq), F32), pltpu.VMEM((g2, 1, nq), F32),
                        pltpu.VMEM((g2, HEADS_PER_STEP, VDIM, tq), F32)],
        compiler_params=pltpu.CompilerParams(
            dimension_semantics=("arbitrary", "arbitrary"), vmem_limit_bytes=VMEM_LIMIT),
        name="attn",
    )(qt, k, vt, sga)

    tok3 = lambda width: pl.BlockSpec((1, t3, width), lambda bi, si: (bi, si, 0))
    out = pl.pallas_call(
        _merge_kernel,
        grid=(b, s // t3),
        in_specs=[tok3(d), tok3(MLA_WIDTH), tok3(POOL_WIDTH), tok3(2 * d),
                  _const_spec((MLA_WIDTH, d)), _const_spec((POOL_WIDTH, d)), _const_spec((d, d)),
                  _const_spec((1, d))],
        out_specs=tok3(d),
        out_shape=jax.ShapeDtypeStruct((b, s, d), x.dtype),
        compiler_params=pltpu.CompilerParams(
            dimension_semantics=("arbitrary", "arbitrary"), vmem_limit_bytes=VMEM_LIMIT),
        name="merge",
    )(x, o, ypool, gates, w_branch_attn.astype(BF16), w_branch_pool.astype(BF16),
      w_out.astype(BF16), row(norm_final))
    return out
```

```python
import jax
import jax.numpy as jnp
from jax import lax
from jax.experimental import pallas as pl
from jax.experimental.pallas import tpu as pltpu

D_MODEL = 1024
CHUNK = 64
HEADS = 8
NOPE = 64
ROPE = 32
VDIM = 64
Q_RANK = 384
KV_RANK = 256
MLA_WIDTH = HEADS * VDIM
ROPE_THETA = 10000.0
POOL_WINDOWS = (2, 4, 8, 16)
POOL_WIDTH = D_MODEL // 2
POOL_GROUP = POOL_WIDTH // len(POOL_WINDOWS)
EPS = 1e-6
QK_SCALE = (NOPE + ROPE) ** -0.5 * 1.4426950408889634

LANES = 128
HALO = 16
SLAB = LANES
LATENT_COLS = Q_RANK + KV_RANK + SLAB
PREP_ROWS = 128
HEADS_PER_STEP = 2
PAIRS_PER_STEP = 2
ONES_ROWS = 16
VT_ROWS = HEADS_PER_STEP * VDIM + ONES_ROWS
PROJ_TILE = 512
ATTN_TQ = 512
ATTN_TK = 512
MERGE_TILE = 1024
VMEM_LIMIT = 56 * 1024 * 1024

BF16 = jnp.bfloat16
F32 = jnp.float32
NT_DIMS = (((1,), (1,)), ((), ()))


def _rms(x, g):
    return x * lax.rsqrt(jnp.mean(x * x, axis=-1, keepdims=True) + EPS) * g


def _sigmoid(x):
    return 1.0 / (1.0 + jnp.exp(-x))


def _weight_prep_kernel(w_ref, wa_ref, wg_ref):
    rows = w_ref.shape[0]
    o_kr = Q_RANK + KV_RANK
    wa_ref[:, :o_kr] = w_ref[:, :o_kr].astype(BF16)
    wa_ref[:, o_kr:] = jnp.zeros((rows, SLAB), BF16)
    wa_ref[:, o_kr + NOPE:o_kr + NOPE + ROPE] = w_ref[:, o_kr:o_kr + ROPE].astype(BF16)
    wg_ref[...] = w_ref[:, o_kr + ROPE:].astype(BF16)


def _proj_kernel(x_ref, nin_ref, wa_ref, wg_ref, qn_ref, wuqt_ref, kvn_ref, wuk_ref, wuvt_ref,
                 poolw_ref, pscale_ref, qcos_ref, qsin_ref, kc_ref, ks1_ref, ks2_ref,
                 qt_out, k_out, vt_out, sga_out, ypool_out, gates_out, carry_ref):
    t = PROJ_TILE
    si = pl.program_id(1)

    @pl.when(si == 0)
    def _():
        carry_ref[...] = jnp.zeros_like(carry_ref)

    x = x_ref[0]
    hn = _rms(x, nin_ref[...]).astype(BF16)

    za = jnp.dot(hn, wa_ref[...], preferred_element_type=F32)
    zq = za[:, :Q_RANK]
    zkv = za[:, Q_RANK:Q_RANK + KV_RANK]
    zkr = za[:, Q_RANK + KV_RANK:]
    cq = _rms(zq, qn_ref[...]).astype(BF16)
    ckv = _rms(zkv, kvn_ref[...]).astype(BF16)

    qt = lax.dot_general(wuqt_ref[...], cq, NT_DIMS, preferred_element_type=F32)
    scale = QK_SCALE
    cos, sin = qcos_ref[...], qsin_ref[...]
    half = ROPE // 2
    tq = ATTN_TQ
    for h in range(HEADS):
        r0 = h * SLAB
        x1 = qt[r0 + NOPE:r0 + NOPE + half]
        x2 = qt[r0 + NOPE + half:r0 + NOPE + ROPE]
        slab = jnp.concatenate([qt[r0:r0 + NOPE] * scale, x1 * cos - x2 * sin, x1 * sin + x2 * cos,
                                jnp.zeros((SLAB - NOPE - ROPE, t), F32)], axis=0).astype(BF16)
        pr, hs = h // HEADS_PER_STEP, h % HEADS_PER_STEP
        for c in range(t // tq):
            for cs in range(HEADS_PER_STEP):
                blk = slab[:, c * tq:(c + 1) * tq] if cs == hs else jnp.zeros((SLAB, tq), BF16)
                qt_out[0, pr, c, hs * SLAB:(hs + 1) * SLAB, cs * tq:(cs + 1) * tq] = blk

    kf = jnp.dot(ckv, wuk_ref[...], preferred_element_type=F32)
    kr = (zkr * kc_ref[...] + pltpu.roll(zkr, LANES - half, axis=1) * ks1_ref[...]
          + pltpu.roll(zkr, half, axis=1) * ks2_ref[...])
    for h in range(HEADS):
        sl = slice(h * SLAB, (h + 1) * SLAB)
        k_out[0, :, sl] = (kf[:, sl] + kr).astype(BF16)

    vt = lax.dot_general(wuvt_ref[...], ckv, NT_DIMS, preferred_element_type=F32)
    for c in range(t // ATTN_TK):
        vt_out[0, :, c, :LANES, :] = vt[:, c * ATTN_TK:(c + 1) * ATTN_TK].reshape(
            HEADS // HEADS_PER_STEP, LANES, ATTN_TK).astype(BF16)
        vt_out[0, :, c, LANES:, :] = jnp.ones((HEADS // HEADS_PER_STEP, ONES_ROWS, ATTN_TK), BF16)

    zg = jnp.dot(hn, wg_ref[:, :3 * POOL_WIDTH], preferred_element_type=F32)
    g_attn = zg[:, :MLA_WIDTH]
    u = zg[:, MLA_WIDTH:MLA_WIDTH + POOL_WIDTH]
    g_pool = zg[:, MLA_WIDTH + POOL_WIDTH:]
    sga_out[0] = (g_attn * _sigmoid(g_attn)).astype(BF16)
    for c in range(4):
        lo = 3 * POOL_WIDTH + c * 512
        gm = jnp.dot(hn, wg_ref[:, lo:lo + 512], preferred_element_type=F32)
        gates_out[0, :, c * 512:(c + 1) * 512] = _sigmoid(gm).astype(BF16)

    ue = jnp.concatenate([carry_ref[...], u], axis=0)
    carry_ref[...] = u[t - HALO:, :]
    pos = si * t + lax.broadcasted_iota(jnp.int32, (t, 1), 0) + 1
    outs = []
    for gi, w in enumerate(POOL_WINDOWS):
        sl = slice(gi * POOL_GROUP, (gi + 1) * POOL_GROUP)
        acc = ue[:, sl]
        step = 1
        while step < w:
            acc = acc + pltpu.roll(acc, step, axis=0)
            step *= 2
        inv = 1.0 / jnp.minimum(pos, w).astype(F32)
        d = acc[HALO:, :] * inv - u[:, sl]
        outs.append(jnp.dot(d.astype(BF16), poolw_ref[gi], preferred_element_type=F32))
    y = jnp.concatenate(outs, axis=-1) * pscale_ref[...]
    ypool_out[0] = (y * (g_pool * _sigmoid(g_pool))).astype(BF16)


def _attn_kernel(qbd_ref, k_ref, vt_ref, sga_ref, o_ref, sa_ref, sb_ref, mxa_ref, mxb_ref,
                 m_ref, l_ref, acc_ref):
    tq, tk = ATTN_TQ, ATTN_TK
    nq = HEADS_PER_STEP * tq
    n_qt = qbd_ref.shape[2]
    kblocks = tk // CHUNK
    qry_chunk = lax.shift_right_logical(lax.broadcasted_iota(jnp.int32, (1, 1, nq), 2) & (tq - 1), 6)
    key_block = lax.broadcasted_iota(jnp.int32, (kblocks, 1, 1), 0)

    def score_tile(s_ref, mx_ref, qi, j, mask_for):
        if mask_for is not None:
            allowed = key_block + (j * kblocks - mask_for * (tq // CHUNK)) <= qry_chunk
            bias = jnp.where(allowed, 0.0, -jnp.inf).astype(F32)
        for g in range(PAIRS_PER_STEP):
            kt = k_ref[0, pl.ds(pl.multiple_of(j * tk, tk), tk),
                       g * HEADS_PER_STEP * SLAB:(g + 1) * HEADS_PER_STEP * SLAB]
            s = jnp.dot(kt, qbd_ref[0, g, qi], preferred_element_type=F32)
            if mask_for is not None:
                s = (s.reshape(kblocks, CHUNK, nq) + bias).reshape(tk, nq)
            s_ref[g] = s
            mx_ref[g] = jnp.max(s, axis=0, keepdims=True)

    def process(s_ref, mx_ref, j):
        for g in range(PAIRS_PER_STEP):
            m_old = m_ref[g]
            m_new = jnp.maximum(m_old, mx_ref[g])
            p = jnp.exp2(s_ref[g] - m_new).astype(BF16)
            alpha = jnp.exp2(m_old - m_new)
            m_ref[g] = m_new
            pv = jnp.dot(vt_ref[0, g, j], p, preferred_element_type=F32)
            l_ref[g] = alpha * l_ref[g] + pv[LANES:LANES + 1, :]
            for h in range(HEADS_PER_STEP):
                cols = slice(h * tq, (h + 1) * tq)
                acc_ref[g, h] = alpha[:, cols] * acc_ref[g, h] + pv[h * VDIM:(h + 1) * VDIM, cols]

    score_tile(sa_ref, mxa_ref, 0, 0, 0)

    def query_tile(i, carry):
        last = ((i + 1) * tq + tk - 1) // tk - 1
        i_next = jnp.minimum(i + 1, n_qt - 1)
        m_ref[...] = jnp.full_like(m_ref, -jnp.inf)
        l_ref[...] = jnp.zeros_like(l_ref)
        acc_ref[...] = jnp.zeros_like(acc_ref)

        def pair(jj, c):
            j = 2 * jj
            score_tile(sb_ref, mxb_ref, i, j + 1, None)
            process(sa_ref, mxa_ref, j)
            score_tile(sa_ref, mxa_ref, i, j + 2, i)
            process(sb_ref, mxb_ref, j + 1)
            return c

        lax.fori_loop(0, last // 2, pair, 0)

        @pl.when(last % 2 == 1)
        def _():
            score_tile(sb_ref, mxb_ref, i, last, i)
            process(sa_ref, mxa_ref, last - 1)
            score_tile(sa_ref, mxa_ref, i_next, 0, i_next)
            process(sb_ref, mxb_ref, last)

        @pl.when(last % 2 == 0)
        def _():
            process(sa_ref, mxa_ref, last)
            score_tile(sa_ref, mxa_ref, i_next, 0, i_next)

        rows = pl.ds(pl.multiple_of(i * tq, tq), tq)
        for g in range(PAIRS_PER_STEP):
            inv_l = 1.0 / l_ref[g]
            ot = jnp.concatenate([acc_ref[g, h] * inv_l[:, h * tq:(h + 1) * tq]
                                  for h in range(HEADS_PER_STEP)], axis=0)
            cols = slice(g * LANES, (g + 1) * LANES)
            o_ref[0, rows, cols] = (ot.T * sga_ref[0, rows, cols].astype(F32)).astype(BF16)
        return carry

    lax.fori_loop(0, n_qt, query_tile, 0)


def _merge_kernel(x_ref, ya_ref, ypool_ref, gates_ref, wa_ref, wp_ref, wo_ref, nf_ref, out_ref):
    a = jnp.dot(ya_ref[0], wa_ref[...], preferred_element_type=F32)
    p = jnp.dot(ypool_ref[0], wp_ref[...], preferred_element_type=F32)
    ga = gates_ref[0, :, :D_MODEL].astype(F32)
    gp = gates_ref[0, :, D_MODEL:].astype(F32)
    merged = (ga * a + gp * p).astype(BF16)
    h = x_ref[0] + jnp.dot(merged, wo_ref[...], preferred_element_type=F32)
    out_ref[0] = _rms(h, nf_ref[...])


def _rope_angles(seq):
    half = ROPE // 2
    inv_freq = ROPE_THETA ** (-jnp.arange(half, dtype=F32) / half)
    ang = jnp.arange(seq, dtype=F32)[:, None] * inv_freq[None, :]
    return jnp.cos(ang), jnp.sin(ang)


def _key_rope_tables(cos, sin):
    seq, half = cos.shape
    zeros_n = jnp.zeros((seq, NOPE), F32)
    zeros_h = jnp.zeros((seq, half), F32)
    zeros_p = jnp.zeros((seq, SLAB - NOPE - ROPE), F32)
    c = jnp.concatenate([zeros_n, cos, cos, zeros_p], axis=1)
    s1 = jnp.concatenate([zeros_n, -sin, zeros_h, zeros_p], axis=1)
    s2 = jnp.concatenate([zeros_n, zeros_h, sin, zeros_p], axis=1)
    return c, s1, s2


def _const_spec(shape):
    return pl.BlockSpec(shape, lambda *_: (0,) * len(shape), pipeline_mode=pl.Buffered(1))


def kernel(x, norm_in, w_in, q_norm, w_uq, kv_norm, w_ukv, pool_w, pool_scale,
           w_branch_attn, w_branch_pool, w_out, norm_final):
    b, s, d = x.shape
    tq, tk, t1, t3 = ATTN_TQ, ATTN_TK, PROJ_TILE, MERGE_TILE
    assert d == D_MODEL and s % t1 == 0 and t1 % tk == 0 and t1 % tq == 0 and s % t3 == 0
    hp = HEADS // HEADS_PER_STEP
    nq = HEADS_PER_STEP * tq
    g2 = PAIRS_PER_STEP

    n_gate = w_in.shape[1] - (Q_RANK + KV_RANK + ROPE)
    w_a, w_g = pl.pallas_call(
        _weight_prep_kernel,
        grid=(d // PREP_ROWS,),
        in_specs=[pl.BlockSpec((PREP_ROWS, w_in.shape[1]), lambda i: (i, 0))],
        out_specs=[pl.BlockSpec((PREP_ROWS, LATENT_COLS), lambda i: (i, 0)),
                   pl.BlockSpec((PREP_ROWS, n_gate), lambda i: (i, 0))],
        out_shape=[jax.ShapeDtypeStruct((d, LATENT_COLS), BF16),
                   jax.ShapeDtypeStruct((d, n_gate), BF16)],
        name="weight_prep",
    )(w_in)
    w_uq_p = jnp.pad(w_uq, ((0, 0), (0, 0), (0, SLAB - NOPE - ROPE)))
    w_uqt = w_uq_p.reshape(Q_RANK, HEADS * SLAB).T.astype(BF16)
    w_uk_p = jnp.pad(w_ukv[:, :, :NOPE], ((0, 0), (0, 0), (0, SLAB - NOPE)))
    w_uk = w_uk_p.reshape(KV_RANK, HEADS * SLAB).astype(BF16)
    w_uvt = w_ukv[:, :, NOPE:].reshape(KV_RANK, MLA_WIDTH).T.astype(BF16)
    cos, sin = _rope_angles(s)
    qcos, qsin = (cos * QK_SCALE).T, (sin * QK_SCALE).T
    kc, ks1, ks2 = _key_rope_tables(cos, sin)

    row = lambda v: v.reshape(1, -1).astype(F32)
    tok = lambda width: pl.BlockSpec((1, t1, width), lambda bi, si: (bi, si, 0))
    tab = pl.BlockSpec((t1, SLAB), lambda bi, si: (si, 0))
    tabt = pl.BlockSpec((ROPE // 2, t1), lambda bi, si: (0, si))
    qt, k, vt, sga, ypool, gates = pl.pallas_call(
        _proj_kernel,
        grid=(b, s // t1),
        in_specs=[tok(d), _const_spec((1, d)), _const_spec(w_a.shape), _const_spec(w_g.shape),
                  _const_spec((1, Q_RANK)), _const_spec(w_uqt.shape), _const_spec((1, KV_RANK)),
                  _const_spec(w_uk.shape), _const_spec(w_uvt.shape),
                  _const_spec(pool_w.shape), _const_spec((1, POOL_WIDTH)),
                  tabt, tabt, tab, tab, tab],
        out_specs=[pl.BlockSpec((1, hp, t1 // tq, HEADS_PER_STEP * SLAB, nq),
                                lambda bi, si: (bi, 0, si, 0, 0)),
                   tok(HEADS * SLAB),
                   pl.BlockSpec((1, hp, t1 // tk, VT_ROWS, tk), lambda bi, si: (bi, 0, si, 0, 0)),
                   tok(MLA_WIDTH), tok(POOL_WIDTH), tok(2 * d)],
        out_shape=[jax.ShapeDtypeStruct((b, hp, s // tq, HEADS_PER_STEP * SLAB, nq), BF16),
                   jax.ShapeDtypeStruct((b, s, HEADS * SLAB), BF16),
                   jax.ShapeDtypeStruct((b, hp, s // tk, VT_ROWS, tk), BF16),
                   jax.ShapeDtypeStruct((b, s, MLA_WIDTH), BF16),
                   jax.ShapeDtypeStruct((b, s, POOL_WIDTH), BF16),
                   jax.ShapeDtypeStruct((b, s, 2 * d), BF16)],
        scratch_shapes=[pltpu.VMEM((HALO, POOL_WIDTH), F32)],
        compiler_params=pltpu.CompilerParams(
            dimension_semantics=("arbitrary", "arbitrary"), vmem_limit_bytes=VMEM_LIMIT),
        name="proj",
    )(x, row(norm_in), w_a, w_g, row(q_norm), w_uqt, row(kv_norm), w_uk, w_uvt,
      pool_w.astype(BF16), row(pool_scale), qcos, qsin, kc, ks1, ks2)

    o = pl.pallas_call(
        _attn_kernel,
        grid=(b, hp // g2),
        in_specs=[pl.BlockSpec((1, g2, s // tq, HEADS_PER_STEP * SLAB, nq), lambda bi, pi: (bi, pi, 0, 0, 0)),
                  pl.BlockSpec((1, s, g2 * HEADS_PER_STEP * SLAB), lambda bi, pi: (bi, 0, pi)),
                  pl.BlockSpec((1, g2, s // tk, VT_ROWS, tk), lambda bi, pi: (bi, pi, 0, 0, 0)),
                  pl.BlockSpec((1, s, g2 * LANES), lambda bi, pi: (bi, 0, pi))],
        out_specs=pl.BlockSpec((1, s, g2 * LANES), lambda bi, pi: (bi, 0, pi)),
        out_shape=jax.ShapeDtypeStruct((b, s, MLA_WIDTH), BF16),
        scratch_shapes=[pltpu.VMEM((g2, tk, nq), F32), pltpu.VMEM((g2, tk, nq), F32),
                        pltpu.VMEM((g2, 1, nq), F32), pltpu.VMEM((g2, 1, nq), F32),
                        pltpu.VMEM((g2, 1, nq), F32), pltpu.VMEM((g2, 1, nq), F32),
                        pltpu.VMEM((g2, HEADS_PER_STEP, VDIM, tq), F32)],
        compiler_params=pltpu.CompilerParams(
            dimension_semantics=("arbitrary", "arbitrary"), vmem_limit_bytes=VMEM_LIMIT),
        name="attn",
    )(qt, k, vt, sga)

    tok3 = lambda width: pl.BlockSpec((1, t3, width), lambda bi, si: (bi, si, 0))
    out = pl.pallas_call(
        _merge_kernel,
        grid=(b, s // t3),
        in_specs=[tok3(d), tok3(MLA_WIDTH), tok3(POOL_WIDTH), tok3(2 * d),
                  _const_spec((MLA_WIDTH, d)), _const_spec((POOL_WIDTH, d)), _const_spec((d, d)),
                  _const_spec((1, d))],
        out_specs=tok3(d),
        out_shape=jax.ShapeDtypeStruct((b, s, d), x.dtype),
        compiler_params=pltpu.CompilerParams(
            dimension_semantics=("arbitrary", "arbitrary"), vmem_limit_bytes=VMEM_LIMIT),
        name="merge",
    )(x, o, ypool, gates, w_branch_attn.astype(BF16), w_branch_pool.astype(BF16),
      w_out.astype(BF16), row(norm_final))
    return out
```

```python
import jax
import jax.numpy as jnp
from jax import lax
from jax.experimental import pallas as pl
from jax.experimental.pallas import tpu as pltpu

D_MODEL = 1024
CHUNK = 64
HEADS = 8
NOPE = 64
ROPE = 32
VDIM = 64
Q_RANK = 384
KV_RANK = 256
MLA_WIDTH = HEADS * VDIM
ROPE_THETA = 10000.0
POOL_WINDOWS = (2, 4, 8, 16)
POOL_WIDTH = D_MODEL // 2
POOL_GROUP = POOL_WIDTH // len(POOL_WINDOWS)
EPS = 1e-6
QK_SCALE = (NOPE + ROPE) ** -0.5 * 1.4426950408889634

LANES = 128
HALO = 16
SLAB = LANES
LATENT_COLS = Q_RANK + KV_RANK + SLAB
PREP_ROWS = 128
HEADS_PER_STEP = 2
PAIRS_PER_STEP = 2
ONES_ROWS = 16
VT_ROWS = HEADS_PER_STEP * VDIM + ONES_ROWS
PROJ_TILE = 512
ATTN_TQ = 512
ATTN_TK = 512
MERGE_TILE = 1024
VMEM_LIMIT = 56 * 1024 * 1024

BF16 = jnp.bfloat16
F32 = jnp.float32
NT_DIMS = (((1,), (1,)), ((), ()))


def _rms(x, g):
    return x * lax.rsqrt(jnp.mean(x * x, axis=-1, keepdims=True) + EPS) * g


def _sigmoid(x):
    return 1.0 / (1.0 + jnp.exp(-x))


def _weight_prep_kernel(wt_ref, wa_ref, wg_ref):
    o_kr = Q_RANK + KV_RANK
    wa_ref[:, :o_kr] = wt_ref[:o_kr, :].T.astype(BF16)
    kr = jnp.concatenate([jnp.zeros((NOPE, PREP_ROWS), F32), wt_ref[o_kr:o_kr + ROPE, :],
                          jnp.zeros((SLAB - NOPE - ROPE, PREP_ROWS), F32)], axis=0)
    wa_ref[:, o_kr:] = kr.T.astype(BF16)
    wg_ref[...] = wt_ref[o_kr + ROPE:, :].T.astype(BF16)


def _proj_kernel(x_ref, nin_ref, wa_ref, wg_ref, qn_ref, wuqt_ref, kvn_ref, wuk_ref, wuvt_ref,
                 poolw_ref, pscale_ref, qcos_ref, qsin_ref, kc_ref, ks1_ref, ks2_ref,
                 qt_out, k_out, vt_out, sga_out, ypool_out, gates_out, carry_ref):
    t = PROJ_TILE
    si = pl.program_id(1)

    @pl.when(si == 0)
    def _():
        carry_ref[...] = jnp.zeros_like(carry_ref)

    x = x_ref[0]
    hn = _rms(x, nin_ref[...]).astype(BF16)

    za = jnp.dot(hn, wa_ref[...], preferred_element_type=F32)
    zq = za[:, :Q_RANK]
    zkv = za[:, Q_RANK:Q_RANK + KV_RANK]
    zkr = za[:, Q_RANK + KV_RANK:]
    cq = _rms(zq, qn_ref[...]).astype(BF16)
    ckv = _rms(zkv, kvn_ref[...]).astype(BF16)

    qt = lax.dot_general(wuqt_ref[...], cq, NT_DIMS, preferred_element_type=F32)
    scale = QK_SCALE
    cos, sin = qcos_ref[...], qsin_ref[...]
    half = ROPE // 2
    tq = ATTN_TQ
    for h in range(HEADS):
        r0 = h * SLAB
        x1 = qt[r0 + NOPE:r0 + NOPE + half]
        x2 = qt[r0 + NOPE + half:r0 + NOPE + ROPE]
        slab = jnp.concatenate([qt[r0:r0 + NOPE] * scale, x1 * cos - x2 * sin, x1 * sin + x2 * cos,
                                jnp.zeros((SLAB - NOPE - ROPE, t), F32)], axis=0).astype(BF16)
        pr, hs = h // HEADS_PER_STEP, h % HEADS_PER_STEP
        for c in range(t // tq):
            for cs in range(HEADS_PER_STEP):
                blk = slab[:, c * tq:(c + 1) * tq] if cs == hs else jnp.zeros((SLAB, tq), BF16)
                qt_out[0, pr, c, hs * SLAB:(hs + 1) * SLAB, cs * tq:(cs + 1) * tq] = blk

    kf = jnp.dot(ckv, wuk_ref[...], preferred_element_type=F32)
    kr = (zkr * kc_ref[...] + pltpu.roll(zkr, LANES - half, axis=1) * ks1_ref[...]
          + pltpu.roll(zkr, half, axis=1) * ks2_ref[...])
    for h in range(HEADS):
        sl = slice(h * SLAB, (h + 1) * SLAB)
        k_out[0, :, sl] = (kf[:, sl] + kr).astype(BF16)

    vt = lax.dot_general(wuvt_ref[...], ckv, NT_DIMS, preferred_element_type=F32)
    for c in range(t // ATTN_TK):
        vt_out[0, :, c, :LANES, :] = vt[:, c * ATTN_TK:(c + 1) * ATTN_TK].reshape(
            HEADS // HEADS_PER_STEP, LANES, ATTN_TK).astype(BF16)
        vt_out[0, :, c, LANES:, :] = jnp.ones((HEADS // HEADS_PER_STEP, ONES_ROWS, ATTN_TK), BF16)

    zg = jnp.dot(hn, wg_ref[:, :3 * POOL_WIDTH], preferred_element_type=F32)
    g_attn = zg[:, :MLA_WIDTH]
    u = zg[:, MLA_WIDTH:MLA_WIDTH + POOL_WIDTH]
    g_pool = zg[:, MLA_WIDTH + POOL_WIDTH:]
    sga_out[0] = (g_attn * _sigmoid(g_attn)).astype(BF16)
    for c in range(4):
        lo = 3 * POOL_WIDTH + c * 512
        gm = jnp.dot(hn, wg_ref[:, lo:lo + 512], preferred_element_type=F32)
        gates_out[0, :, c * 512:(c + 1) * 512] = _sigmoid(gm).astype(BF16)

    ue = jnp.concatenate([carry_ref[...], u], axis=0)
    carry_ref[...] = u[t - HALO:, :]
    pos = si * t + lax.broadcasted_iota(jnp.int32, (t, 1), 0) + 1
    outs = []
    for gi, w in enumerate(POOL_WINDOWS):
        sl = slice(gi * POOL_GROUP, (gi + 1) * POOL_GROUP)
        acc = ue[:, sl]
        step = 1
        while step < w:
            acc = acc + pltpu.roll(acc, step, axis=0)
            step *= 2
        inv = 1.0 / jnp.minimum(pos, w).astype(F32)
        d = acc[HALO:, :] * inv - u[:, sl]
        outs.append(jnp.dot(d.astype(BF16), poolw_ref[gi], preferred_element_type=F32))
    y = jnp.concatenate(outs, axis=-1) * pscale_ref[...]
    ypool_out[0] = (y * (g_pool * _sigmoid(g_pool))).astype(BF16)


def _attn_kernel(qbd_ref, k_ref, vt_ref, sga_ref, o_ref, sa_ref, sb_ref, mxa_ref, mxb_ref,
                 m_ref, l_ref, acc_ref):
    tq, tk = ATTN_TQ, ATTN_TK
    nq = HEADS_PER_STEP * tq
    n_qt = qbd_ref.shape[2]
    kblocks = tk // CHUNK
    qry_chunk = lax.shift_right_logical(lax.broadcasted_iota(jnp.int32, (1, 1, nq), 2) & (tq - 1), 6)
    key_block = lax.broadcasted_iota(jnp.int32, (kblocks, 1, 1), 0)

    def score_tile(s_ref, mx_ref, qi, j, mask_for):
        if mask_for is not None:
            allowed = key_block + (j * kblocks - mask_for * (tq // CHUNK)) <= qry_chunk
            bias = jnp.where(allowed, 0.0, -jnp.inf).astype(F32)
        for g in range(PAIRS_PER_STEP):
            kt = k_ref[0, pl.ds(pl.multiple_of(j * tk, tk), tk),
                       g * HEADS_PER_STEP * SLAB:(g + 1) * HEADS_PER_STEP * SLAB]
            s = jnp.dot(kt, qbd_ref[0, g, qi], preferred_element_type=F32)
            if mask_for is not None:
                s = (s.reshape(kblocks, CHUNK, nq) + bias).reshape(tk, nq)
            s_ref[g] = s
            mx_ref[g] = jnp.max(s, axis=0, keepdims=True)

    def process(s_ref, mx_ref, j):
        for g in range(PAIRS_PER_STEP):
            m_old = m_ref[g]
            m_new = jnp.maximum(m_old, mx_ref[g])
            p = jnp.exp2(s_ref[g] - m_new).astype(BF16)
            alpha = jnp.exp2(m_old - m_new)
            m_ref[g] = m_new
            pv = jnp.dot(vt_ref[0, g, j], p, preferred_element_type=F32)
            l_ref[g] = alpha * l_ref[g] + pv[LANES:LANES + 1, :]
            for h in range(HEADS_PER_STEP):
                cols = slice(h * tq, (h + 1) * tq)
                acc_ref[g, h] = alpha[:, cols] * acc_ref[g, h] + pv[h * VDIM:(h + 1) * VDIM, cols]

    score_tile(sa_ref, mxa_ref, 0, 0, 0)

    def query_tile(i, carry):
        last = ((i + 1) * tq + tk - 1) // tk - 1
        i_next = jnp.minimum(i + 1, n_qt - 1)
        m_ref[...] = jnp.full_like(m_ref, -jnp.inf)
        l_ref[...] = jnp.zeros_like(l_ref)
        acc_ref[...] = jnp.zeros_like(acc_ref)

        def pair(jj, c):
            j = 2 * jj
            score_tile(sb_ref, mxb_ref, i, j + 1, None)
            process(sa_ref, mxa_ref, j)
            score_tile(sa_ref, mxa_ref, i, j + 2, i)
            process(sb_ref, mxb_ref, j + 1)
            return c

        lax.fori_loop(0, last // 2, pair, 0)

        @pl.when(last % 2 == 1)
        def _():
            score_tile(sb_ref, mxb_ref, i, last, i)
            process(sa_ref, mxa_ref, last - 1)
            score_tile(sa_ref, mxa_ref, i_next, 0, i_next)
            process(sb_ref, mxb_ref, last)

        @pl.when(last % 2 == 0)
        def _():
            process(sa_ref, mxa_ref, last)
            score_tile(sa_ref, mxa_ref, i_next, 0, i_next)

        rows = pl.ds(pl.multiple_of(i * tq, tq), tq)
        for g in range(PAIRS_PER_STEP):
            inv_l = 1.0 / l_ref[g]
            ot = jnp.concatenate([acc_ref[g, h] * inv_l[:, h * tq:(h + 1) * tq]
                                  for h in range(HEADS_PER_STEP)], axis=0)
            cols = slice(g * LANES, (g + 1) * LANES)
            o_ref[0, rows, cols] = (ot.T * sga_ref[0, rows, cols].astype(F32)).astype(BF16)
        return carry

    lax.fori_loop(0, n_qt, query_tile, 0)


def _merge_kernel(x_ref, ya_ref, ypool_ref, gates_ref, wa_ref, wp_ref, wo_ref, nf_ref, out_ref):
    a = jnp.dot(ya_ref[0], wa_ref[...], preferred_element_type=F32)
    p = jnp.dot(ypool_ref[0], wp_ref[...], preferred_element_type=F32)
    ga = gates_ref[0, :, :D_MODEL].astype(F32)
    gp = gates_ref[0, :, D_MODEL:].astype(F32)
    merged = (ga * a + gp * p).astype(BF16)
    h = x_ref[0] + jnp.dot(merged, wo_ref[...], preferred_element_type=F32)
    out_ref[0] = _rms(h, nf_ref[...])


def _rope_angles(seq):
    half = ROPE // 2
    inv_freq = ROPE_THETA ** (-jnp.arange(half, dtype=F32) / half)
    ang = jnp.arange(seq, dtype=F32)[:, None] * inv_freq[None, :]
    return jnp.cos(ang), jnp.sin(ang)


def _key_rope_tables(cos, sin):
    seq, half = cos.shape
    zeros_n = jnp.zeros((seq, NOPE), F32)
    zeros_h = jnp.zeros((seq, half), F32)
    zeros_p = jnp.zeros((seq, SLAB - NOPE - ROPE), F32)
    c = jnp.concatenate([zeros_n, cos, cos, zeros_p], axis=1)
    s1 = jnp.concatenate([zeros_n, -sin, zeros_h, zeros_p], axis=1)
    s2 = jnp.concatenate([zeros_n, zeros_h, sin, zeros_p], axis=1)
    return c, s1, s2


def _const_spec(shape):
    return pl.BlockSpec(shape, lambda *_: (0,) * len(shape), pipeline_mode=pl.Buffered(1))


def kernel(x, norm_in, w_in, q_norm, w_uq, kv_norm, w_ukv, pool_w, pool_scale,
           w_branch_attn, w_branch_pool, w_out, norm_final):
    b, s, d = x.shape
    tq, tk, t1, t3 = ATTN_TQ, ATTN_TK, PROJ_TILE, MERGE_TILE
    assert d == D_MODEL and s % t1 == 0 and t1 % tk == 0 and t1 % tq == 0 and s % t3 == 0
    hp = HEADS // HEADS_PER_STEP
    nq = HEADS_PER_STEP * tq
    g2 = PAIRS_PER_STEP

    n_gate = w_in.shape[1] - (Q_RANK + KV_RANK + ROPE)
    w_a, w_g = pl.pallas_call(
        _weight_prep_kernel,
        grid=(d // PREP_ROWS,),
        in_specs=[pl.BlockSpec((w_in.shape[1], PREP_ROWS), lambda i: (0, i))],
        out_specs=[pl.BlockSpec((PREP_ROWS, LATENT_COLS), lambda i: (i, 0)),
                   pl.BlockSpec((PREP_ROWS, n_gate), lambda i: (i, 0))],
        out_shape=[jax.ShapeDtypeStruct((d, LATENT_COLS), BF16),
                   jax.ShapeDtypeStruct((d, n_gate), BF16)],
        name="weight_prep",
    )(w_in.T)
    w_uq_p = jnp.pad(w_uq, ((0, 0), (0, 0), (0, SLAB - NOPE - ROPE)))
    w_uqt = w_uq_p.reshape(Q_RANK, HEADS * SLAB).T.astype(BF16)
    w_uk_p = jnp.pad(w_ukv[:, :, :NOPE], ((0, 0), (0, 0), (0, SLAB - NOPE)))
    w_uk = w_uk_p.reshape(KV_RANK, HEADS * SLAB).astype(BF16)
    w_uvt = w_ukv[:, :, NOPE:].reshape(KV_RANK, MLA_WIDTH).T.astype(BF16)
    cos, sin = _rope_angles(s)
    qcos, qsin = (cos * QK_SCALE).T, (sin * QK_SCALE).T
    kc, ks1, ks2 = _key_rope_tables(cos, sin)

    row = lambda v: v.reshape(1, -1).astype(F32)
    tok = lambda width: pl.BlockSpec((1, t1, width), lambda bi, si: (bi, si, 0))
    tab = pl.BlockSpec((t1, SLAB), lambda bi, si: (si, 0))
    tabt = pl.BlockSpec((ROPE // 2, t1), lambda bi, si: (0, si))
    qt, k, vt, sga, ypool, gates = pl.pallas_call(
        _proj_kernel,
        grid=(b, s // t1),
        in_specs=[tok(d), _const_spec((1, d)), _const_spec(w_a.shape), _const_spec(w_g.shape),
                  _const_spec((1, Q_RANK)), _const_spec(w_uqt.shape), _const_spec((1, KV_RANK)),
                  _const_spec(w_uk.shape), _const_spec(w_uvt.shape),
                  _const_spec(pool_w.shape), _const_spec((1, POOL_WIDTH)),
                  tabt, tabt, tab, tab, tab],
        out_specs=[pl.BlockSpec((1, hp, t1 // tq, HEADS_PER_STEP * SLAB, nq),
                                lambda bi, si: (bi, 0, si, 0, 0)),
                   tok(HEADS * SLAB),
                   pl.BlockSpec((1, hp, t1 // tk, VT_ROWS, tk), lambda bi, si: (bi, 0, si, 0, 0)),
                   tok(MLA_WIDTH), tok(POOL_WIDTH), tok(2 * d)],
        out_shape=[jax.ShapeDtypeStruct((b, hp, s // tq, HEADS_PER_STEP * SLAB, nq), BF16),
                   jax.ShapeDtypeStruct((b, s, HEADS * SLAB), BF16),
                   jax.ShapeDtypeStruct((b, hp, s // tk, VT_ROWS, tk), BF16),
                   jax.ShapeDtypeStruct((b, s, MLA_WIDTH), BF16),
                   jax.ShapeDtypeStruct((b, s, POOL_WIDTH), BF16),
                   jax.ShapeDtypeStruct((b, s, 2 * d), BF16)],
        scratch_shapes=[pltpu.VMEM((HALO, POOL_WIDTH), F32)],
        compiler_params=pltpu.CompilerParams(
            dimension_semantics=("arbitrary", "arbitrary"), vmem_limit_bytes=VMEM_LIMIT),
        name="proj",
    )(x, row(norm_in), w_a, w_g, row(q_norm), w_uqt, row(kv_norm), w_uk, w_uvt,
      pool_w.astype(BF16), row(pool_scale), qcos, qsin, kc, ks1, ks2)

    o = pl.pallas_call(
        _attn_kernel,
        grid=(b, hp // g2),
        in_specs=[pl.BlockSpec((1, g2, s // tq, HEADS_PER_STEP * SLAB, nq), lambda bi, pi: (bi, pi, 0, 0, 0)),
                  pl.BlockSpec((1, s, g2 * HEADS_PER_STEP * SLAB), lambda bi, pi: (bi, 0, pi)),
                  pl.BlockSpec((1, g2, s // tk, VT_ROWS, tk), lambda bi, pi: (bi, pi, 0, 0, 0)),
                  pl.BlockSpec((1, s, g2 * LANES), lambda bi, pi: (bi, 0, pi))],
        out_specs=pl.BlockSpec((1, s, g2 * LANES), lambda bi, pi: (bi, 0, pi)),
        out_shape=jax.ShapeDtypeStruct((b, s, MLA_WIDTH), BF16),
        scratch_shapes=[pltpu.VMEM((g2, tk, nq), F32), pltpu.VMEM((g2, tk, nq), F32),
                        pltpu.VMEM((g2, 1, nq), F32), pltpu.VMEM((g2, 1, nq), F32),
                        pltpu.VMEM((g2, 1, nq), F32), pltpu.VMEM((g2, 1, nq), F32),
                        pltpu.VMEM((g2, HEADS_PER_STEP, VDIM, tq), F32)],
        compiler_params=pltpu.CompilerParams(
            dimension_semantics=("arbitrary", "arbitrary"), vmem_limit_bytes=VMEM_LIMIT),
        name="attn",
    )(qt, k, vt, sga)

    tok3 = lambda width: pl.BlockSpec((1, t3, width), lambda bi, si: (bi, si, 0))
    out = pl.pallas_call(
        _merge_kernel,
        grid=(b, s // t3),
        in_specs=[tok3(d), tok3(MLA_WIDTH), tok3(POOL_WIDTH), tok3(2 * d),
                  _const_spec((MLA_WIDTH, d)), _const_spec((POOL_WIDTH, d)), _const_spec((d, d)),
                  _const_spec((1, d))],
        out_specs=tok3(d),
        out_shape=jax.ShapeDtypeStruct((b, s, d), x.dtype),
        compiler_params=pltpu.CompilerParams(
            dimension_semantics=("arbitrary", "arbitrary"), vmem_limit_bytes=VMEM_LIMIT),
        name="merge",
    )(x, o, ypool, gates, w_branch_attn.astype(BF16), w_branch_pool.astype(BF16),
      w_out.astype(BF16), row(norm_final))
    return out
```

```python
import jax
import jax.numpy as jnp
from jax import lax
from jax.experimental import pallas as pl
from jax.experimental.pallas import tpu as pltpu

D_MODEL = 1024
CHUNK = 64
HEADS = 8
NOPE = 64
ROPE = 32
VDIM = 64
Q_RANK = 384
KV_RANK = 256
MLA_WIDTH = HEADS * VDIM
ROPE_THETA = 10000.0
POOL_WINDOWS = (2, 4, 8, 16)
POOL_WIDTH = D_MODEL // 2
POOL_GROUP = POOL_WIDTH // len(POOL_WINDOWS)
EPS = 1e-6
QK_SCALE = (NOPE + ROPE) ** -0.5 * 1.4426950408889634

LANES = 128
HALO = 16
SLAB = LANES
LATENT_COLS = Q_RANK + KV_RANK + SLAB
PREP_ROWS = 128
HEADS_PER_STEP = 2
PAIRS_PER_STEP = 2
ONES_ROWS = 16
VH_ROWS = VDIM + ONES_ROWS
VT_ROWS = HEADS_PER_STEP * VH_ROWS
PROJ_TILE = 512
GATE_COLS = 512
ATTN_TQ = 512
ATTN_TK = 512
MERGE_TILE = 1024
VMEM_LIMIT = 56 * 1024 * 1024

BF16 = jnp.bfloat16
F32 = jnp.float32
NT_DIMS = (((1,), (1,)), ((), ()))


def _rms(x, g):
    return x * lax.rsqrt(jnp.mean(x * x, axis=-1, keepdims=True) + EPS) * g


def _sigmoid(x):
    return 1.0 / (1.0 + jnp.exp(-x))


def _weight_prep_kernel(wt_ref, wa_ref, wg_ref):
    o_kr = Q_RANK + KV_RANK
    wa_ref[:, :o_kr] = wt_ref[:o_kr, :].T.astype(BF16)
    kr = jnp.concatenate([jnp.zeros((NOPE, PREP_ROWS), F32), wt_ref[o_kr:o_kr + ROPE, :],
                          jnp.zeros((SLAB - NOPE - ROPE, PREP_ROWS), F32)], axis=0)
    wa_ref[:, o_kr:] = kr.T.astype(BF16)
    wg_ref[...] = wt_ref[o_kr + ROPE:, :].T.astype(BF16)


def _proj_kernel(x_ref, nin_ref, wa_ref, wg_ref, qn_ref, wuqt_ref, kvn_ref, wuk_ref, wuvt_ref,
                 poolw_ref, pscale_ref, qcos_ref, qsin_ref, kc_ref, ks1_ref, ks2_ref,
                 qt_out, k_out, vt_out, sga_out, ypool_out, gates_out, carry_ref):
    t = PROJ_TILE
    si = pl.program_id(1)

    @pl.when(si == 0)
    def _():
        carry_ref[...] = jnp.zeros_like(carry_ref)

    x = x_ref[0]
    hn = _rms(x, nin_ref[...]).astype(BF16)

    za = jnp.dot(hn, wa_ref[...], preferred_element_type=F32)
    zq = za[:, :Q_RANK]
    zkv = za[:, Q_RANK:Q_RANK + KV_RANK]
    zkr = za[:, Q_RANK + KV_RANK:]
    cq = _rms(zq, qn_ref[...]).astype(BF16)
    ckv = _rms(zkv, kvn_ref[...]).astype(BF16)

    qt = lax.dot_general(wuqt_ref[...], cq, NT_DIMS, preferred_element_type=F32)
    scale = QK_SCALE
    cos, sin = qcos_ref[...], qsin_ref[...]
    half = ROPE // 2
    tq = ATTN_TQ
    for h in range(HEADS):
        r0 = h * SLAB
        x1 = qt[r0 + NOPE:r0 + NOPE + half]
        x2 = qt[r0 + NOPE + half:r0 + NOPE + ROPE]
        slab = jnp.concatenate([qt[r0:r0 + NOPE] * scale, x1 * cos - x2 * sin, x1 * sin + x2 * cos,
                                jnp.zeros((SLAB - NOPE - ROPE, t), F32)], axis=0).astype(BF16)
        pr, hs = h // HEADS_PER_STEP, h % HEADS_PER_STEP
        for c in range(t // tq):
            for cs in range(HEADS_PER_STEP):
                blk = slab[:, c * tq:(c + 1) * tq] if cs == hs else jnp.zeros((SLAB, tq), BF16)
                qt_out[0, pr, c, hs * SLAB:(hs + 1) * SLAB, cs * tq:(cs + 1) * tq] = blk

    kf = jnp.dot(ckv, wuk_ref[...], preferred_element_type=F32)
    kr = (zkr * kc_ref[...] + pltpu.roll(zkr, LANES - half, axis=1) * ks1_ref[...]
          + pltpu.roll(zkr, half, axis=1) * ks2_ref[...])
    for h in range(HEADS):
        sl = slice(h * SLAB, (h + 1) * SLAB)
        k_out[0, :, sl] = (kf[:, sl] + kr).astype(BF16)

    vt = lax.dot_general(wuvt_ref[...], ckv, NT_DIMS, preferred_element_type=F32)
    for c in range(t // ATTN_TK):
        cols = slice(c * ATTN_TK, (c + 1) * ATTN_TK)
        for h in range(HEADS):
            pr, r0 = h // HEADS_PER_STEP, (h % HEADS_PER_STEP) * VH_ROWS
            vt_out[0, pr, c, r0:r0 + VDIM, :] = vt[h * VDIM:(h + 1) * VDIM, cols].astype(BF16)
            vt_out[0, pr, c, r0 + VDIM:r0 + VH_ROWS, :] = jnp.ones((ONES_ROWS, ATTN_TK), BF16)

    zg = jnp.dot(hn, wg_ref[:, :3 * POOL_WIDTH], preferred_element_type=F32)
    g_attn = zg[:, :MLA_WIDTH]
    u = zg[:, MLA_WIDTH:MLA_WIDTH + POOL_WIDTH]
    g_pool = zg[:, MLA_WIDTH + POOL_WIDTH:]
    sga_out[0] = (g_attn * _sigmoid(g_attn)).astype(BF16)
    for c in range(2 * D_MODEL // GATE_COLS):
        lo = 3 * POOL_WIDTH + c * GATE_COLS
        gm = jnp.dot(hn, wg_ref[:, lo:lo + GATE_COLS], preferred_element_type=F32)
        gates_out[0, :, c * GATE_COLS:(c + 1) * GATE_COLS] = _sigmoid(gm).astype(BF16)

    ue = jnp.concatenate([carry_ref[...], u], axis=0)
    carry_ref[...] = u[t - HALO:, :]
    pos = si * t + lax.broadcasted_iota(jnp.int32, (t, 1), 0) + 1
    outs = []
    for gi, w in enumerate(POOL_WINDOWS):
        sl = slice(gi * POOL_GROUP, (gi + 1) * POOL_GROUP)
        acc = ue[:, sl]
        step = 1
        while step < w:
            acc = acc + pltpu.roll(acc, step, axis=0)
            step *= 2
        inv = 1.0 / jnp.minimum(pos, w).astype(F32)
        d = acc[HALO:, :] * inv - u[:, sl]
        outs.append(jnp.dot(d.astype(BF16), poolw_ref[gi], preferred_element_type=F32))
    y = jnp.concatenate(outs, axis=-1) * pscale_ref[...]
    ypool_out[0] = (y * (g_pool * _sigmoid(g_pool))).astype(BF16)


def _attn_kernel(qbd_ref, k_ref, vt_ref, sga_ref, o_ref, sa_ref, sb_ref, mxa_ref, mxb_ref,
                 m_ref, l_ref, acc_ref):
    tq, tk = ATTN_TQ, ATTN_TK
    nq = HEADS_PER_STEP * tq
    n_qt = qbd_ref.shape[2]
    kblocks = tk // CHUNK
    qry_chunk = lax.shift_right_logical(lax.broadcasted_iota(jnp.int32, (1, 1, nq), 2) & (tq - 1),
                                        CHUNK.bit_length() - 1)
    key_block = lax.broadcasted_iota(jnp.int32, (kblocks, 1, 1), 0)

    def score_tile(s_ref, mx_ref, qi, j, mask_for):
        if mask_for is not None:
            allowed = key_block + (j * kblocks - mask_for * (tq // CHUNK)) <= qry_chunk
            bias = jnp.where(allowed, 0.0, -jnp.inf).astype(F32)
        for g in range(PAIRS_PER_STEP):
            kt = k_ref[0, pl.ds(pl.multiple_of(j * tk, tk), tk),
                       g * HEADS_PER_STEP * SLAB:(g + 1) * HEADS_PER_STEP * SLAB]
            s = jnp.dot(kt, qbd_ref[0, g, qi], preferred_element_type=F32)
            if mask_for is not None:
                s = (s.reshape(kblocks, CHUNK, nq) + bias).reshape(tk, nq)
            s_ref[g] = s
            mx_ref[g] = jnp.max(s, axis=0, keepdims=True)

    def process(s_ref, mx_ref, j):
        for g in range(PAIRS_PER_STEP):
            m_old = m_ref[g]
            m_new = jnp.maximum(m_old, mx_ref[g])
            p = jnp.exp2(s_ref[g] - m_new).astype(BF16)
            alpha = jnp.exp2(m_old - m_new)
            m_ref[g] = m_new
            for h in range(HEADS_PER_STEP):
                cols = slice(h * tq, (h + 1) * tq)
                pv = jnp.dot(vt_ref[0, g, j, h * VH_ROWS:(h + 1) * VH_ROWS, :], p[:, cols],
                             preferred_element_type=F32)
                l_ref[g, :, cols] = alpha[:, cols] * l_ref[g, :, cols] + pv[VDIM:VDIM + 1, :]
                acc_ref[g, h] = alpha[:, cols] * acc_ref[g, h] + pv[:VDIM, :]

    score_tile(sa_ref, mxa_ref, 0, 0, 0)

    def query_tile(i, carry):
        last = ((i + 1) * tq + tk - 1) // tk - 1
        i_next = jnp.minimum(i + 1, n_qt - 1)
        m_ref[...] = jnp.full_like(m_ref, -jnp.inf)
        l_ref[...] = jnp.zeros_like(l_ref)
        acc_ref[...] = jnp.zeros_like(acc_ref)

        def pair(jj, c):
            j = 2 * jj
            score_tile(sb_ref, mxb_ref, i, j + 1, None)
            process(sa_ref, mxa_ref, j)
            score_tile(sa_ref, mxa_ref, i, j + 2, i)
            process(sb_ref, mxb_ref, j + 1)
            return c

        lax.fori_loop(0, last // 2, pair, 0)

        @pl.when(last % 2 == 1)
        def _():
            score_tile(sb_ref, mxb_ref, i, last, i)
            process(sa_ref, mxa_ref, last - 1)
            score_tile(sa_ref, mxa_ref, i_next, 0, i_next)
            process(sb_ref, mxb_ref, last)

        @pl.when(last % 2 == 0)
        def _():
            process(sa_ref, mxa_ref, last)
            score_tile(sa_ref, mxa_ref, i_next, 0, i_next)

        rows = pl.ds(pl.multiple_of(i * tq, tq), tq)
        for g in range(PAIRS_PER_STEP):
            inv_l = 1.0 / l_ref[g]
            ot = jnp.concatenate([acc_ref[g, h] * inv_l[:, h * tq:(h + 1) * tq]
                                  for h in range(HEADS_PER_STEP)], axis=0)
            cols = slice(g * LANES, (g + 1) * LANES)
            o_ref[0, rows, cols] = (ot.T * sga_ref[0, rows, cols].astype(F32)).astype(BF16)
        return carry

    lax.fori_loop(0, n_qt, query_tile, 0)


def _merge_kernel(x_ref, ya_ref, ypool_ref, gates_ref, wa_ref, wp_ref, wo_ref, nf_ref, out_ref):
    a = jnp.dot(ya_ref[0], wa_ref[...], preferred_element_type=F32)
    p = jnp.dot(ypool_ref[0], wp_ref[...], preferred_element_type=F32)
    ga = gates_ref[0, :, :D_MODEL].astype(F32)
    gp = gates_ref[0, :, D_MODEL:].astype(F32)
    merged = (ga * a + gp * p).astype(BF16)
    h = x_ref[0] + jnp.dot(merged, wo_ref[...], preferred_element_type=F32)
    out_ref[0] = _rms(h, nf_ref[...])


def _rope_angles(seq):
    half = ROPE // 2
    inv_freq = ROPE_THETA ** (-jnp.arange(half, dtype=F32) / half)
    ang = jnp.arange(seq, dtype=F32)[:, None] * inv_freq[None, :]
    return jnp.cos(ang), jnp.sin(ang)


def _key_rope_tables(cos, sin):
    seq, half = cos.shape
    zeros_n = jnp.zeros((seq, NOPE), F32)
    zeros_h = jnp.zeros((seq, half), F32)
    zeros_p = jnp.zeros((seq, SLAB - NOPE - ROPE), F32)
    c = jnp.concatenate([zeros_n, cos, cos, zeros_p], axis=1)
    s1 = jnp.concatenate([zeros_n, -sin, zeros_h, zeros_p], axis=1)
    s2 = jnp.concatenate([zeros_n, zeros_h, sin, zeros_p], axis=1)
    return c, s1, s2


def _const_spec(shape):
    return pl.BlockSpec(shape, lambda *_: (0,) * len(shape), pipeline_mode=pl.Buffered(1))


def kernel(x, norm_in, w_in, q_norm, w_uq, kv_norm, w_ukv, pool_w, pool_scale,
           w_branch_attn, w_branch_pool, w_out, norm_final):
    b, s, d = x.shape
    tq, tk, t1, t3 = ATTN_TQ, ATTN_TK, PROJ_TILE, MERGE_TILE
    assert d == D_MODEL and s % t1 == 0 and t1 % tk == 0 and t1 % tq == 0 and s % t3 == 0
    hp = HEADS // HEADS_PER_STEP
    nq = HEADS_PER_STEP * tq
    g2 = PAIRS_PER_STEP

    n_gate = w_in.shape[1] - (Q_RANK + KV_RANK + ROPE)
    w_a, w_g = pl.pallas_call(
        _weight_prep_kernel,
        grid=(d // PREP_ROWS,),
        in_specs=[pl.BlockSpec((w_in.shape[1], PREP_ROWS), lambda i: (0, i))],
        out_specs=[pl.BlockSpec((PREP_ROWS, LATENT_COLS), lambda i: (i, 0)),
                   pl.BlockSpec((PREP_ROWS, n_gate), lambda i: (i, 0))],
        out_shape=[jax.ShapeDtypeStruct((d, LATENT_COLS), BF16),
                   jax.ShapeDtypeStruct((d, n_gate), BF16)],
        name="weight_prep",
    )(w_in.T)
    w_uq_p = jnp.pad(w_uq, ((0, 0), (0, 0), (0, SLAB - NOPE - ROPE)))
    w_uqt = w_uq_p.reshape(Q_RANK, HEADS * SLAB).T.astype(BF16)
    w_uk_p = jnp.pad(w_ukv[:, :, :NOPE], ((0, 0), (0, 0), (0, SLAB - NOPE)))
    w_uk = w_uk_p.reshape(KV_RANK, HEADS * SLAB).astype(BF16)
    w_uvt = w_ukv[:, :, NOPE:].reshape(KV_RANK, MLA_WIDTH).T.astype(BF16)
    cos, sin = _rope_angles(s)
    qcos, qsin = (cos * QK_SCALE).T, (sin * QK_SCALE).T
    kc, ks1, ks2 = _key_rope_tables(cos, sin)

    row = lambda v: v.reshape(1, -1).astype(F32)
    tok = lambda width: pl.BlockSpec((1, t1, width), lambda bi, si: (bi, si, 0))
    tab = pl.BlockSpec((t1, SLAB), lambda bi, si: (si, 0))
    tabt = pl.BlockSpec((ROPE // 2, t1), lambda bi, si: (0, si))
    qt, k, vt, sga, ypool, gates = pl.pallas_call(
        _proj_kernel,
        grid=(b, s // t1),
        in_specs=[tok(d), _const_spec((1, d)), _const_spec(w_a.shape), _const_spec(w_g.shape),
                  _const_spec((1, Q_RANK)), _const_spec(w_uqt.shape), _const_spec((1, KV_RANK)),
                  _const_spec(w_uk.shape), _const_spec(w_uvt.shape),
                  _const_spec(pool_w.shape), _const_spec((1, POOL_WIDTH)),
                  tabt, tabt, tab, tab, tab],
        out_specs=[pl.BlockSpec((1, hp, t1 // tq, HEADS_PER_STEP * SLAB, nq),
                                lambda bi, si: (bi, 0, si, 0, 0)),
                   tok(HEADS * SLAB),
                   pl.BlockSpec((1, hp, t1 // tk, VT_ROWS, tk), lambda bi, si: (bi, 0, si, 0, 0)),
                   tok(MLA_WIDTH), tok(POOL_WIDTH), tok(2 * d)],
        out_shape=[jax.ShapeDtypeStruct((b, hp, s // tq, HEADS_PER_STEP * SLAB, nq), BF16),
                   jax.ShapeDtypeStruct((b, s, HEADS * SLAB), BF16),
                   jax.ShapeDtypeStruct((b, hp, s // tk, VT_ROWS, tk), BF16),
                   jax.ShapeDtypeStruct((b, s, MLA_WIDTH), BF16),
                   jax.ShapeDtypeStruct((b, s, POOL_WIDTH), BF16),
                   jax.ShapeDtypeStruct((b, s, 2 * d), BF16)],
        scratch_shapes=[pltpu.VMEM((HALO, POOL_WIDTH), F32)],
        compiler_params=pltpu.CompilerParams(
            dimension_semantics=("arbitrary", "arbitrary"), vmem_limit_bytes=VMEM_LIMIT),
        name="proj",
    )(x, row(norm_in), w_a, w_g, row(q_norm), w_uqt, row(kv_norm), w_uk, w_uvt,
      pool_w.astype(BF16), row(pool_scale), qcos, qsin, kc, ks1, ks2)

    o = pl.pallas_call(
        _attn_kernel,
        grid=(b, hp // g2),
        in_specs=[pl.BlockSpec((1, g2, s // tq, HEADS_PER_STEP * SLAB, nq), lambda bi, pi: (bi, pi, 0, 0, 0)),
                  pl.BlockSpec((1, s, g2 * HEADS_PER_STEP * SLAB), lambda bi, pi: (bi, 0, pi)),
                  pl.BlockSpec((1, g2, s // tk, VT_ROWS, tk), lambda bi, pi: (bi, pi, 0, 0, 0)),
                  pl.BlockSpec((1, s, g2 * LANES), lambda bi, pi: (bi, 0, pi))],
        out_specs=pl.BlockSpec((1, s, g2 * LANES), lambda bi, pi: (bi, 0, pi)),
        out_shape=jax.ShapeDtypeStruct((b, s, MLA_WIDTH), BF16),
        scratch_shapes=[pltpu.VMEM((g2, tk, nq), F32), pltpu.VMEM((g2, tk, nq), F32),
                        pltpu.VMEM((g2, 1, nq), F32), pltpu.VMEM((g2, 1, nq), F32),
                        pltpu.VMEM((g2, 1, nq), F32), pltpu.VMEM((g2, 1, nq), F32),
                        pltpu.VMEM((g2, HEADS_PER_STEP, VDIM, tq), F32)],
        compiler_params=pltpu.CompilerParams(
            dimension_semantics=("arbitrary", "arbitrary"), vmem_limit_bytes=VMEM_LIMIT),
        name="attn",
    )(qt, k, vt, sga)

    tok3 = lambda width: pl.BlockSpec((1, t3, width), lambda bi, si: (bi, si, 0))
    out = pl.pallas_call(
        _merge_kernel,
        grid=(b, s // t3),
        in_specs=[tok3(d), tok3(MLA_WIDTH), tok3(POOL_WIDTH), tok3(2 * d),
                  _const_spec((MLA_WIDTH, d)), _const_spec((POOL_WIDTH, d)), _const_spec((d, d)),
                  _const_spec((1, d))],
        out_specs=tok3(d),
        out_shape=jax.ShapeDtypeStruct((b, s, d), x.dtype),
        compiler_params=pltpu.CompilerParams(
            dimension_semantics=("arbitrary", "arbitrary"), vmem_limit_bytes=VMEM_LIMIT),
        name="merge",
    )(x, o, ypool, gates, w_branch_attn.astype(BF16), w_branch_pool.astype(BF16),
      w_out.astype(BF16), row(norm_final))
    return out
```

```python
import jax
import jax.numpy as jnp
from jax import lax
from jax.experimental import pallas as pl
from jax.experimental.pallas import tpu as pltpu

D_MODEL = 1024
CHUNK = 64
HEADS = 8
NOPE = 64
ROPE = 32
VDIM = 64
Q_RANK = 384
KV_RANK = 256
MLA_WIDTH = HEADS * VDIM
ROPE_THETA = 10000.0
POOL_WINDOWS = (2, 4, 8, 16)
POOL_WIDTH = D_MODEL // 2
POOL_GROUP = POOL_WIDTH // len(POOL_WINDOWS)
EPS = 1e-6
QK_SCALE = (NOPE + ROPE) ** -0.5 * 1.4426950408889634

LANES = 128
HALO = 16
SLAB = LANES
LATENT_COLS = Q_RANK + KV_RANK + SLAB
PREP_ROWS = 128
HEADS_PER_STEP = 2
PAIRS_PER_STEP = 2
ONES_ROWS = 16
VH_ROWS = VDIM + ONES_ROWS
VT_ROWS = HEADS_PER_STEP * VH_ROWS
PROJ_TILE = 512
GATE_COLS = 512
ATTN_TQ = 512
ATTN_TK = 512
MERGE_TILE = 1024
VMEM_LIMIT = 56 * 1024 * 1024

BF16 = jnp.bfloat16
F32 = jnp.float32
NT_DIMS = (((1,), (1,)), ((), ()))


def _rms(x, g):
    return x * lax.rsqrt(jnp.mean(x * x, axis=-1, keepdims=True) + EPS) * g


def _sigmoid(x):
    return 1.0 / (1.0 + jnp.exp(-x))


def _weight_prep_kernel(wt_ref, wa_ref, wg_ref):
    o_kr = Q_RANK + KV_RANK
    wa_ref[:, :o_kr] = wt_ref[:o_kr, :].T.astype(BF16)
    kr = jnp.concatenate([jnp.zeros((NOPE, PREP_ROWS), F32), wt_ref[o_kr:o_kr + ROPE, :],
                          jnp.zeros((SLAB - NOPE - ROPE, PREP_ROWS), F32)], axis=0)
    wa_ref[:, o_kr:] = kr.T.astype(BF16)
    wg_ref[...] = wt_ref[o_kr + ROPE:, :].T.astype(BF16)


def _proj_kernel(x_ref, nin_ref, wa_ref, wg_ref, qn_ref, wuqt_ref, kvn_ref, wuk_ref, wuvt_ref,
                 poolw_ref, pscale_ref, qcos_ref, qsin_ref, kc_ref, ks1_ref, ks2_ref,
                 qt_out, k_out, vt_out, sga_out, ypool_out, gates_out, carry_ref):
    t = PROJ_TILE
    si = pl.program_id(1)

    @pl.when(si == 0)
    def _():
        carry_ref[...] = jnp.zeros_like(carry_ref)

    x = x_ref[0]
    hn = _rms(x, nin_ref[...]).astype(BF16)

    za = jnp.dot(hn, wa_ref[...], preferred_element_type=F32)
    zq = za[:, :Q_RANK]
    zkv = za[:, Q_RANK:Q_RANK + KV_RANK]
    zkr = za[:, Q_RANK + KV_RANK:]
    cq = _rms(zq, qn_ref[...]).astype(BF16)
    ckv = _rms(zkv, kvn_ref[...]).astype(BF16)

    qt = lax.dot_general(wuqt_ref[...], cq, NT_DIMS, preferred_element_type=F32)
    scale = QK_SCALE
    cos, sin = qcos_ref[...], qsin_ref[...]
    half = ROPE // 2
    tq = ATTN_TQ
    for h in range(HEADS):
        r0 = h * SLAB
        x1 = qt[r0 + NOPE:r0 + NOPE + half]
        x2 = qt[r0 + NOPE + half:r0 + NOPE + ROPE]
        slab = jnp.concatenate([qt[r0:r0 + NOPE] * scale, x1 * cos - x2 * sin, x1 * sin + x2 * cos,
                                jnp.zeros((SLAB - NOPE - ROPE, t), F32)], axis=0).astype(BF16)
        pr, hs = h // HEADS_PER_STEP, h % HEADS_PER_STEP
        for c in range(t // tq):
            for cs in range(HEADS_PER_STEP):
                blk = slab[:, c * tq:(c + 1) * tq] if cs == hs else jnp.zeros((SLAB, tq), BF16)
                qt_out[0, pr, c, hs * SLAB:(hs + 1) * SLAB, cs * tq:(cs + 1) * tq] = blk

    kf = jnp.dot(ckv, wuk_ref[...], preferred_element_type=F32)
    kr = (zkr * kc_ref[...] + pltpu.roll(zkr, LANES - half, axis=1) * ks1_ref[...]
          + pltpu.roll(zkr, half, axis=1) * ks2_ref[...])
    for h in range(HEADS):
        sl = slice(h * SLAB, (h + 1) * SLAB)
        k_out[0, :, sl] = (kf[:, sl] + kr).astype(BF16)

    vt = lax.dot_general(wuvt_ref[...], ckv, NT_DIMS, preferred_element_type=F32)
    for c in range(t // ATTN_TK):
        cols = slice(c * ATTN_TK, (c + 1) * ATTN_TK)
        for h in range(HEADS):
            pr, r0 = h // HEADS_PER_STEP, (h % HEADS_PER_STEP) * VH_ROWS
            vt_out[0, pr, c, r0:r0 + VDIM, :] = vt[h * VDIM:(h + 1) * VDIM, cols].astype(BF16)
            vt_out[0, pr, c, r0 + VDIM:r0 + VH_ROWS, :] = jnp.ones((ONES_ROWS, ATTN_TK), BF16)

    zg = jnp.dot(hn, wg_ref[:, :3 * POOL_WIDTH], preferred_element_type=F32)
    g_attn = zg[:, :MLA_WIDTH]
    u = zg[:, MLA_WIDTH:MLA_WIDTH + POOL_WIDTH]
    g_pool = zg[:, MLA_WIDTH + POOL_WIDTH:]
    sga_out[0] = (g_attn * _sigmoid(g_attn)).astype(BF16)
    for c in range(2 * D_MODEL // GATE_COLS):
        lo = 3 * POOL_WIDTH + c * GATE_COLS
        gm = jnp.dot(hn, wg_ref[:, lo:lo + GATE_COLS], preferred_element_type=F32)
        gates_out[0, :, c * GATE_COLS:(c + 1) * GATE_COLS] = _sigmoid(gm).astype(BF16)

    ue = jnp.concatenate([carry_ref[...], u], axis=0)
    carry_ref[...] = u[t - HALO:, :]
    pos = si * t + lax.broadcasted_iota(jnp.int32, (t, 1), 0) + 1
    outs = []
    for gi, w in enumerate(POOL_WINDOWS):
        sl = slice(gi * POOL_GROUP, (gi + 1) * POOL_GROUP)
        acc = ue[:, sl]
        step = 1
        while step < w:
            acc = acc + pltpu.roll(acc, step, axis=0)
            step *= 2
        inv = 1.0 / jnp.minimum(pos, w).astype(F32)
        d = acc[HALO:, :] * inv - u[:, sl]
        outs.append(jnp.dot(d.astype(BF16), poolw_ref[gi], preferred_element_type=F32))
    y = jnp.concatenate(outs, axis=-1) * pscale_ref[...]
    ypool_out[0] = (y * (g_pool * _sigmoid(g_pool))).astype(BF16)


def _attn_kernel(qbd_ref, k_ref, vt_ref, sga_ref, o_ref, sa_ref, sb_ref, mxa_ref, mxb_ref,
                 m_ref, l_ref, acc_ref):
    tq, tk = ATTN_TQ, ATTN_TK
    nq = HEADS_PER_STEP * tq
    n_qt = qbd_ref.shape[2]
    kblocks = tk // CHUNK
    qry_chunk = lax.shift_right_logical(lax.broadcasted_iota(jnp.int32, (1, 1, nq), 2) & (tq - 1),
                                        CHUNK.bit_length() - 1)
    key_block = lax.broadcasted_iota(jnp.int32, (kblocks, 1, 1), 0)

    def score_tile(s_ref, mx_ref, qi, j, mask_for):
        if mask_for is not None:
            allowed = key_block + (j * kblocks - mask_for * (tq // CHUNK)) <= qry_chunk
            bias = jnp.where(allowed, 0.0, -jnp.inf).astype(F32)
        for g in range(PAIRS_PER_STEP):
            kt = k_ref[0, pl.ds(pl.multiple_of(j * tk, tk), tk),
                       g * HEADS_PER_STEP * SLAB:(g + 1) * HEADS_PER_STEP * SLAB]
            s = jnp.dot(kt, qbd_ref[0, g, qi], preferred_element_type=F32)
            if mask_for is not None:
                s = (s.reshape(kblocks, CHUNK, nq) + bias).reshape(tk, nq)
            s_ref[g] = s
            mx_ref[g] = jnp.max(s, axis=0, keepdims=True)

    def process(s_ref, mx_ref, j, diagonal=False):
        split = diagonal and tq == tk
        hk, hq = tk // 2, tq // 2
        for g in range(PAIRS_PER_STEP):
            m_old = m_ref[g]
            m_new = jnp.maximum(m_old, mx_ref[g])
            p = jnp.exp2((s_ref[g, :hk, :] if split else s_ref[g]) - m_new).astype(BF16)
            alpha = jnp.exp2(m_old - m_new)
            m_ref[g] = m_new
            for h in range(HEADS_PER_STEP):
                cols = slice(h * tq, (h + 1) * tq)
                vrows = slice(h * VH_ROWS, (h + 1) * VH_ROWS)
                if split:
                    late = slice(h * tq + hq, (h + 1) * tq)
                    p_late = jnp.exp2(s_ref[g, hk:, late] - m_new[:, late]).astype(BF16)
                    pv = jnp.dot(vt_ref[0, g, j, vrows, :hk], p[:, cols], preferred_element_type=F32)
                    pv_late = jnp.dot(vt_ref[0, g, j, vrows, hk:], p_late, preferred_element_type=F32)
                    pv = jnp.concatenate([pv[:, :hq], pv[:, hq:] + pv_late], axis=1)
                else:
                    pv = jnp.dot(vt_ref[0, g, j, vrows, :], p[:, cols],
                                 preferred_element_type=F32)
                l_ref[g, :, cols] = alpha[:, cols] * l_ref[g, :, cols] + pv[VDIM:VDIM + 1, :]
                acc_ref[g, h] = alpha[:, cols] * acc_ref[g, h] + pv[:VDIM, :]

    score_tile(sa_ref, mxa_ref, 0, 0, 0)

    def query_tile(i, carry):
        last = ((i + 1) * tq + tk - 1) // tk - 1
        i_next = jnp.minimum(i + 1, n_qt - 1)
        m_ref[...] = jnp.full_like(m_ref, -jnp.inf)
        l_ref[...] = jnp.zeros_like(l_ref)
        acc_ref[...] = jnp.zeros_like(acc_ref)

        def pair(jj, c):
            j = 2 * jj
            score_tile(sb_ref, mxb_ref, i, j + 1, None)
            process(sa_ref, mxa_ref, j)
            score_tile(sa_ref, mxa_ref, i, j + 2, i)
            process(sb_ref, mxb_ref, j + 1)
            return c

        lax.fori_loop(0, last // 2, pair, 0)

        @pl.when(last % 2 == 1)
        def _():
            score_tile(sb_ref, mxb_ref, i, last, i)
            process(sa_ref, mxa_ref, last - 1)
            score_tile(sa_ref, mxa_ref, i_next, 0, i_next)
            process(sb_ref, mxb_ref, last, diagonal=True)

        @pl.when(last % 2 == 0)
        def _():
            process(sa_ref, mxa_ref, last, diagonal=True)
            score_tile(sa_ref, mxa_ref, i_next, 0, i_next)

        rows = pl.ds(pl.multiple_of(i * tq, tq), tq)
        for g in range(PAIRS_PER_STEP):
            inv_l = 1.0 / l_ref[g]
            ot = jnp.concatenate([acc_ref[g, h] * inv_l[:, h * tq:(h + 1) * tq]
                                  for h in range(HEADS_PER_STEP)], axis=0)
            cols = slice(g * LANES, (g + 1) * LANES)
            o_ref[0, rows, cols] = (ot.T * sga_ref[0, rows, cols].astype(F32)).astype(BF16)
        return carry

    lax.fori_loop(0, n_qt, query_tile, 0)


def _merge_kernel(x_ref, ya_ref, ypool_ref, gates_ref, wa_ref, wp_ref, wo_ref, nf_ref, out_ref):
    a = jnp.dot(ya_ref[0], wa_ref[...], preferred_element_type=F32)
    p = jnp.dot(ypool_ref[0], wp_ref[...], preferred_element_type=F32)
    ga = gates_ref[0, :, :D_MODEL].astype(F32)
    gp = gates_ref[0, :, D_MODEL:].astype(F32)
    merged = (ga * a + gp * p).astype(BF16)
    h = x_ref[0] + jnp.dot(merged, wo_ref[...], preferred_element_type=F32)
    out_ref[0] = _rms(h, nf_ref[...])


def _rope_angles(seq):
    half = ROPE // 2
    inv_freq = ROPE_THETA ** (-jnp.arange(half, dtype=F32) / half)
    ang = jnp.arange(seq, dtype=F32)[:, None] * inv_freq[None, :]
    return jnp.cos(ang), jnp.sin(ang)


def _key_rope_tables(cos, sin):
    seq, half = cos.shape
    zeros_n = jnp.zeros((seq, NOPE), F32)
    zeros_h = jnp.zeros((seq, half), F32)
    zeros_p = jnp.zeros((seq, SLAB - NOPE - ROPE), F32)
    c = jnp.concatenate([zeros_n, cos, cos, zeros_p], axis=1)
    s1 = jnp.concatenate([zeros_n, -sin, zeros_h, zeros_p], axis=1)
    s2 = jnp.concatenate([zeros_n, zeros_h, sin, zeros_p], axis=1)
    return c, s1, s2


def _const_spec(shape):
    return pl.BlockSpec(shape, lambda *_: (0,) * len(shape), pipeline_mode=pl.Buffered(1))


def kernel(x, norm_in, w_in, q_norm, w_uq, kv_norm, w_ukv, pool_w, pool_scale,
           w_branch_attn, w_branch_pool, w_out, norm_final):
    b, s, d = x.shape
    tq, tk, t1, t3 = ATTN_TQ, ATTN_TK, PROJ_TILE, MERGE_TILE
    assert d == D_MODEL and s % t1 == 0 and t1 % tk == 0 and t1 % tq == 0 and s % t3 == 0
    hp = HEADS // HEADS_PER_STEP
    nq = HEADS_PER_STEP * tq
    g2 = PAIRS_PER_STEP

    n_gate = w_in.shape[1] - (Q_RANK + KV_RANK + ROPE)
    w_a, w_g = pl.pallas_call(
        _weight_prep_kernel,
        grid=(d // PREP_ROWS,),
        in_specs=[pl.BlockSpec((w_in.shape[1], PREP_ROWS), lambda i: (0, i))],
        out_specs=[pl.BlockSpec((PREP_ROWS, LATENT_COLS), lambda i: (i, 0)),
                   pl.BlockSpec((PREP_ROWS, n_gate), lambda i: (i, 0))],
        out_shape=[jax.ShapeDtypeStruct((d, LATENT_COLS), BF16),
                   jax.ShapeDtypeStruct((d, n_gate), BF16)],
        name="weight_prep",
    )(w_in.T)
    w_uq_p = jnp.pad(w_uq, ((0, 0), (0, 0), (0, SLAB - NOPE - ROPE)))
    w_uqt = w_uq_p.reshape(Q_RANK, HEADS * SLAB).T.astype(BF16)
    w_uk_p = jnp.pad(w_ukv[:, :, :NOPE], ((0, 0), (0, 0), (0, SLAB - NOPE)))
    w_uk = w_uk_p.reshape(KV_RANK, HEADS * SLAB).astype(BF16)
    w_uvt = w_ukv[:, :, NOPE:].reshape(KV_RANK, MLA_WIDTH).T.astype(BF16)
    cos, sin = _rope_angles(s)
    qcos, qsin = (cos * QK_SCALE).T, (sin * QK_SCALE).T
    kc, ks1, ks2 = _key_rope_tables(cos, sin)

    row = lambda v: v.reshape(1, -1).astype(F32)
    tok = lambda width: pl.BlockSpec((1, t1, width), lambda bi, si: (bi, si, 0))
    tab = pl.BlockSpec((t1, SLAB), lambda bi, si: (si, 0))
    tabt = pl.BlockSpec((ROPE // 2, t1), lambda bi, si: (0, si))
    qt, k, vt, sga, ypool, gates = pl.pallas_call(
        _proj_kernel,
        grid=(b, s // t1),
        in_specs=[tok(d), _const_spec((1, d)), _const_spec(w_a.shape), _const_spec(w_g.shape),
                  _const_spec((1, Q_RANK)), _const_spec(w_uqt.shape), _const_spec((1, KV_RANK)),
                  _const_spec(w_uk.shape), _const_spec(w_uvt.shape),
                  _const_spec(pool_w.shape), _const_spec((1, POOL_WIDTH)),
                  tabt, tabt, tab, tab, tab],
        out_specs=[pl.BlockSpec((1, hp, t1 // tq, HEADS_PER_STEP * SLAB, nq),
                                lambda bi, si: (bi, 0, si, 0, 0)),
                   tok(HEADS * SLAB),
                   pl.BlockSpec((1, hp, t1 // tk, VT_ROWS, tk), lambda bi, si: (bi, 0, si, 0, 0)),
                   tok(MLA_WIDTH), tok(POOL_WIDTH), tok(2 * d)],
        out_shape=[jax.ShapeDtypeStruct((b, hp, s // tq, HEADS_PER_STEP * SLAB, nq), BF16),
                   jax.ShapeDtypeStruct((b, s, HEADS * SLAB), BF16),
                   jax.ShapeDtypeStruct((b, hp, s // tk, VT_ROWS, tk), BF16),
                   jax.ShapeDtypeStruct((b, s, MLA_WIDTH), BF16),
                   jax.ShapeDtypeStruct((b, s, POOL_WIDTH), BF16),
                   jax.ShapeDtypeStruct((b, s, 2 * d), BF16)],
        scratch_shapes=[pltpu.VMEM((HALO, POOL_WIDTH), F32)],
        compiler_params=pltpu.CompilerParams(
            dimension_semantics=("arbitrary", "arbitrary"), vmem_limit_bytes=VMEM_LIMIT),
        name="proj",
    )(x, row(norm_in), w_a, w_g, row(q_norm), w_uqt, row(kv_norm), w_uk, w_uvt,
      pool_w.astype(BF16), row(pool_scale), qcos, qsin, kc, ks1, ks2)

    o = pl.pallas_call(
        _attn_kernel,
        grid=(b, hp // g2),
        in_specs=[pl.BlockSpec((1, g2, s // tq, HEADS_PER_STEP * SLAB, nq), lambda bi, pi: (bi, pi, 0, 0, 0)),
                  pl.BlockSpec((1, s, g2 * HEADS_PER_STEP * SLAB), lambda bi, pi: (bi, 0, pi)),
                  pl.BlockSpec((1, g2, s // tk, VT_ROWS, tk), lambda bi, pi: (bi, pi, 0, 0, 0)),
                  pl.BlockSpec((1, s, g2 * LANES), lambda bi, pi: (bi, 0, pi))],
        out_specs=pl.BlockSpec((1, s, g2 * LANES), lambda bi, pi: (bi, 0, pi)),
        out_shape=jax.ShapeDtypeStruct((b, s, MLA_WIDTH), BF16),
        scratch_shapes=[pltpu.VMEM((g2, tk, nq), F32), pltpu.VMEM((g2, tk, nq), F32),
                        pltpu.VMEM((g2, 1, nq), F32), pltpu.VMEM((g2, 1, nq), F32),
                        pltpu.VMEM((g2, 1, nq), F32), pltpu.VMEM((g2, 1, nq), F32),
                        pltpu.VMEM((g2, HEADS_PER_STEP, VDIM, tq), F32)],
        compiler_params=pltpu.CompilerParams(
            dimension_semantics=("arbitrary", "arbitrary"), vmem_limit_bytes=VMEM_LIMIT),
        name="attn",
    )(qt, k, vt, sga)

    tok3 = lambda width: pl.BlockSpec((1, t3, width), lambda bi, si: (bi, si, 0))
    out = pl.pallas_call(
        _merge_kernel,
        grid=(b, s // t3),
        in_specs=[tok3(d), tok3(MLA_WIDTH), tok3(POOL_WIDTH), tok3(2 * d),
                  _const_spec((MLA_WIDTH, d)), _const_spec((POOL_WIDTH, d)), _const_spec((d, d)),
                  _const_spec((1, d))],
        out_specs=tok3(d),
        out_shape=jax.ShapeDtypeStruct((b, s, d), x.dtype),
        compiler_params=pltpu.CompilerParams(
            dimension_semantics=("arbitrary", "arbitrary"), vmem_limit_bytes=VMEM_LIMIT),
        name="merge",
    )(x, o, ypool, gates, w_branch_attn.astype(BF16), w_branch_pool.astype(BF16),
      w_out.astype(BF16), row(norm_final))
    return out
```

```python
import jax
import jax.numpy as jnp
from jax import lax
from jax.experimental import pallas as pl
from jax.experimental.pallas import tpu as pltpu

D_MODEL = 1024
CHUNK = 64
HEADS = 8
NOPE = 64
ROPE = 32
VDIM = 64
Q_RANK = 384
KV_RANK = 256
MLA_WIDTH = HEADS * VDIM
ROPE_THETA = 10000.0
POOL_WINDOWS = (2, 4, 8, 16)
POOL_WIDTH = D_MODEL // 2
POOL_GROUP = POOL_WIDTH // len(POOL_WINDOWS)
EPS = 1e-6
QK_SCALE = (NOPE + ROPE) ** -0.5 * 1.4426950408889634

LANES = 128
HALO = 16
SLAB = LANES
LATENT_COLS = Q_RANK + KV_RANK + SLAB
PREP_ROWS = 128
HEADS_PER_STEP = 2
PAIRS_PER_STEP = 2
ONES_ROWS = 16
VH_ROWS = VDIM + ONES_ROWS
VT_ROWS = HEADS_PER_STEP * VH_ROWS
PROJ_TILE = 512
GATE_COLS = 512
ATTN_TQ = 512
ATTN_TK = 512
MERGE_TILE = 1024
VMEM_LIMIT = 56 * 1024 * 1024

BF16 = jnp.bfloat16
F32 = jnp.float32
NT_DIMS = (((1,), (1,)), ((), ()))


def _rms(x, g):
    return x * lax.rsqrt(jnp.mean(x * x, axis=-1, keepdims=True) + EPS) * g


def _sigmoid(x):
    return 1.0 / (1.0 + jnp.exp(-x))


def _weight_prep_kernel(wt_ref, wa_ref, wg_ref):
    o_kr = Q_RANK + KV_RANK
    wa_ref[:, :o_kr] = wt_ref[:o_kr, :].T.astype(BF16)
    kr = jnp.concatenate([jnp.zeros((NOPE, PREP_ROWS), F32), wt_ref[o_kr:o_kr + ROPE, :],
                          jnp.zeros((SLAB - NOPE - ROPE, PREP_ROWS), F32)], axis=0)
    wa_ref[:, o_kr:] = kr.T.astype(BF16)
    wg_ref[...] = wt_ref[o_kr + ROPE:, :].T.astype(BF16)


def _proj_kernel(x_ref, nin_ref, wa_ref, wg_ref, qn_ref, wuqt_ref, kvn_ref, wuk_ref, wuvt_ref,
                 poolw_ref, pscale_ref, cos_ref, sin_ref,
                 qt_out, k_out, vt_out, sga_out, ypool_out, gates_out, carry_ref):
    t = PROJ_TILE
    si = pl.program_id(1)

    @pl.when(si == 0)
    def _():
        carry_ref[...] = jnp.zeros_like(carry_ref)

    x = x_ref[0]
    hn = _rms(x, nin_ref[...]).astype(BF16)

    za = jnp.dot(hn, wa_ref[...], preferred_element_type=F32)
    zq = za[:, :Q_RANK]
    zkv = za[:, Q_RANK:Q_RANK + KV_RANK]
    zkr = za[:, Q_RANK + KV_RANK:]
    cq = _rms(zq, qn_ref[...]).astype(BF16)
    ckv = _rms(zkv, kvn_ref[...]).astype(BF16)

    qt = lax.dot_general(wuqt_ref[...], cq, NT_DIMS, preferred_element_type=F32)
    scale = QK_SCALE
    cos, sin = cos_ref[...], sin_ref[...]
    qcos, qsin = cos * scale, sin * scale
    half = ROPE // 2
    tq = ATTN_TQ
    for h in range(HEADS):
        r0 = h * SLAB
        x1 = qt[r0 + NOPE:r0 + NOPE + half]
        x2 = qt[r0 + NOPE + half:r0 + NOPE + ROPE]
        slab = jnp.concatenate([qt[r0:r0 + NOPE] * scale, x1 * qcos - x2 * qsin, x1 * qsin + x2 * qcos,
                                jnp.zeros((SLAB - NOPE - ROPE, t), F32)], axis=0).astype(BF16)
        pr, hs = h // HEADS_PER_STEP, h % HEADS_PER_STEP
        for c in range(t // tq):
            for cs in range(HEADS_PER_STEP):
                blk = slab[:, c * tq:(c + 1) * tq] if cs == hs else jnp.zeros((SLAB, tq), BF16)
                qt_out[0, pr, c, hs * SLAB:(hs + 1) * SLAB, cs * tq:(cs + 1) * tq] = blk

    kf = jnp.dot(ckv, wuk_ref[...], preferred_element_type=F32)
    zt = zkr.T
    k1, k2 = zt[NOPE:NOPE + half], zt[NOPE + half:NOPE + ROPE]
    kr = jnp.concatenate([jnp.zeros((NOPE, t), F32), k1 * cos - k2 * sin, k1 * sin + k2 * cos,
                          jnp.zeros((SLAB - NOPE - ROPE, t), F32)], axis=0).T
    for h in range(HEADS):
        sl = slice(h * SLAB, (h + 1) * SLAB)
        k_out[0, :, sl] = (kf[:, sl] + kr).astype(BF16)

    vt = lax.dot_general(wuvt_ref[...], ckv, NT_DIMS, preferred_element_type=F32)
    for c in range(t // ATTN_TK):
        cols = slice(c * ATTN_TK, (c + 1) * ATTN_TK)
        for h in range(HEADS):
            pr, r0 = h // HEADS_PER_STEP, (h % HEADS_PER_STEP) * VH_ROWS
            vt_out[0, pr, c, r0:r0 + VDIM, :] = vt[h * VDIM:(h + 1) * VDIM, cols].astype(BF16)
            vt_out[0, pr, c, r0 + VDIM:r0 + VH_ROWS, :] = jnp.ones((ONES_ROWS, ATTN_TK), BF16)

    zg = jnp.dot(hn, wg_ref[:, :3 * POOL_WIDTH], preferred_element_type=F32)
    g_attn = zg[:, :MLA_WIDTH]
    u = zg[:, MLA_WIDTH:MLA_WIDTH + POOL_WIDTH]
    g_pool = zg[:, MLA_WIDTH + POOL_WIDTH:]
    sga_out[0] = (g_attn * _sigmoid(g_attn)).astype(BF16)
    for c in range(2 * D_MODEL // GATE_COLS):
        lo = 3 * POOL_WIDTH + c * GATE_COLS
        gm = jnp.dot(hn, wg_ref[:, lo:lo + GATE_COLS], preferred_element_type=F32)
        gates_out[0, :, c * GATE_COLS:(c + 1) * GATE_COLS] = _sigmoid(gm).astype(BF16)

    ue = jnp.concatenate([carry_ref[...], u], axis=0)
    carry_ref[...] = u[t - HALO:, :]
    pos = si * t + lax.broadcasted_iota(jnp.int32, (t, 1), 0) + 1
    outs = []
    for gi, w in enumerate(POOL_WINDOWS):
        sl = slice(gi * POOL_GROUP, (gi + 1) * POOL_GROUP)
        acc = ue[:, sl]
        step = 1
        while step < w:
            acc = acc + pltpu.roll(acc, step, axis=0)
            step *= 2
        inv = 1.0 / jnp.minimum(pos, w).astype(F32)
        d = acc[HALO:, :] * inv - u[:, sl]
        outs.append(jnp.dot(d.astype(BF16), poolw_ref[gi], preferred_element_type=F32))
    y = jnp.concatenate(outs, axis=-1) * pscale_ref[...]
    ypool_out[0] = (y * (g_pool * _sigmoid(g_pool))).astype(BF16)


def _attn_kernel(qbd_ref, k_ref, vt_ref, sga_ref, o_ref, sa_ref, sb_ref, mxa_ref, mxb_ref,
                 m_ref, l_ref, acc_ref):
    tq, tk = ATTN_TQ, ATTN_TK
    nq = HEADS_PER_STEP * tq
    n_qt = qbd_ref.shape[2]
    kblocks = tk // CHUNK
    qry_chunk = lax.shift_right_logical(lax.broadcasted_iota(jnp.int32, (1, 1, nq), 2) & (tq - 1),
                                        CHUNK.bit_length() - 1)
    key_block = lax.broadcasted_iota(jnp.int32, (kblocks, 1, 1), 0)

    def score_tile(s_ref, mx_ref, qi, j, mask_for):
        if mask_for is not None:
            allowed = key_block + (j * kblocks - mask_for * (tq // CHUNK)) <= qry_chunk
            bias = jnp.where(allowed, 0.0, -jnp.inf).astype(F32)
        for g in range(PAIRS_PER_STEP):
            kt = k_ref[0, pl.ds(pl.multiple_of(j * tk, tk), tk),
                       g * HEADS_PER_STEP * SLAB:(g + 1) * HEADS_PER_STEP * SLAB]
            s = jnp.dot(kt, qbd_ref[0, g, qi], preferred_element_type=F32)
            if mask_for is not None:
                s = (s.reshape(kblocks, CHUNK, nq) + bias).reshape(tk, nq)
            s_ref[g] = s
            mx_ref[g] = jnp.max(s, axis=0, keepdims=True)

    def process(s_ref, mx_ref, j, diagonal=False):
        split = diagonal and tq == tk
        hk, hq = tk // 2, tq // 2
        for g in range(PAIRS_PER_STEP):
            m_old = m_ref[g]
            m_new = jnp.maximum(m_old, mx_ref[g])
            p = jnp.exp2((s_ref[g, :hk, :] if split else s_ref[g]) - m_new).astype(BF16)
            alpha = jnp.exp2(m_old - m_new)
            m_ref[g] = m_new
            for h in range(HEADS_PER_STEP):
                cols = slice(h * tq, (h + 1) * tq)
                vrows = slice(h * VH_ROWS, (h + 1) * VH_ROWS)
                if split:
                    late = slice(h * tq + hq, (h + 1) * tq)
                    p_late = jnp.exp2(s_ref[g, hk:, late] - m_new[:, late]).astype(BF16)
                    pv = jnp.dot(vt_ref[0, g, j, vrows, :hk], p[:, cols], preferred_element_type=F32)
                    pv_late = jnp.dot(vt_ref[0, g, j, vrows, hk:], p_late, preferred_element_type=F32)
                    pv = jnp.concatenate([pv[:, :hq], pv[:, hq:] + pv_late], axis=1)
                else:
                    pv = jnp.dot(vt_ref[0, g, j, vrows, :], p[:, cols],
                                 preferred_element_type=F32)
                l_ref[g, :, cols] = alpha[:, cols] * l_ref[g, :, cols] + pv[VDIM:VDIM + 1, :]
                acc_ref[g, h] = alpha[:, cols] * acc_ref[g, h] + pv[:VDIM, :]

    score_tile(sa_ref, mxa_ref, 0, 0, 0)

    def query_tile(i, carry):
        last = ((i + 1) * tq + tk - 1) // tk - 1
        i_next = jnp.minimum(i + 1, n_qt - 1)
        m_ref[...] = jnp.full_like(m_ref, -jnp.inf)
        l_ref[...] = jnp.zeros_like(l_ref)
        acc_ref[...] = jnp.zeros_like(acc_ref)

        def pair(jj, c):
            j = 2 * jj
            score_tile(sb_ref, mxb_ref, i, j + 1, None)
            process(sa_ref, mxa_ref, j)
            score_tile(sa_ref, mxa_ref, i, j + 2, i)
            process(sb_ref, mxb_ref, j + 1)
            return c

        lax.fori_loop(0, last // 2, pair, 0)

        @pl.when(last % 2 == 1)
        def _():
            score_tile(sb_ref, mxb_ref, i, last, i)
            process(sa_ref, mxa_ref, last - 1)
            score_tile(sa_ref, mxa_ref, i_next, 0, i_next)
            process(sb_ref, mxb_ref, last, diagonal=True)

        @pl.when(last % 2 == 0)
        def _():
            process(sa_ref, mxa_ref, last, diagonal=True)
            score_tile(sa_ref, mxa_ref, i_next, 0, i_next)

        rows = pl.ds(pl.multiple_of(i * tq, tq), tq)
        for g in range(PAIRS_PER_STEP):
            inv_l = 1.0 / l_ref[g]
            ot = jnp.concatenate([acc_ref[g, h] * inv_l[:, h * tq:(h + 1) * tq]
                                  for h in range(HEADS_PER_STEP)], axis=0)
            cols = slice(g * LANES, (g + 1) * LANES)
            o_ref[0, rows, cols] = (ot.T * sga_ref[0, rows, cols].astype(F32)).astype(BF16)
        return carry

    lax.fori_loop(0, n_qt, query_tile, 0)


def _merge_kernel(x_ref, ya_ref, ypool_ref, gates_ref, wa_ref, wp_ref, wo_ref, nf_ref, out_ref):
    a = jnp.dot(ya_ref[0], wa_ref[...], preferred_element_type=F32)
    p = jnp.dot(ypool_ref[0], wp_ref[...], preferred_element_type=F32)
    ga = gates_ref[0, :, :D_MODEL].astype(F32)
    gp = gates_ref[0, :, D_MODEL:].astype(F32)
    merged = (ga * a + gp * p).astype(BF16)
    h = x_ref[0] + jnp.dot(merged, wo_ref[...], preferred_element_type=F32)
    out_ref[0] = _rms(h, nf_ref[...])


def _rope_tables(seq):
    half = ROPE // 2
    inv_freq = ROPE_THETA ** (-jnp.arange(half, dtype=F32) / half)
    ang = jnp.arange(seq, dtype=F32)[None, :] * inv_freq[:, None]
    return jnp.cos(ang), jnp.sin(ang)


def _const_spec(shape):
    return pl.BlockSpec(shape, lambda *_: (0,) * len(shape), pipeline_mode=pl.Buffered(1))


def kernel(x, norm_in, w_in, q_norm, w_uq, kv_norm, w_ukv, pool_w, pool_scale,
           w_branch_attn, w_branch_pool, w_out, norm_final):
    b, s, d = x.shape
    tq, tk, t1, t3 = ATTN_TQ, ATTN_TK, PROJ_TILE, MERGE_TILE
    assert d == D_MODEL and s % t1 == 0 and t1 % tk == 0 and t1 % tq == 0 and s % t3 == 0
    hp = HEADS // HEADS_PER_STEP
    nq = HEADS_PER_STEP * tq
    g2 = PAIRS_PER_STEP

    n_gate = w_in.shape[1] - (Q_RANK + KV_RANK + ROPE)
    w_a, w_g = pl.pallas_call(
        _weight_prep_kernel,
        grid=(d // PREP_ROWS,),
        in_specs=[pl.BlockSpec((w_in.shape[1], PREP_ROWS), lambda i: (0, i))],
        out_specs=[pl.BlockSpec((PREP_ROWS, LATENT_COLS), lambda i: (i, 0)),
                   pl.BlockSpec((PREP_ROWS, n_gate), lambda i: (i, 0))],
        out_shape=[jax.ShapeDtypeStruct((d, LATENT_COLS), BF16),
                   jax.ShapeDtypeStruct((d, n_gate), BF16)],
        name="weight_prep",
    )(w_in.T)
    w_uq_p = jnp.pad(w_uq, ((0, 0), (0, 0), (0, SLAB - NOPE - ROPE)))
    w_uqt = w_uq_p.reshape(Q_RANK, HEADS * SLAB).T.astype(BF16)
    w_uk_p = jnp.pad(w_ukv[:, :, :NOPE], ((0, 0), (0, 0), (0, SLAB - NOPE)))
    w_uk = w_uk_p.reshape(KV_RANK, HEADS * SLAB).astype(BF16)
    w_uvt = w_ukv[:, :, NOPE:].reshape(KV_RANK, MLA_WIDTH).T.astype(BF16)
    cos_t, sin_t = _rope_tables(s)

    row = lambda v: v.reshape(1, -1).astype(F32)
    tok = lambda width: pl.BlockSpec((1, t1, width), lambda bi, si: (bi, si, 0))
    tabt = pl.BlockSpec((ROPE // 2, t1), lambda bi, si: (0, si))
    qt, k, vt, sga, ypool, gates = pl.pallas_call(
        _proj_kernel,
        grid=(b, s // t1),
        in_specs=[tok(d), _const_spec((1, d)), _const_spec(w_a.shape), _const_spec(w_g.shape),
                  _const_spec((1, Q_RANK)), _const_spec(w_uqt.shape), _const_spec((1, KV_RANK)),
                  _const_spec(w_uk.shape), _const_spec(w_uvt.shape),
                  _const_spec(pool_w.shape), _const_spec((1, POOL_WIDTH)),
                  tabt, tabt],
        out_specs=[pl.BlockSpec((1, hp, t1 // tq, HEADS_PER_STEP * SLAB, nq),
                                lambda bi, si: (bi, 0, si, 0, 0)),
                   tok(HEADS * SLAB),
                   pl.BlockSpec((1, hp, t1 // tk, VT_ROWS, tk), lambda bi, si: (bi, 0, si, 0, 0)),
                   tok(MLA_WIDTH), tok(POOL_WIDTH), tok(2 * d)],
        out_shape=[jax.ShapeDtypeStruct((b, hp, s // tq, HEADS_PER_STEP * SLAB, nq), BF16),
                   jax.ShapeDtypeStruct((b, s, HEADS * SLAB), BF16),
                   jax.ShapeDtypeStruct((b, hp, s // tk, VT_ROWS, tk), BF16),
                   jax.ShapeDtypeStruct((b, s, MLA_WIDTH), BF16),
                   jax.ShapeDtypeStruct((b, s, POOL_WIDTH), BF16),
                   jax.ShapeDtypeStruct((b, s, 2 * d), BF16)],
        scratch_shapes=[pltpu.VMEM((HALO, POOL_WIDTH), F32)],
        compiler_params=pltpu.CompilerParams(
            dimension_semantics=("arbitrary", "arbitrary"), vmem_limit_bytes=VMEM_LIMIT),
        name="proj",
    )(x, row(norm_in), w_a, w_g, row(q_norm), w_uqt, row(kv_norm), w_uk, w_uvt,
      pool_w.astype(BF16), row(pool_scale), cos_t, sin_t)

    o = pl.pallas_call(
        _attn_kernel,
        grid=(b, hp // g2),
        in_specs=[pl.BlockSpec((1, g2, s // tq, HEADS_PER_STEP * SLAB, nq), lambda bi, pi: (bi, pi, 0, 0, 0)),
                  pl.BlockSpec((1, s, g2 * HEADS_PER_STEP * SLAB), lambda bi, pi: (bi, 0, pi)),
                  pl.BlockSpec((1, g2, s // tk, VT_ROWS, tk), lambda bi, pi: (bi, pi, 0, 0, 0)),
                  pl.BlockSpec((1, s, g2 * LANES), lambda bi, pi: (bi, 0, pi))],
        out_specs=pl.BlockSpec((1, s, g2 * LANES), lambda bi, pi: (bi, 0, pi)),
        out_shape=jax.ShapeDtypeStruct((b, s, MLA_WIDTH), BF16),
        scratch_shapes=[pltpu.VMEM((g2, tk, nq), F32), pltpu.VMEM((g2, tk, nq), F32),
                        pltpu.VMEM((g2, 1, nq), F32), pltpu.VMEM((g2, 1, nq), F32),
                        pltpu.VMEM((g2, 1, nq), F32), pltpu.VMEM((g2, 1, nq), F32),
                        pltpu.VMEM((g2, HEADS_PER_STEP, VDIM, tq), F32)],
        compiler_params=pltpu.CompilerParams(
            dimension_semantics=("arbitrary", "arbitrary"), vmem_limit_bytes=VMEM_LIMIT),
        name="attn",
    )(qt, k, vt, sga)

    tok3 = lambda width: pl.BlockSpec((1, t3, width), lambda bi, si: (bi, si, 0))
    out = pl.pallas_call(
        _merge_kernel,
        grid=(b, s // t3),
        in_specs=[tok3(d), tok3(MLA_WIDTH), tok3(POOL_WIDTH), tok3(2 * d),
                  _const_spec((MLA_WIDTH, d)), _const_spec((POOL_WIDTH, d)), _const_spec((d, d)),
                  _const_spec((1, d))],
        out_specs=tok3(d),
        out_shape=jax.ShapeDtypeStruct((b, s, d), x.dtype),
        compiler_params=pltpu.CompilerParams(
            dimension_semantics=("arbitrary", "arbitrary"), vmem_limit_bytes=VMEM_LIMIT),
        name="merge",
    )(x, o, ypool, gates, w_branch_attn.astype(BF16), w_branch_pool.astype(BF16),
      w_out.astype(BF16), row(norm_final))
    return out
```

```python
import jax
import jax.numpy as jnp
from jax import lax
from jax.experimental import pallas as pl
from jax.experimental.pallas import tpu as pltpu

D_MODEL = 1024
CHUNK = 64
HEADS = 8
NOPE = 64
ROPE = 32
VDIM = 64
Q_RANK = 384
KV_RANK = 256
MLA_WIDTH = HEADS * VDIM
ROPE_THETA = 10000.0
POOL_WINDOWS = (2, 4, 8, 16)
POOL_WIDTH = D_MODEL // 2
POOL_GROUP = POOL_WIDTH // len(POOL_WINDOWS)
EPS = 1e-6
QK_SCALE = (NOPE + ROPE) ** -0.5 * 1.4426950408889634

LANES = 128
HALO = 16
SLAB = LANES
LATENT_COLS = Q_RANK + KV_RANK + SLAB
PREP_ROWS = 256
HEADS_PER_STEP = 2
PAIRS_PER_STEP = 2
ONES_ROWS = 16
VH_ROWS = VDIM + ONES_ROWS
VT_ROWS = HEADS_PER_STEP * VH_ROWS
PROJ_TILE = 512
GATE_COLS = 512
ATTN_TQ = 512
ATTN_TK = 512
MERGE_TILE = 1024
VMEM_LIMIT = 56 * 1024 * 1024

BF16 = jnp.bfloat16
F32 = jnp.float32
NT_DIMS = (((1,), (1,)), ((), ()))


def _rms(x, g):
    return x * lax.rsqrt(jnp.mean(x * x, axis=-1, keepdims=True) + EPS) * g


def _sigmoid(x):
    return 1.0 / (1.0 + jnp.exp(-x))


def _weight_prep_kernel(wt_ref, wba_ref, wbp_ref, wo_ref, wa_ref, wg_ref, wba_out, wbp_out, wo_out):
    wba_out[...] = wba_ref[...].astype(BF16)
    wbp_out[...] = wbp_ref[...].astype(BF16)
    wo_out[...] = wo_ref[...].astype(BF16)
    o_kr = Q_RANK + KV_RANK
    wa_ref[:, :o_kr] = wt_ref[:o_kr, :].T.astype(BF16)
    kr = jnp.concatenate([jnp.zeros((NOPE, PREP_ROWS), F32), wt_ref[o_kr:o_kr + ROPE, :],
                          jnp.zeros((SLAB - NOPE - ROPE, PREP_ROWS), F32)], axis=0)
    wa_ref[:, o_kr:] = kr.T.astype(BF16)
    wg_ref[...] = wt_ref[o_kr + ROPE:, :].T.astype(BF16)


def _proj_kernel(x_ref, nin_ref, wa_ref, wg_ref, qn_ref, wuqt_ref, kvn_ref, wuk_ref, wuvt_ref,
                 poolw_ref, pscale_ref, cos_ref, sin_ref,
                 qt_out, k_out, vt_out, sga_out, ypool_out, gates_out, carry_ref):
    t = PROJ_TILE
    si = pl.program_id(1)

    @pl.when(si == 0)
    def _():
        carry_ref[...] = jnp.zeros_like(carry_ref)

    x = x_ref[0]
    hn = _rms(x, nin_ref[...]).astype(BF16)

    za = jnp.dot(hn, wa_ref[...], preferred_element_type=F32)
    zq = za[:, :Q_RANK]
    zkv = za[:, Q_RANK:Q_RANK + KV_RANK]
    zkr = za[:, Q_RANK + KV_RANK:]
    cq = _rms(zq, qn_ref[...]).astype(BF16)
    ckv = _rms(zkv, kvn_ref[...]).astype(BF16)

    qt = lax.dot_general(wuqt_ref[...], cq, NT_DIMS, preferred_element_type=F32)
    scale = QK_SCALE
    cos, sin = cos_ref[...], sin_ref[...]
    qcos, qsin = cos * scale, sin * scale
    half = ROPE // 2
    tq = ATTN_TQ
    for h in range(HEADS):
        r0 = h * SLAB
        x1 = qt[r0 + NOPE:r0 + NOPE + half]
        x2 = qt[r0 + NOPE + half:r0 + NOPE + ROPE]
        slab = jnp.concatenate([qt[r0:r0 + NOPE] * scale, x1 * qcos - x2 * qsin, x1 * qsin + x2 * qcos,
                                jnp.zeros((SLAB - NOPE - ROPE, t), F32)], axis=0).astype(BF16)
        pr, hs = h // HEADS_PER_STEP, h % HEADS_PER_STEP
        for c in range(t // tq):
            for cs in range(HEADS_PER_STEP):
                blk = slab[:, c * tq:(c + 1) * tq] if cs == hs else jnp.zeros((SLAB, tq), BF16)
                qt_out[0, pr, c, hs * SLAB:(hs + 1) * SLAB, cs * tq:(cs + 1) * tq] = blk

    kf = jnp.dot(ckv, wuk_ref[...], preferred_element_type=F32)
    zt = zkr.T
    k1, k2 = zt[NOPE:NOPE + half], zt[NOPE + half:NOPE + ROPE]
    kr = jnp.concatenate([jnp.zeros((NOPE, t), F32), k1 * cos - k2 * sin, k1 * sin + k2 * cos,
                          jnp.zeros((SLAB - NOPE - ROPE, t), F32)], axis=0).T
    for h in range(HEADS):
        sl = slice(h * SLAB, (h + 1) * SLAB)
        k_out[0, :, sl] = (kf[:, sl] + kr).astype(BF16)

    vt = lax.dot_general(wuvt_ref[...], ckv, NT_DIMS, preferred_element_type=F32)
    for c in range(t // ATTN_TK):
        cols = slice(c * ATTN_TK, (c + 1) * ATTN_TK)
        for h in range(HEADS):
            pr, r0 = h // HEADS_PER_STEP, (h % HEADS_PER_STEP) * VH_ROWS
            vt_out[0, pr, c, r0:r0 + VDIM, :] = vt[h * VDIM:(h + 1) * VDIM, cols].astype(BF16)
            vt_out[0, pr, c, r0 + VDIM:r0 + VH_ROWS, :] = jnp.ones((ONES_ROWS, ATTN_TK), BF16)

    zg = jnp.dot(hn, wg_ref[:, :3 * POOL_WIDTH], preferred_element_type=F32)
    g_attn = zg[:, :MLA_WIDTH]
    u = zg[:, MLA_WIDTH:MLA_WIDTH + POOL_WIDTH]
    g_pool = zg[:, MLA_WIDTH + POOL_WIDTH:]
    sga_out[0] = (g_attn * _sigmoid(g_attn)).astype(BF16)
    for c in range(2 * D_MODEL // GATE_COLS):
        lo = 3 * POOL_WIDTH + c * GATE_COLS
        gm = jnp.dot(hn, wg_ref[:, lo:lo + GATE_COLS], preferred_element_type=F32)
        gates_out[0, :, c * GATE_COLS:(c + 1) * GATE_COLS] = _sigmoid(gm).astype(BF16)

    ue = jnp.concatenate([carry_ref[...], u], axis=0)
    carry_ref[...] = u[t - HALO:, :]
    pos = si * t + lax.broadcasted_iota(jnp.int32, (t, 1), 0) + 1
    outs = []
    for gi, w in enumerate(POOL_WINDOWS):
        sl = slice(gi * POOL_GROUP, (gi + 1) * POOL_GROUP)
        acc = ue[:, sl]
        step = 1
        while step < w:
            acc = acc + pltpu.roll(acc, step, axis=0)
            step *= 2
        inv = 1.0 / jnp.minimum(pos, w).astype(F32)
        d = acc[HALO:, :] * inv - u[:, sl]
        outs.append(jnp.dot(d.astype(BF16), poolw_ref[gi], preferred_element_type=F32))
    y = jnp.concatenate(outs, axis=-1) * pscale_ref[...]
    ypool_out[0] = (y * (g_pool * _sigmoid(g_pool))).astype(BF16)


def _attn_kernel(qbd_ref, k_ref, vt_ref, sga_ref, o_ref, sa_ref, sb_ref, mxa_ref, mxb_ref,
                 m_ref, l_ref, acc_ref):
    tq, tk = ATTN_TQ, ATTN_TK
    nq = HEADS_PER_STEP * tq
    n_qt = qbd_ref.shape[2]
    kblocks = tk // CHUNK
    qry_chunk = lax.shift_right_logical(lax.broadcasted_iota(jnp.int32, (1, 1, nq), 2) & (tq - 1),
                                        CHUNK.bit_length() - 1)
    key_block = lax.broadcasted_iota(jnp.int32, (kblocks, 1, 1), 0)

    def score_tile(s_ref, mx_ref, qi, j, mask_for, diagonal=False):
        split = diagonal and tq == tk
        hk, hq, hb = tk // 2, tq // 2, kblocks // 2
        if mask_for is not None:
            allowed = key_block + (j * kblocks - mask_for * (tq // CHUNK)) <= qry_chunk
            bias = jnp.where(allowed, 0.0, -jnp.inf).astype(F32)
        for g in range(PAIRS_PER_STEP):
            kcols = slice(g * HEADS_PER_STEP * SLAB, (g + 1) * HEADS_PER_STEP * SLAB)
            if not split:
                kt = k_ref[0, pl.ds(pl.multiple_of(j * tk, tk), tk), kcols]
                s = jnp.dot(kt, qbd_ref[0, g, qi], preferred_element_type=F32)
                if mask_for is not None:
                    s = (s.reshape(kblocks, CHUNK, nq) + bias).reshape(tk, nq)
                s_ref[g] = s
                mx_ref[g] = jnp.max(s, axis=0, keepdims=True)
                continue
            k_top = k_ref[0, pl.ds(pl.multiple_of(j * tk, tk), hk), kcols]
            k_bot = k_ref[0, pl.ds(pl.multiple_of(j * tk + hk, hk), hk), kcols]
            s = jnp.dot(k_top, qbd_ref[0, g, qi], preferred_element_type=F32)
            s = (s.reshape(hb, CHUNK, nq) + bias[:hb]).reshape(hk, nq)
            s_ref[g, :hk, :] = s
            mx = jnp.max(s, axis=0, keepdims=True)
            mx_ref[g] = mx
            for h in range(HEADS_PER_STEP):
                late = slice(h * tq + hq, (h + 1) * tq)
                s = jnp.dot(k_bot, qbd_ref[0, g, qi, :, late], preferred_element_type=F32)
                s = (s.reshape(hb, CHUNK, hq) + bias[hb:, :, late]).reshape(hk, hq)
                s_ref[g, hk:, late] = s
                mx_ref[g, :, late] = jnp.maximum(mx[:, late], jnp.max(s, axis=0, keepdims=True))

    def process(s_ref, mx_ref, j, diagonal=False):
        split = diagonal and tq == tk
        hk, hq = tk // 2, tq // 2
        for g in range(PAIRS_PER_STEP):
            m_old = m_ref[g]
            m_new = jnp.maximum(m_old, mx_ref[g])
            p = jnp.exp2((s_ref[g, :hk, :] if split else s_ref[g]) - m_new).astype(BF16)
            alpha = jnp.exp2(m_old - m_new)
            m_ref[g] = m_new
            for h in range(HEADS_PER_STEP):
                cols = slice(h * tq, (h + 1) * tq)
                vrows = slice(h * VH_ROWS, (h + 1) * VH_ROWS)
                if split:
                    late = slice(h * tq + hq, (h + 1) * tq)
                    p_late = jnp.exp2(s_ref[g, hk:, late] - m_new[:, late]).astype(BF16)
                    pv = jnp.dot(vt_ref[0, g, j, vrows, :hk], p[:, cols], preferred_element_type=F32)
                    pv_late = jnp.dot(vt_ref[0, g, j, vrows, hk:], p_late, preferred_element_type=F32)
                    pv = jnp.concatenate([pv[:, :hq], pv[:, hq:] + pv_late], axis=1)
                else:
                    pv = jnp.dot(vt_ref[0, g, j, vrows, :], p[:, cols],
                                 preferred_element_type=F32)
                l_ref[g, :, cols] = alpha[:, cols] * l_ref[g, :, cols] + pv[VDIM:VDIM + 1, :]
                acc_ref[g, h] = alpha[:, cols] * acc_ref[g, h] + pv[:VDIM, :]

    score_tile(sa_ref, mxa_ref, 0, 0, 0)

    def query_tile(i, carry):
        last = ((i + 1) * tq + tk - 1) // tk - 1
        i_next = jnp.minimum(i + 1, n_qt - 1)
        m_ref[...] = jnp.full_like(m_ref, -jnp.inf)
        l_ref[...] = jnp.zeros_like(l_ref)
        acc_ref[...] = jnp.zeros_like(acc_ref)

        def pair(jj, c):
            j = 2 * jj
            score_tile(sb_ref, mxb_ref, i, j + 1, None)
            process(sa_ref, mxa_ref, j)
            score_tile(sa_ref, mxa_ref, i, j + 2, i)
            process(sb_ref, mxb_ref, j + 1)
            return c

        lax.fori_loop(0, last // 2, pair, 0)

        @pl.when(last % 2 == 1)
        def _():
            score_tile(sb_ref, mxb_ref, i, last, i, diagonal=True)
            process(sa_ref, mxa_ref, last - 1)
            score_tile(sa_ref, mxa_ref, i_next, 0, i_next)
            process(sb_ref, mxb_ref, last, diagonal=True)

        @pl.when(last % 2 == 0)
        def _():
            process(sa_ref, mxa_ref, last, diagonal=True)
            score_tile(sa_ref, mxa_ref, i_next, 0, i_next)

        rows = pl.ds(pl.multiple_of(i * tq, tq), tq)
        for g in range(PAIRS_PER_STEP):
            inv_l = 1.0 / l_ref[g]
            ot = jnp.concatenate([acc_ref[g, h] * inv_l[:, h * tq:(h + 1) * tq]
                                  for h in range(HEADS_PER_STEP)], axis=0)
            cols = slice(g * LANES, (g + 1) * LANES)
            o_ref[0, rows, cols] = (ot.T * sga_ref[0, rows, cols].astype(F32)).astype(BF16)
        return carry

    lax.fori_loop(0, n_qt, query_tile, 0)


def _merge_kernel(x_ref, ya_ref, ypool_ref, gates_ref, wa_ref, wp_ref, wo_ref, nf_ref, out_ref):
    a = jnp.dot(ya_ref[0], wa_ref[...], preferred_element_type=F32)
    p = jnp.dot(ypool_ref[0], wp_ref[...], preferred_element_type=F32)
    ga = gates_ref[0, :, :D_MODEL].astype(F32)
    gp = gates_ref[0, :, D_MODEL:].astype(F32)
    merged = (ga * a + gp * p).astype(BF16)
    h = x_ref[0] + jnp.dot(merged, wo_ref[...], preferred_element_type=F32)
    out_ref[0] = _rms(h, nf_ref[...])


def _rope_tables(seq):
    half = ROPE // 2
    inv_freq = ROPE_THETA ** (-jnp.arange(half, dtype=F32) / half)
    ang = jnp.arange(seq, dtype=F32)[None, :] * inv_freq[:, None]
    return jnp.cos(ang), jnp.sin(ang)


def _const_spec(shape):
    return pl.BlockSpec(shape, lambda *_: (0,) * len(shape), pipeline_mode=pl.Buffered(1))


def kernel(x, norm_in, w_in, q_norm, w_uq, kv_norm, w_ukv, pool_w, pool_scale,
           w_branch_attn, w_branch_pool, w_out, norm_final):
    b, s, d = x.shape
    tq, tk, t1, t3 = ATTN_TQ, ATTN_TK, PROJ_TILE, MERGE_TILE
    assert d == D_MODEL and s % t1 == 0 and t1 % tk == 0 and t1 % tq == 0 and s % t3 == 0
    hp = HEADS // HEADS_PER_STEP
    nq = HEADS_PER_STEP * tq
    g2 = PAIRS_PER_STEP

    n_gate = w_in.shape[1] - (Q_RANK + KV_RANK + ROPE)
    n_prep = d // PREP_ROWS
    rows_spec = lambda rows: pl.BlockSpec((rows // n_prep, d), lambda i: (i, 0))
    w_a, w_g, w_ba16, w_bp16, w_o16 = pl.pallas_call(
        _weight_prep_kernel,
        grid=(n_prep,),
        in_specs=[pl.BlockSpec((w_in.shape[1], PREP_ROWS), lambda i: (0, i)),
                  rows_spec(MLA_WIDTH), rows_spec(POOL_WIDTH), rows_spec(d)],
        out_specs=[pl.BlockSpec((PREP_ROWS, LATENT_COLS), lambda i: (i, 0)),
                   pl.BlockSpec((PREP_ROWS, n_gate), lambda i: (i, 0)),
                   rows_spec(MLA_WIDTH), rows_spec(POOL_WIDTH), rows_spec(d)],
        out_shape=[jax.ShapeDtypeStruct((d, LATENT_COLS), BF16),
                   jax.ShapeDtypeStruct((d, n_gate), BF16),
                   jax.ShapeDtypeStruct((MLA_WIDTH, d), BF16),
                   jax.ShapeDtypeStruct((POOL_WIDTH, d), BF16),
                   jax.ShapeDtypeStruct((d, d), BF16)],
        name="weight_prep",
    )(w_in.T, w_branch_attn, w_branch_pool, w_out)
    w_uq_p = jnp.pad(w_uq, ((0, 0), (0, 0), (0, SLAB - NOPE - ROPE)))
    w_uqt = w_uq_p.reshape(Q_RANK, HEADS * SLAB).T.astype(BF16)
    w_uk_p = jnp.pad(w_ukv[:, :, :NOPE], ((0, 0), (0, 0), (0, SLAB - NOPE)))
    w_uk = w_uk_p.reshape(KV_RANK, HEADS * SLAB).astype(BF16)
    w_uvt = w_ukv[:, :, NOPE:].reshape(KV_RANK, MLA_WIDTH).T.astype(BF16)
    cos_t, sin_t = _rope_tables(s)

    row = lambda v: v.reshape(1, -1).astype(F32)
    tok = lambda width: pl.BlockSpec((1, t1, width), lambda bi, si: (bi, si, 0))
    tabt = pl.BlockSpec((ROPE // 2, t1), lambda bi, si: (0, si))
    qt, k, vt, sga, ypool, gates = pl.pallas_call(
        _proj_kernel,
        grid=(b, s // t1),
        in_specs=[tok(d), _const_spec((1, d)), _const_spec(w_a.shape), _const_spec(w_g.shape),
                  _const_spec((1, Q_RANK)), _const_spec(w_uqt.shape), _const_spec((1, KV_RANK)),
                  _const_spec(w_uk.shape), _const_spec(w_uvt.shape),
                  _const_spec(pool_w.shape), _const_spec((1, POOL_WIDTH)),
                  tabt, tabt],
        out_specs=[pl.BlockSpec((1, hp, t1 // tq, HEADS_PER_STEP * SLAB, nq),
                                lambda bi, si: (bi, 0, si, 0, 0)),
                   tok(HEADS * SLAB),
                   pl.BlockSpec((1, hp, t1 // tk, VT_ROWS, tk), lambda bi, si: (bi, 0, si, 0, 0)),
                   tok(MLA_WIDTH), tok(POOL_WIDTH), tok(2 * d)],
        out_shape=[jax.ShapeDtypeStruct((b, hp, s // tq, HEADS_PER_STEP * SLAB, nq), BF16),
                   jax.ShapeDtypeStruct((b, s, HEADS * SLAB), BF16),
                   jax.ShapeDtypeStruct((b, hp, s // tk, VT_ROWS, tk), BF16),
                   jax.ShapeDtypeStruct((b, s, MLA_WIDTH), BF16),
                   jax.ShapeDtypeStruct((b, s, POOL_WIDTH), BF16),
                   jax.ShapeDtypeStruct((b, s, 2 * d), BF16)],
        scratch_shapes=[pltpu.VMEM((HALO, POOL_WIDTH), F32)],
        compiler_params=pltpu.CompilerParams(
            dimension_semantics=("arbitrary", "arbitrary"), vmem_limit_bytes=VMEM_LIMIT),
        name="proj",
    )(x, row(norm_in), w_a, w_g, row(q_norm), w_uqt, row(kv_norm), w_uk, w_uvt,
      pool_w.astype(BF16), row(pool_scale), cos_t, sin_t)

    o = pl.pallas_call(
        _attn_kernel,
        grid=(b, hp // g2),
        in_specs=[pl.BlockSpec((1, g2, s // tq, HEADS_PER_STEP * SLAB, nq), lambda bi, pi: (bi, pi, 0, 0, 0)),
                  pl.BlockSpec((1, s, g2 * HEADS_PER_STEP * SLAB), lambda bi, pi: (bi, 0, pi)),
                  pl.BlockSpec((1, g2, s // tk, VT_ROWS, tk), lambda bi, pi: (bi, pi, 0, 0, 0)),
                  pl.BlockSpec((1, s, g2 * LANES), lambda bi, pi: (bi, 0, pi))],
        out_specs=pl.BlockSpec((1, s, g2 * LANES), lambda bi, pi: (bi, 0, pi)),
        out_shape=jax.ShapeDtypeStruct((b, s, MLA_WIDTH), BF16),
        scratch_shapes=[pltpu.VMEM((g2, tk, nq), F32), pltpu.VMEM((g2, tk, nq), F32),
                        pltpu.VMEM((g2, 1, nq), F32), pltpu.VMEM((g2, 1, nq), F32),
                        pltpu.VMEM((g2, 1, nq), F32), pltpu.VMEM((g2, 1, nq), F32),
                        pltpu.VMEM((g2, HEADS_PER_STEP, VDIM, tq), F32)],
        compiler_params=pltpu.CompilerParams(
            dimension_semantics=("arbitrary", "arbitrary"), vmem_limit_bytes=VMEM_LIMIT),
        name="attn",
    )(qt, k, vt, sga)

    tok3 = lambda width: pl.BlockSpec((1, t3, width), lambda bi, si: (bi, si, 0))
    out = pl.pallas_call(
        _merge_kernel,
        grid=(b, s // t3),
        in_specs=[tok3(d), tok3(MLA_WIDTH), tok3(POOL_WIDTH), tok3(2 * d),
                  _const_spec((MLA_WIDTH, d)), _const_spec((POOL_WIDTH, d)), _const_spec((d, d)),
                  _const_spec((1, d))],
        out_specs=tok3(d),
        out_shape=jax.ShapeDtypeStruct((b, s, d), x.dtype),
        compiler_params=pltpu.CompilerParams(
            dimension_semantics=("arbitrary", "arbitrary"), vmem_limit_bytes=VMEM_LIMIT),
        name="merge",
    )(x, o, ypool, gates, w_ba16, w_bp16, w_o16, row(norm_final))
    return out
```

```python
import jax
import jax.numpy as jnp
from jax import lax
from jax.experimental import pallas as pl
from jax.experimental.pallas import tpu as pltpu

D_MODEL = 1024
CHUNK = 64
HEADS = 8
NOPE = 64
ROPE = 32
VDIM = 64
Q_RANK = 384
KV_RANK = 256
MLA_WIDTH = HEADS * VDIM
ROPE_THETA = 10000.0
POOL_WINDOWS = (2, 4, 8, 16)
POOL_WIDTH = D_MODEL // 2
POOL_GROUP = POOL_WIDTH // len(POOL_WINDOWS)
EPS = 1e-6
QK_SCALE = (NOPE + ROPE) ** -0.5 * 1.4426950408889634

LANES = 128
HALO = 16
SLAB = LANES
LATENT_COLS = Q_RANK + KV_RANK + SLAB
PREP_ROWS = 256
HEADS_PER_STEP = 2
PAIRS_PER_STEP = 2
ONES_ROWS = 16
VH_ROWS = VDIM + ONES_ROWS
VT_ROWS = HEADS_PER_STEP * VH_ROWS
PROJ_TILE = 512
GATE_COLS = 512
ATTN_TQ = 512
ATTN_TK = 512
MERGE_TILE = 1024
VMEM_LIMIT = 56 * 1024 * 1024

BF16 = jnp.bfloat16
F32 = jnp.float32
NT_DIMS = (((1,), (1,)), ((), ()))


def _rms(x, g):
    return x * lax.rsqrt(jnp.mean(x * x, axis=-1, keepdims=True) + EPS) * g


def _sigmoid(x):
    return 1.0 / (1.0 + jnp.exp(-x))


def _weight_prep_kernel(wt_ref, wba_ref, wbp_ref, wo_ref, wa_ref, wg_ref, wba_out, wbp_out, wo_out):
    wba_out[...] = wba_ref[...].astype(BF16)
    wbp_out[...] = wbp_ref[...].astype(BF16)
    wo_out[...] = wo_ref[...].astype(BF16)
    o_kr = Q_RANK + KV_RANK
    wa_ref[:, :o_kr] = wt_ref[:o_kr, :].T.astype(BF16)
    kr = jnp.concatenate([jnp.zeros((NOPE, PREP_ROWS), F32), wt_ref[o_kr:o_kr + ROPE, :],
                          jnp.zeros((SLAB - NOPE - ROPE, PREP_ROWS), F32)], axis=0)
    wa_ref[:, o_kr:] = kr.T.astype(BF16)
    wg_ref[...] = wt_ref[o_kr + ROPE:, :].T.astype(BF16)


def _proj_kernel(x_ref, nin_ref, wa_ref, wg_ref, qn_ref, wuqt_ref, kvn_ref, wuk_ref, wuvt_ref,
                 poolw_ref, pscale_ref, cos_ref, sin_ref,
                 qt_out, k_out, vt_out, sga_out, ypool_out, gates_out, carry_ref):
    t = PROJ_TILE
    si = pl.program_id(1)

    @pl.when(si == 0)
    def _():
        carry_ref[...] = jnp.zeros_like(carry_ref)

    x = x_ref[0]
    hn = _rms(x, nin_ref[...]).astype(BF16)

    za = jnp.dot(hn, wa_ref[...], preferred_element_type=F32)
    zq = za[:, :Q_RANK]
    zkv = za[:, Q_RANK:Q_RANK + KV_RANK]
    zkr = za[:, Q_RANK + KV_RANK:]
    cq = _rms(zq, qn_ref[...]).astype(BF16)
    ckv = _rms(zkv, kvn_ref[...]).astype(BF16)

    qt = lax.dot_general(wuqt_ref[...], cq, NT_DIMS, preferred_element_type=F32)
    scale = QK_SCALE
    cos, sin = cos_ref[...], sin_ref[...]
    qcos, qsin = cos * scale, sin * scale
    half = ROPE // 2
    tq = ATTN_TQ
    for h in range(HEADS):
        r0 = h * SLAB
        x1 = qt[r0 + NOPE:r0 + NOPE + half]
        x2 = qt[r0 + NOPE + half:r0 + NOPE + ROPE]
        slab = jnp.concatenate([qt[r0:r0 + NOPE] * scale, x1 * qcos - x2 * qsin, x1 * qsin + x2 * qcos,
                                jnp.zeros((SLAB - NOPE - ROPE, t), F32)], axis=0).astype(BF16)
        pr, hs = h // HEADS_PER_STEP, h % HEADS_PER_STEP
        for c in range(t // tq):
            for cs in range(HEADS_PER_STEP):
                blk = slab[:, c * tq:(c + 1) * tq] if cs == hs else jnp.zeros((SLAB, tq), BF16)
                qt_out[0, pr, c, hs * SLAB:(hs + 1) * SLAB, cs * tq:(cs + 1) * tq] = blk

    kf = jnp.dot(ckv, wuk_ref[...], preferred_element_type=F32)
    zt = zkr.T
    k1, k2 = zt[NOPE:NOPE + half], zt[NOPE + half:NOPE + ROPE]
    kr = jnp.concatenate([jnp.zeros((NOPE, t), F32), k1 * cos - k2 * sin, k1 * sin + k2 * cos,
                          jnp.zeros((SLAB - NOPE - ROPE, t), F32)], axis=0).T
    for h in range(HEADS):
        sl = slice(h * SLAB, (h + 1) * SLAB)
        k_out[0, :, sl] = (kf[:, sl] + kr).astype(BF16)

    vt = lax.dot_general(wuvt_ref[...], ckv, NT_DIMS, preferred_element_type=F32)
    for c in range(t // ATTN_TK):
        cols = slice(c * ATTN_TK, (c + 1) * ATTN_TK)
        for h in range(HEADS):
            pr, r0 = h // HEADS_PER_STEP, (h % HEADS_PER_STEP) * VH_ROWS
            vt_out[0, pr, c, r0:r0 + VDIM, :] = vt[h * VDIM:(h + 1) * VDIM, cols].astype(BF16)
            vt_out[0, pr, c, r0 + VDIM:r0 + VH_ROWS, :] = jnp.ones((ONES_ROWS, ATTN_TK), BF16)

    zg = jnp.dot(hn, wg_ref[:, :3 * POOL_WIDTH], preferred_element_type=F32)
    g_attn = zg[:, :MLA_WIDTH]
    u = zg[:, MLA_WIDTH:MLA_WIDTH + POOL_WIDTH]
    g_pool = zg[:, MLA_WIDTH + POOL_WIDTH:]
    sga_out[0] = (g_attn * _sigmoid(g_attn)).astype(BF16)
    for c in range(2 * D_MODEL // GATE_COLS):
        lo = 3 * POOL_WIDTH + c * GATE_COLS
        gm = jnp.dot(hn, wg_ref[:, lo:lo + GATE_COLS], preferred_element_type=F32)
        gates_out[0, :, c * GATE_COLS:(c + 1) * GATE_COLS] = _sigmoid(gm).astype(BF16)

    ue = jnp.concatenate([carry_ref[...], u], axis=0)
    carry_ref[...] = u[t - HALO:, :]
    pos = si * t + lax.broadcasted_iota(jnp.int32, (t, 1), 0) + 1
    outs = []
    for gi, w in enumerate(POOL_WINDOWS):
        sl = slice(gi * POOL_GROUP, (gi + 1) * POOL_GROUP)
        acc = ue[:, sl]
        step = 1
        while step < w:
            acc = acc + pltpu.roll(acc, step, axis=0)
            step *= 2
        inv = 1.0 / jnp.minimum(pos, w).astype(F32)
        d = acc[HALO:, :] * inv - u[:, sl]
        outs.append(jnp.dot(d.astype(BF16), poolw_ref[gi], preferred_element_type=F32))
    y = jnp.concatenate(outs, axis=-1) * pscale_ref[...]
    ypool_out[0] = (y * (g_pool * _sigmoid(g_pool))).astype(BF16)


def _attn_kernel(qbd_ref, k_ref, vt_ref, sga_ref, o_ref, sa_ref, sb_ref, mxa_ref, mxb_ref,
                 m_ref, l_ref, acc_ref):
    tq, tk = ATTN_TQ, ATTN_TK
    nq = HEADS_PER_STEP * tq
    n_qt = qbd_ref.shape[2]
    kblocks = tk // CHUNK
    qry_chunk = lax.shift_right_logical(lax.broadcasted_iota(jnp.int32, (1, 1, nq), 2) & (tq - 1),
                                        CHUNK.bit_length() - 1)
    key_block = lax.broadcasted_iota(jnp.int32, (kblocks, 1, 1), 0)

    def score_tile(s_ref, mx_ref, qi, j, mask_for, diagonal=False):
        split = diagonal and tq == tk
        hk, hq, hb = tk // 2, tq // 2, kblocks // 2
        if mask_for is not None:
            allowed = key_block + (j * kblocks - mask_for * (tq // CHUNK)) <= qry_chunk
            bias = jnp.where(allowed, 0.0, -jnp.inf).astype(F32)
        for g in range(PAIRS_PER_STEP):
            kcols = slice(g * HEADS_PER_STEP * SLAB, (g + 1) * HEADS_PER_STEP * SLAB)
            if not split:
                kt = k_ref[0, pl.ds(pl.multiple_of(j * tk, tk), tk), kcols]
                s = jnp.dot(kt, qbd_ref[0, g, qi], preferred_element_type=F32)
                if mask_for is not None:
                    s = (s.reshape(kblocks, CHUNK, nq) + bias).reshape(tk, nq)
                s_ref[g] = s
                mx_ref[g] = jnp.max(s, axis=0, keepdims=True)
                continue
            k_top = k_ref[0, pl.ds(pl.multiple_of(j * tk, tk), hk), kcols]
            k_bot = k_ref[0, pl.ds(pl.multiple_of(j * tk + hk, hk), hk), kcols]
            s = jnp.dot(k_top, qbd_ref[0, g, qi], preferred_element_type=F32)
            s = (s.reshape(hb, CHUNK, nq) + bias[:hb]).reshape(hk, nq)
            s_ref[g, :hk, :] = s
            mx = jnp.max(s, axis=0, keepdims=True)
            mx_ref[g] = mx
            for h in range(HEADS_PER_STEP):
                late = slice(h * tq + hq, (h + 1) * tq)
                s = jnp.dot(k_bot, qbd_ref[0, g, qi, :, late], preferred_element_type=F32)
                s = (s.reshape(hb, CHUNK, hq) + bias[hb:, :, late]).reshape(hk, hq)
                s_ref[g, hk:, late] = s
                mx_ref[g, :, late] = jnp.maximum(mx[:, late], jnp.max(s, axis=0, keepdims=True))

    def process(s_ref, mx_ref, j, diagonal=False):
        split = diagonal and tq == tk
        hk, hq = tk // 2, tq // 2
        for g in range(PAIRS_PER_STEP):
            m_old = m_ref[g]
            m_new = jnp.maximum(m_old, mx_ref[g])
            p = jnp.exp2((s_ref[g, :hk, :] if split else s_ref[g]) - m_new).astype(BF16)
            alpha = jnp.exp2(m_old - m_new)
            m_ref[g] = m_new
            for h in range(HEADS_PER_STEP):
                cols = slice(h * tq, (h + 1) * tq)
                vrows = slice(h * VH_ROWS, (h + 1) * VH_ROWS)
                if split:
                    late = slice(h * tq + hq, (h + 1) * tq)
                    p_late = jnp.exp2(s_ref[g, hk:, late] - m_new[:, late]).astype(BF16)
                    pv = jnp.dot(vt_ref[0, g, j, vrows, :hk], p[:, cols], preferred_element_type=F32)
                    pv_late = jnp.dot(vt_ref[0, g, j, vrows, hk:], p_late, preferred_element_type=F32)
                    pv = jnp.concatenate([pv[:, :hq], pv[:, hq:] + pv_late], axis=1)
                else:
                    pv = jnp.dot(vt_ref[0, g, j, vrows, :], p[:, cols],
                                 preferred_element_type=F32)
                l_ref[g, :, cols] = alpha[:, cols] * l_ref[g, :, cols] + pv[VDIM:VDIM + 1, :]
                acc_ref[g, h] = alpha[:, cols] * acc_ref[g, h] + pv[:VDIM, :]

    score_tile(sa_ref, mxa_ref, 0, 0, 0, diagonal=True)

    def query_tile(i, carry):
        last = ((i + 1) * tq + tk - 1) // tk - 1
        i_next = jnp.minimum(i + 1, n_qt - 1)
        m_ref[...] = jnp.full_like(m_ref, -jnp.inf)
        l_ref[...] = jnp.zeros_like(l_ref)
        acc_ref[...] = jnp.zeros_like(acc_ref)
        below_mask = None if tq == tk else i

        def pair(j, scores_last):
            score_tile(sb_ref, mxb_ref, i, j + 1, below_mask)
            process(sa_ref, mxa_ref, j)
            score_tile(sa_ref, mxa_ref, i, j + 2, i if scores_last else below_mask, diagonal=scores_last)
            process(sb_ref, mxb_ref, j + 1)

        even = last % 2 == 0
        peel = even & (last >= 2)

        def pair_body(jj, c):
            pair(2 * jj, False)
            return c

        lax.fori_loop(0, last // 2 - jnp.where(peel, 1, 0), pair_body, 0)

        @pl.when(peel)
        def _():
            pair(last - 2, True)

        @pl.when(last % 2 == 1)
        def _():
            score_tile(sb_ref, mxb_ref, i, last, i, diagonal=True)
            process(sa_ref, mxa_ref, last - 1)
            score_tile(sa_ref, mxa_ref, i_next, 0, i_next)
            process(sb_ref, mxb_ref, last, diagonal=True)

        @pl.when(even)
        def _():
            process(sa_ref, mxa_ref, last, diagonal=True)
            score_tile(sa_ref, mxa_ref, i_next, 0, i_next)

        rows = pl.ds(pl.multiple_of(i * tq, tq), tq)
        for g in range(PAIRS_PER_STEP):
            inv_l = 1.0 / l_ref[g]
            ot = jnp.concatenate([acc_ref[g, h] * inv_l[:, h * tq:(h + 1) * tq]
                                  for h in range(HEADS_PER_STEP)], axis=0)
            cols = slice(g * LANES, (g + 1) * LANES)
            o_ref[0, rows, cols] = (ot.T * sga_ref[0, rows, cols].astype(F32)).astype(BF16)
        return carry

    lax.fori_loop(0, n_qt, query_tile, 0)


def _merge_kernel(x_ref, ya_ref, ypool_ref, gates_ref, wa_ref, wp_ref, wo_ref, nf_ref, out_ref):
    a = jnp.dot(ya_ref[0], wa_ref[...], preferred_element_type=F32)
    p = jnp.dot(ypool_ref[0], wp_ref[...], preferred_element_type=F32)
    ga = gates_ref[0, :, :D_MODEL].astype(F32)
    gp = gates_ref[0, :, D_MODEL:].astype(F32)
    merged = (ga * a + gp * p).astype(BF16)
    h = x_ref[0] + jnp.dot(merged, wo_ref[...], preferred_element_type=F32)
    out_ref[0] = _rms(h, nf_ref[...])


def _rope_tables(seq):
    half = ROPE // 2
    inv_freq = ROPE_THETA ** (-jnp.arange(half, dtype=F32) / half)
    ang = jnp.arange(seq, dtype=F32)[None, :] * inv_freq[:, None]
    return jnp.cos(ang), jnp.sin(ang)


def _const_spec(shape):
    return pl.BlockSpec(shape, lambda *_: (0,) * len(shape), pipeline_mode=pl.Buffered(1))


def kernel(x, norm_in, w_in, q_norm, w_uq, kv_norm, w_ukv, pool_w, pool_scale,
           w_branch_attn, w_branch_pool, w_out, norm_final):
    b, s, d = x.shape
    tq, tk, t1, t3 = ATTN_TQ, ATTN_TK, PROJ_TILE, MERGE_TILE
    assert d == D_MODEL and s % t1 == 0 and t1 % tk == 0 and t1 % tq == 0 and s % t3 == 0
    hp = HEADS // HEADS_PER_STEP
    nq = HEADS_PER_STEP * tq
    g2 = PAIRS_PER_STEP

    n_gate = w_in.shape[1] - (Q_RANK + KV_RANK + ROPE)
    n_prep = d // PREP_ROWS
    rows_spec = lambda rows: pl.BlockSpec((rows // n_prep, d), lambda i: (i, 0))
    w_a, w_g, w_ba16, w_bp16, w_o16 = pl.pallas_call(
        _weight_prep_kernel,
        grid=(n_prep,),
        in_specs=[pl.BlockSpec((w_in.shape[1], PREP_ROWS), lambda i: (0, i)),
                  rows_spec(MLA_WIDTH), rows_spec(POOL_WIDTH), rows_spec(d)],
        out_specs=[pl.BlockSpec((PREP_ROWS, LATENT_COLS), lambda i: (i, 0)),
                   pl.BlockSpec((PREP_ROWS, n_gate), lambda i: (i, 0)),
                   rows_spec(MLA_WIDTH), rows_spec(POOL_WIDTH), rows_spec(d)],
        out_shape=[jax.ShapeDtypeStruct((d, LATENT_COLS), BF16),
                   jax.ShapeDtypeStruct((d, n_gate), BF16),
                   jax.ShapeDtypeStruct((MLA_WIDTH, d), BF16),
                   jax.ShapeDtypeStruct((POOL_WIDTH, d), BF16),
                   jax.ShapeDtypeStruct((d, d), BF16)],
        name="weight_prep",
    )(w_in.T, w_branch_attn, w_branch_pool, w_out)
    w_uq_p = jnp.pad(w_uq, ((0, 0), (0, 0), (0, SLAB - NOPE - ROPE)))
    w_uqt = w_uq_p.reshape(Q_RANK, HEADS * SLAB).T.astype(BF16)
    w_uk_p = jnp.pad(w_ukv[:, :, :NOPE], ((0, 0), (0, 0), (0, SLAB - NOPE)))
    w_uk = w_uk_p.reshape(KV_RANK, HEADS * SLAB).astype(BF16)
    w_uvt = w_ukv[:, :, NOPE:].reshape(KV_RANK, MLA_WIDTH).T.astype(BF16)
    cos_t, sin_t = _rope_tables(s)

    row = lambda v: v.reshape(1, -1).astype(F32)
    tok = lambda width: pl.BlockSpec((1, t1, width), lambda bi, si: (bi, si, 0))
    tabt = pl.BlockSpec((ROPE // 2, t1), lambda bi, si: (0, si))
    qt, k, vt, sga, ypool, gates = pl.pallas_call(
        _proj_kernel,
        grid=(b, s // t1),
        in_specs=[tok(d), _const_spec((1, d)), _const_spec(w_a.shape), _const_spec(w_g.shape),
                  _const_spec((1, Q_RANK)), _const_spec(w_uqt.shape), _const_spec((1, KV_RANK)),
                  _const_spec(w_uk.shape), _const_spec(w_uvt.shape),
                  _const_spec(pool_w.shape), _const_spec((1, POOL_WIDTH)),
                  tabt, tabt],
        out_specs=[pl.BlockSpec((1, hp, t1 // tq, HEADS_PER_STEP * SLAB, nq),
                                lambda bi, si: (bi, 0, si, 0, 0)),
                   tok(HEADS * SLAB),
                   pl.BlockSpec((1, hp, t1 // tk, VT_ROWS, tk), lambda bi, si: (bi, 0, si, 0, 0)),
                   tok(MLA_WIDTH), tok(POOL_WIDTH), tok(2 * d)],
        out_shape=[jax.ShapeDtypeStruct((b, hp, s // tq, HEADS_PER_STEP * SLAB, nq), BF16),
                   jax.ShapeDtypeStruct((b, s, HEADS * SLAB), BF16),
                   jax.ShapeDtypeStruct((b, hp, s // tk, VT_ROWS, tk), BF16),
                   jax.ShapeDtypeStruct((b, s, MLA_WIDTH), BF16),
                   jax.ShapeDtypeStruct((b, s, POOL_WIDTH), BF16),
                   jax.ShapeDtypeStruct((b, s, 2 * d), BF16)],
        scratch_shapes=[pltpu.VMEM((HALO, POOL_WIDTH), F32)],
        compiler_params=pltpu.CompilerParams(
            dimension_semantics=("arbitrary", "arbitrary"), vmem_limit_bytes=VMEM_LIMIT),
        name="proj",
    )(x, row(norm_in), w_a, w_g, row(q_norm), w_uqt, row(kv_norm), w_uk, w_uvt,
      pool_w.astype(BF16), row(pool_scale), cos_t, sin_t)

    o = pl.pallas_call(
        _attn_kernel,
        grid=(b, hp // g2),
        in_specs=[pl.BlockSpec((1, g2, s // tq, HEADS_PER_STEP * SLAB, nq), lambda bi, pi: (bi, pi, 0, 0, 0)),
                  pl.BlockSpec((1, s, g2 * HEADS_PER_STEP * SLAB), lambda bi, pi: (bi, 0, pi)),
                  pl.BlockSpec((1, g2, s // tk, VT_ROWS, tk), lambda bi, pi: (bi, pi, 0, 0, 0)),
                  pl.BlockSpec((1, s, g2 * LANES), lambda bi, pi: (bi, 0, pi))],
        out_specs=pl.BlockSpec((1, s, g2 * LANES), lambda bi, pi: (bi, 0, pi)),
        out_shape=jax.ShapeDtypeStruct((b, s, MLA_WIDTH), BF16),
        scratch_shapes=[pltpu.VMEM((g2, tk, nq), F32), pltpu.VMEM((g2, tk, nq), F32),
                        pltpu.VMEM((g2, 1, nq), F32), pltpu.VMEM((g2, 1, nq), F32),
                        pltpu.VMEM((g2, 1, nq), F32), pltpu.VMEM((g2, 1, nq), F32),
                        pltpu.VMEM((g2, HEADS_PER_STEP, VDIM, tq), F32)],
        compiler_params=pltpu.CompilerParams(
            dimension_semantics=("arbitrary", "arbitrary"), vmem_limit_bytes=VMEM_LIMIT),
        name="attn",
    )(qt, k, vt, sga)

    tok3 = lambda width: pl.BlockSpec((1, t3, width), lambda bi, si: (bi, si, 0))
    out = pl.pallas_call(
        _merge_kernel,
        grid=(b, s // t3),
        in_specs=[tok3(d), tok3(MLA_WIDTH), tok3(POOL_WIDTH), tok3(2 * d),
                  _const_spec((MLA_WIDTH, d)), _const_spec((POOL_WIDTH, d)), _const_spec((d, d)),
                  _const_spec((1, d))],
        out_specs=tok3(d),
        out_shape=jax.ShapeDtypeStruct((b, s, d), x.dtype),
        compiler_params=pltpu.CompilerParams(
            dimension_semantics=("arbitrary", "arbitrary"), vmem_limit_bytes=VMEM_LIMIT),
        name="merge",
    )(x, o, ypool, gates, w_ba16, w_bp16, w_o16, row(norm_final))
    return out
```

```python
import jax
import jax.numpy as jnp
from jax import lax
from jax.experimental import pallas as pl
from jax.experimental.pallas import tpu as pltpu

D_MODEL = 1024
CHUNK = 64
HEADS = 8
NOPE = 64
ROPE = 32
VDIM = 64
Q_RANK = 384
KV_RANK = 256
MLA_WIDTH = HEADS * VDIM
ROPE_THETA = 10000.0
POOL_WINDOWS = (2, 4, 8, 16)
POOL_WIDTH = D_MODEL // 2
POOL_GROUP = POOL_WIDTH // len(POOL_WINDOWS)
EPS = 1e-6
QK_SCALE = (NOPE + ROPE) ** -0.5 * 1.4426950408889634

LANES = 128
HALO = 16
SLAB = LANES
LATENT_COLS = Q_RANK + KV_RANK + SLAB
PREP_ROWS = 256
HEADS_PER_STEP = 2
PAIRS_PER_STEP = 2
ONES_ROWS = 16
VH_ROWS = VDIM + ONES_ROWS
VT_ROWS = HEADS_PER_STEP * VH_ROWS
PROJ_TILE = 512
GATE_COLS = 512
ATTN_TQ = 512
ATTN_TK = 512
MERGE_TILE = 1024
VMEM_LIMIT = 56 * 1024 * 1024

BF16 = jnp.bfloat16
F32 = jnp.float32
NT_DIMS = (((1,), (1,)), ((), ()))


def _rms(x, g):
    return x * lax.rsqrt(jnp.mean(x * x, axis=-1, keepdims=True) + EPS) * g


def _sigmoid(x):
    return 1.0 / (1.0 + jnp.exp(-x))


def _weight_prep_kernel(wt_ref, wba_ref, wbp_ref, wo_ref, wa_ref, wg_ref, wba_out, wbp_out, wo_out):
    wba_out[...] = wba_ref[...].astype(BF16)
    wbp_out[...] = wbp_ref[...].astype(BF16)
    wo_out[...] = wo_ref[...].astype(BF16)
    o_kr = Q_RANK + KV_RANK
    wa_ref[:, :o_kr] = wt_ref[:o_kr, :].T.astype(BF16)
    kr = jnp.concatenate([jnp.zeros((NOPE, PREP_ROWS), F32), wt_ref[o_kr:o_kr + ROPE, :],
                          jnp.zeros((SLAB - NOPE - ROPE, PREP_ROWS), F32)], axis=0)
    wa_ref[:, o_kr:] = kr.T.astype(BF16)
    wg_ref[...] = wt_ref[o_kr + ROPE:, :].T.astype(BF16)


def _proj_kernel(x_ref, nin_ref, wa_ref, wg_ref, qn_ref, wuqt_ref, kvn_ref, wuk_ref, wuvt_ref,
                 poolw_ref, pscale_ref, cos_ref, sin_ref,
                 qt_out, k_out, vt_out, sga_out, ypool_out, gates_out, carry_ref):
    t = PROJ_TILE
    si = pl.program_id(1)

    @pl.when(si == 0)
    def _():
        carry_ref[...] = jnp.zeros_like(carry_ref)

    x = x_ref[0]
    hn = _rms(x, nin_ref[...]).astype(BF16)

    za = jnp.dot(hn, wa_ref[...], preferred_element_type=F32)
    zq = za[:, :Q_RANK]
    zkv = za[:, Q_RANK:Q_RANK + KV_RANK]
    zkr = za[:, Q_RANK + KV_RANK:]
    cq = _rms(zq, qn_ref[...]).astype(BF16)
    ckv = _rms(zkv, kvn_ref[...]).astype(BF16)

    qt = lax.dot_general(wuqt_ref[...], cq, NT_DIMS, preferred_element_type=F32)
    scale = QK_SCALE
    cos, sin = cos_ref[...], sin_ref[...]
    qcos, qsin = cos * scale, sin * scale
    half = ROPE // 2
    tq = ATTN_TQ
    for h in range(HEADS):
        r0 = h * SLAB
        x1 = qt[r0 + NOPE:r0 + NOPE + half]
        x2 = qt[r0 + NOPE + half:r0 + NOPE + ROPE]
        slab = jnp.concatenate([qt[r0:r0 + NOPE] * scale, x1 * qcos - x2 * qsin, x1 * qsin + x2 * qcos,
                                jnp.zeros((SLAB - NOPE - ROPE, t), F32)], axis=0).astype(BF16)
        pr, hs = h // HEADS_PER_STEP, h % HEADS_PER_STEP
        for c in range(t // tq):
            qt_out[0, pr, c, hs * SLAB:(hs + 1) * SLAB, :] = slab[:, c * tq:(c + 1) * tq]

    kf = jnp.dot(ckv, wuk_ref[...], preferred_element_type=F32)
    zt = zkr.T
    k1, k2 = zt[NOPE:NOPE + half], zt[NOPE + half:NOPE + ROPE]
    kr = jnp.concatenate([jnp.zeros((NOPE, t), F32), k1 * cos - k2 * sin, k1 * sin + k2 * cos,
                          jnp.zeros((SLAB - NOPE - ROPE, t), F32)], axis=0).T
    for h in range(HEADS):
        sl = slice(h * SLAB, (h + 1) * SLAB)
        k_out[0, :, sl] = (kf[:, sl] + kr).astype(BF16)

    vt = lax.dot_general(wuvt_ref[...], ckv, NT_DIMS, preferred_element_type=F32)
    for c in range(t // ATTN_TK):
        cols = slice(c * ATTN_TK, (c + 1) * ATTN_TK)
        for h in range(HEADS):
            pr, r0 = h // HEADS_PER_STEP, (h % HEADS_PER_STEP) * VH_ROWS
            vt_out[0, pr, c, r0:r0 + VDIM, :] = vt[h * VDIM:(h + 1) * VDIM, cols].astype(BF16)
            vt_out[0, pr, c, r0 + VDIM:r0 + VH_ROWS, :] = jnp.ones((ONES_ROWS, ATTN_TK), BF16)

    zg = jnp.dot(hn, wg_ref[:, :3 * POOL_WIDTH], preferred_element_type=F32)
    g_attn = zg[:, :MLA_WIDTH]
    u = zg[:, MLA_WIDTH:MLA_WIDTH + POOL_WIDTH]
    g_pool = zg[:, MLA_WIDTH + POOL_WIDTH:]
    sga_out[0] = (g_attn * _sigmoid(g_attn)).astype(BF16)
    for c in range(2 * D_MODEL // GATE_COLS):
        lo = 3 * POOL_WIDTH + c * GATE_COLS
        gm = jnp.dot(hn, wg_ref[:, lo:lo + GATE_COLS], preferred_element_type=F32)
        gates_out[0, :, c * GATE_COLS:(c + 1) * GATE_COLS] = _sigmoid(gm).astype(BF16)

    ue = jnp.concatenate([carry_ref[...], u], axis=0)
    carry_ref[...] = u[t - HALO:, :]
    pos = si * t + lax.broadcasted_iota(jnp.int32, (t, 1), 0) + 1
    outs = []
    for gi, w in enumerate(POOL_WINDOWS):
        sl = slice(gi * POOL_GROUP, (gi + 1) * POOL_GROUP)
        acc = ue[:, sl]
        step = 1
        while step < w:
            acc = acc + pltpu.roll(acc, step, axis=0)
            step *= 2
        inv = 1.0 / jnp.minimum(pos, w).astype(F32)
        d = acc[HALO:, :] * inv - u[:, sl]
        outs.append(jnp.dot(d.astype(BF16), poolw_ref[gi], preferred_element_type=F32))
    y = jnp.concatenate(outs, axis=-1) * pscale_ref[...]
    ypool_out[0] = (y * (g_pool * _sigmoid(g_pool))).astype(BF16)


def _attn_kernel(qt_ref, k_ref, vt_ref, sga_ref, o_ref, sa_ref, sb_ref, mxa_ref, mxb_ref,
                 m_ref, l_ref, acc_ref):
    tq, tk = ATTN_TQ, ATTN_TK
    nq = HEADS_PER_STEP * tq
    n_qt = qt_ref.shape[2]
    kblocks = tk // CHUNK
    qry_chunk = lax.shift_right_logical(lax.broadcasted_iota(jnp.int32, (1, 1, nq), 2) & (tq - 1),
                                        CHUNK.bit_length() - 1)
    key_block = lax.broadcasted_iota(jnp.int32, (kblocks, 1, 1), 0)

    def score_tile(s_ref, mx_ref, qi, j, mask_for, diagonal=False):
        split = diagonal and tq == tk
        hk, hq, hb = tk // 2, tq // 2, kblocks // 2
        if mask_for is not None:
            allowed = key_block + (j * kblocks - mask_for * (tq // CHUNK)) <= qry_chunk
            bias = jnp.where(allowed, 0.0, -jnp.inf).astype(F32)
        for g in range(PAIRS_PER_STEP):
            for h in range(HEADS_PER_STEP):
                klanes = slice((g * HEADS_PER_STEP + h) * SLAB, (g * HEADS_PER_STEP + h + 1) * SLAB)
                qrows = slice(h * SLAB, (h + 1) * SLAB)
                cols = slice(h * tq, (h + 1) * tq)
                if not split:
                    kt = k_ref[0, pl.ds(pl.multiple_of(j * tk, tk), tk), klanes]
                    s = jnp.dot(kt, qt_ref[0, g, qi, qrows, :], preferred_element_type=F32)
                    if mask_for is not None:
                        s = (s.reshape(kblocks, CHUNK, tq) + bias[:, :, cols]).reshape(tk, tq)
                    s_ref[g, :, cols] = s
                    mx_ref[g, :, cols] = jnp.max(s, axis=0, keepdims=True)
                    continue
                late = slice(h * tq + hq, (h + 1) * tq)
                k_top = k_ref[0, pl.ds(pl.multiple_of(j * tk, tk), hk), klanes]
                k_bot = k_ref[0, pl.ds(pl.multiple_of(j * tk + hk, hk), hk), klanes]
                s = jnp.dot(k_top, qt_ref[0, g, qi, qrows, :], preferred_element_type=F32)
                s = (s.reshape(hb, CHUNK, tq) + bias[:hb, :, cols]).reshape(hk, tq)
                s_ref[g, :hk, cols] = s
                mx = jnp.max(s, axis=0, keepdims=True)
                s = jnp.dot(k_bot, qt_ref[0, g, qi, qrows, hq:], preferred_element_type=F32)
                s = (s.reshape(hb, CHUNK, hq) + bias[hb:, :, late]).reshape(hk, hq)
                s_ref[g, hk:, late] = s
                mx_ref[g, :, cols] = jnp.concatenate(
                    [mx[:, :hq], jnp.maximum(mx[:, hq:], jnp.max(s, axis=0, keepdims=True))], axis=1)

    def process(s_ref, mx_ref, j, diagonal=False):
        split = diagonal and tq == tk
        hk, hq = tk // 2, tq // 2
        for g in range(PAIRS_PER_STEP):
            m_old = m_ref[g]
            m_new = jnp.maximum(m_old, mx_ref[g])
            p = jnp.exp2((s_ref[g, :hk, :] if split else s_ref[g]) - m_new).astype(BF16)
            alpha = jnp.exp2(m_old - m_new)
            m_ref[g] = m_new
            for h in range(HEADS_PER_STEP):
                cols = slice(h * tq, (h + 1) * tq)
                vrows = slice(h * VH_ROWS, (h + 1) * VH_ROWS)
                if split:
                    late = slice(h * tq + hq, (h + 1) * tq)
                    p_late = jnp.exp2(s_ref[g, hk:, late] - m_new[:, late]).astype(BF16)
                    pv = jnp.dot(vt_ref[0, g, j, vrows, :hk], p[:, cols], preferred_element_type=F32)
                    pv_late = jnp.dot(vt_ref[0, g, j, vrows, hk:], p_late, preferred_element_type=F32)
                    pv = jnp.concatenate([pv[:, :hq], pv[:, hq:] + pv_late], axis=1)
                else:
                    pv = jnp.dot(vt_ref[0, g, j, vrows, :], p[:, cols],
                                 preferred_element_type=F32)
                l_ref[g, :, cols] = alpha[:, cols] * l_ref[g, :, cols] + pv[VDIM:VDIM + 1, :]
                acc_ref[g, h] = alpha[:, cols] * acc_ref[g, h] + pv[:VDIM, :]

    score_tile(sa_ref, mxa_ref, 0, 0, 0, diagonal=True)

    def query_tile(i, carry):
        last = ((i + 1) * tq + tk - 1) // tk - 1
        i_next = jnp.minimum(i + 1, n_qt - 1)
        m_ref[...] = jnp.full_like(m_ref, -jnp.inf)
        l_ref[...] = jnp.zeros_like(l_ref)
        acc_ref[...] = jnp.zeros_like(acc_ref)
        below_mask = None if tq == tk else i

        def pair(j, scores_last):
            score_tile(sb_ref, mxb_ref, i, j + 1, below_mask)
            process(sa_ref, mxa_ref, j)
            score_tile(sa_ref, mxa_ref, i, j + 2, i if scores_last else below_mask, diagonal=scores_last)
            process(sb_ref, mxb_ref, j + 1)

        even = last % 2 == 0
        peel = even & (last >= 2)

        def pair_body(jj, c):
            pair(2 * jj, False)
            return c

        lax.fori_loop(0, last // 2 - jnp.where(peel, 1, 0), pair_body, 0)

        @pl.when(peel)
        def _():
            pair(last - 2, True)

        @pl.when(last % 2 == 1)
        def _():
            score_tile(sb_ref, mxb_ref, i, last, i, diagonal=True)
            process(sa_ref, mxa_ref, last - 1)
            score_tile(sa_ref, mxa_ref, i_next, 0, i_next)
            process(sb_ref, mxb_ref, last, diagonal=True)

        @pl.when(even)
        def _():
            process(sa_ref, mxa_ref, last, diagonal=True)
            score_tile(sa_ref, mxa_ref, i_next, 0, i_next)

        rows = pl.ds(pl.multiple_of(i * tq, tq), tq)
        for g in range(PAIRS_PER_STEP):
            inv_l = 1.0 / l_ref[g]
            ot = jnp.concatenate([acc_ref[g, h] * inv_l[:, h * tq:(h + 1) * tq]
                                  for h in range(HEADS_PER_STEP)], axis=0)
            cols = slice(g * LANES, (g + 1) * LANES)
            o_ref[0, rows, cols] = (ot.T * sga_ref[0, rows, cols].astype(F32)).astype(BF16)
        return carry

    lax.fori_loop(0, n_qt, query_tile, 0)


def _merge_kernel(x_ref, ya_ref, ypool_ref, gates_ref, wa_ref, wp_ref, wo_ref, nf_ref, out_ref):
    a = jnp.dot(ya_ref[0], wa_ref[...], preferred_element_type=F32)
    p = jnp.dot(ypool_ref[0], wp_ref[...], preferred_element_type=F32)
    ga = gates_ref[0, :, :D_MODEL].astype(F32)
    gp = gates_ref[0, :, D_MODEL:].astype(F32)
    merged = (ga * a + gp * p).astype(BF16)
    h = x_ref[0] + jnp.dot(merged, wo_ref[...], preferred_element_type=F32)
    out_ref[0] = _rms(h, nf_ref[...])


def _rope_tables(seq):
    half = ROPE // 2
    inv_freq = ROPE_THETA ** (-jnp.arange(half, dtype=F32) / half)
    ang = jnp.arange(seq, dtype=F32)[None, :] * inv_freq[:, None]
    return jnp.cos(ang), jnp.sin(ang)


def _const_spec(shape):
    return pl.BlockSpec(shape, lambda *_: (0,) * len(shape), pipeline_mode=pl.Buffered(1))


def kernel(x, norm_in, w_in, q_norm, w_uq, kv_norm, w_ukv, pool_w, pool_scale,
           w_branch_attn, w_branch_pool, w_out, norm_final):
    b, s, d = x.shape
    tq, tk, t1, t3 = ATTN_TQ, ATTN_TK, PROJ_TILE, MERGE_TILE
    assert d == D_MODEL and s % t1 == 0 and t1 % tk == 0 and t1 % tq == 0 and s % t3 == 0
    hp = HEADS // HEADS_PER_STEP
    nq = HEADS_PER_STEP * tq
    g2 = PAIRS_PER_STEP

    n_gate = w_in.shape[1] - (Q_RANK + KV_RANK + ROPE)
    n_prep = d // PREP_ROWS
    rows_spec = lambda rows: pl.BlockSpec((rows // n_prep, d), lambda i: (i, 0))
    w_a, w_g, w_ba16, w_bp16, w_o16 = pl.pallas_call(
        _weight_prep_kernel,
        grid=(n_prep,),
        in_specs=[pl.BlockSpec((w_in.shape[1], PREP_ROWS), lambda i: (0, i)),
                  rows_spec(MLA_WIDTH), rows_spec(POOL_WIDTH), rows_spec(d)],
        out_specs=[pl.BlockSpec((PREP_ROWS, LATENT_COLS), lambda i: (i, 0)),
                   pl.BlockSpec((PREP_ROWS, n_gate), lambda i: (i, 0)),
                   rows_spec(MLA_WIDTH), rows_spec(POOL_WIDTH), rows_spec(d)],
        out_shape=[jax.ShapeDtypeStruct((d, LATENT_COLS), BF16),
                   jax.ShapeDtypeStruct((d, n_gate), BF16),
                   jax.ShapeDtypeStruct((MLA_WIDTH, d), BF16),
                   jax.ShapeDtypeStruct((POOL_WIDTH, d), BF16),
                   jax.ShapeDtypeStruct((d, d), BF16)],
        name="weight_prep",
    )(w_in.T, w_branch_attn, w_branch_pool, w_out)
    w_uq_p = jnp.pad(w_uq, ((0, 0), (0, 0), (0, SLAB - NOPE - ROPE)))
    w_uqt = w_uq_p.reshape(Q_RANK, HEADS * SLAB).T.astype(BF16)
    w_uk_p = jnp.pad(w_ukv[:, :, :NOPE], ((0, 0), (0, 0), (0, SLAB - NOPE)))
    w_uk = w_uk_p.reshape(KV_RANK, HEADS * SLAB).astype(BF16)
    w_uvt = w_ukv[:, :, NOPE:].reshape(KV_RANK, MLA_WIDTH).T.astype(BF16)
    cos_t, sin_t = _rope_tables(s)

    row = lambda v: v.reshape(1, -1).astype(F32)
    tok = lambda width: pl.BlockSpec((1, t1, width), lambda bi, si: (bi, si, 0))
    tabt = pl.BlockSpec((ROPE // 2, t1), lambda bi, si: (0, si))
    qt, k, vt, sga, ypool, gates = pl.pallas_call(
        _proj_kernel,
        grid=(b, s // t1),
        in_specs=[tok(d), _const_spec((1, d)), _const_spec(w_a.shape), _const_spec(w_g.shape),
                  _const_spec((1, Q_RANK)), _const_spec(w_uqt.shape), _const_spec((1, KV_RANK)),
                  _const_spec(w_uk.shape), _const_spec(w_uvt.shape),
                  _const_spec(pool_w.shape), _const_spec((1, POOL_WIDTH)),
                  tabt, tabt],
        out_specs=[pl.BlockSpec((1, hp, t1 // tq, HEADS_PER_STEP * SLAB, tq),
                                lambda bi, si: (bi, 0, si, 0, 0)),
                   tok(HEADS * SLAB),
                   pl.BlockSpec((1, hp, t1 // tk, VT_ROWS, tk), lambda bi, si: (bi, 0, si, 0, 0)),
                   tok(MLA_WIDTH), tok(POOL_WIDTH), tok(2 * d)],
        out_shape=[jax.ShapeDtypeStruct((b, hp, s // tq, HEADS_PER_STEP * SLAB, tq), BF16),
                   jax.ShapeDtypeStruct((b, s, HEADS * SLAB), BF16),
                   jax.ShapeDtypeStruct((b, hp, s // tk, VT_ROWS, tk), BF16),
                   jax.ShapeDtypeStruct((b, s, MLA_WIDTH), BF16),
                   jax.ShapeDtypeStruct((b, s, POOL_WIDTH), BF16),
                   jax.ShapeDtypeStruct((b, s, 2 * d), BF16)],
        scratch_shapes=[pltpu.VMEM((HALO, POOL_WIDTH), F32)],
        compiler_params=pltpu.CompilerParams(
            dimension_semantics=("arbitrary", "arbitrary"), vmem_limit_bytes=VMEM_LIMIT),
        name="proj",
    )(x, row(norm_in), w_a, w_g, row(q_norm), w_uqt, row(kv_norm), w_uk, w_uvt,
      pool_w.astype(BF16), row(pool_scale), cos_t, sin_t)

    o = pl.pallas_call(
        _attn_kernel,
        grid=(b, hp // g2),
        in_specs=[pl.BlockSpec((1, g2, s // tq, HEADS_PER_STEP * SLAB, tq), lambda bi, pi: (bi, pi, 0, 0, 0)),
                  pl.BlockSpec((1, s, g2 * HEADS_PER_STEP * SLAB), lambda bi, pi: (bi, 0, pi)),
                  pl.BlockSpec((1, g2, s // tk, VT_ROWS, tk), lambda bi, pi: (bi, pi, 0, 0, 0)),
                  pl.BlockSpec((1, s, g2 * LANES), lambda bi, pi: (bi, 0, pi))],
        out_specs=pl.BlockSpec((1, s, g2 * LANES), lambda bi, pi: (bi, 0, pi)),
        out_shape=jax.ShapeDtypeStruct((b, s, MLA_WIDTH), BF16),
        scratch_shapes=[pltpu.VMEM((g2, tk, nq), F32), pltpu.VMEM((g2, tk, nq), F32),
                        pltpu.VMEM((g2, 1, nq), F32), pltpu.VMEM((g2, 1, nq), F32),
                        pltpu.VMEM((g2, 1, nq), F32), pltpu.VMEM((g2, 1, nq), F32),
                        pltpu.VMEM((g2, HEADS_PER_STEP, VDIM, tq), F32)],
        compiler_params=pltpu.CompilerParams(
            dimension_semantics=("arbitrary", "arbitrary"), vmem_limit_bytes=VMEM_LIMIT),
        name="attn",
    )(qt, k, vt, sga)

    tok3 = lambda width: pl.BlockSpec((1, t3, width), lambda bi, si: (bi, si, 0))
    out = pl.pallas_call(
        _merge_kernel,
        grid=(b, s // t3),
        in_specs=[tok3(d), tok3(MLA_WIDTH), tok3(POOL_WIDTH), tok3(2 * d),
                  _const_spec((MLA_WIDTH, d)), _const_spec((POOL_WIDTH, d)), _const_spec((d, d)),
                  _const_spec((1, d))],
        out_specs=tok3(d),
        out_shape=jax.ShapeDtypeStruct((b, s, d), x.dtype),
        compiler_params=pltpu.CompilerParams(
            dimension_semantics=("arbitrary", "arbitrary"), vmem_limit_bytes=VMEM_LIMIT),
        name="merge",
    )(x, o, ypool, gates, w_ba16, w_bp16, w_o16, row(norm_final))
    return out
```

```python
import jax
import jax.numpy as jnp
from jax import lax
from jax.experimental import pallas as pl
from jax.experimental.pallas import tpu as pltpu

D_MODEL = 1024
CHUNK = 64
HEADS = 8
NOPE = 64
ROPE = 32
VDIM = 64
Q_RANK = 384
KV_RANK = 256
MLA_WIDTH = HEADS * VDIM
ROPE_THETA = 10000.0
POOL_WINDOWS = (2, 4, 8, 16)
POOL_WIDTH = D_MODEL // 2
POOL_GROUP = POOL_WIDTH // len(POOL_WINDOWS)
EPS = 1e-6
QK_SCALE = (NOPE + ROPE) ** -0.5 * 1.4426950408889634

LANES = 128
HALO = 16
SLAB = LANES
LATENT_COLS = Q_RANK + KV_RANK + SLAB
PREP_ROWS = 256
HEADS_PER_STEP = 2
PAIRS_PER_STEP = 2
ONES_ROWS = 16
VH_ROWS = VDIM + ONES_ROWS
VT_ROWS = HEADS_PER_STEP * VH_ROWS
PROJ_TILE = 512
GATE_COLS = 512
ATTN_TQ = 512
ATTN_TK = 512
MERGE_TILE = 1024
VMEM_LIMIT = 56 * 1024 * 1024

BF16 = jnp.bfloat16
F32 = jnp.float32
NT_DIMS = (((1,), (1,)), ((), ()))


def _rms(x, g):
    return x * lax.rsqrt(jnp.mean(x * x, axis=-1, keepdims=True) + EPS) * g


def _sigmoid(x):
    return 1.0 / (1.0 + jnp.exp(-x))


def _weight_prep_kernel(wt_ref, wba_ref, wbp_ref, wo_ref, wa_ref, wg_ref, wba_out, wbp_out, wo_out):
    wba_out[...] = wba_ref[...].astype(BF16)
    wbp_out[...] = wbp_ref[...].astype(BF16)
    wo_out[...] = wo_ref[...].astype(BF16)
    o_kr = Q_RANK + KV_RANK
    wa_ref[:, :o_kr] = wt_ref[:o_kr, :].T.astype(BF16)
    kr = jnp.concatenate([jnp.zeros((NOPE, PREP_ROWS), F32), wt_ref[o_kr:o_kr + ROPE, :],
                          jnp.zeros((SLAB - NOPE - ROPE, PREP_ROWS), F32)], axis=0)
    wa_ref[:, o_kr:] = kr.T.astype(BF16)
    wg_ref[...] = wt_ref[o_kr + ROPE:, :].T.astype(BF16)


def _proj_kernel(x_ref, nin_ref, wa_ref, wg_ref, qn_ref, wuqt_ref, kvn_ref, wuk_ref, wuvt_ref,
                 poolw_ref, pscale_ref, cos_ref, sin_ref,
                 qt_out, k_out, vt_out, sga_out, ypool_out, gates_out, carry_ref):
    t = PROJ_TILE
    si = pl.program_id(1)

    @pl.when(si == 0)
    def _():
        carry_ref[...] = jnp.zeros_like(carry_ref)

    x = x_ref[0]
    hn = _rms(x, nin_ref[...]).astype(BF16)

    za = jnp.dot(hn, wa_ref[...], preferred_element_type=F32)
    zq = za[:, :Q_RANK]
    zkv = za[:, Q_RANK:Q_RANK + KV_RANK]
    zkr = za[:, Q_RANK + KV_RANK:]
    cq = _rms(zq, qn_ref[...]).astype(BF16)
    ckv = _rms(zkv, kvn_ref[...]).astype(BF16)

    qt = lax.dot_general(wuqt_ref[...], cq, NT_DIMS, preferred_element_type=F32)
    scale = QK_SCALE
    cos, sin = cos_ref[...], sin_ref[...]
    qcos, qsin = cos * scale, sin * scale
    half = ROPE // 2
    tq = ATTN_TQ
    for h in range(HEADS):
        r0 = h * SLAB
        x1 = qt[r0 + NOPE:r0 + NOPE + half]
        x2 = qt[r0 + NOPE + half:r0 + NOPE + ROPE]
        slab = jnp.concatenate([qt[r0:r0 + NOPE] * scale, x1 * qcos - x2 * qsin, x1 * qsin + x2 * qcos,
                                jnp.zeros((SLAB - NOPE - ROPE, t), F32)], axis=0).astype(BF16)
        pr, hs = h // HEADS_PER_STEP, h % HEADS_PER_STEP
        for c in range(t // tq):
            qt_out[0, pr, c, hs * SLAB:(hs + 1) * SLAB, :] = slab[:, c * tq:(c + 1) * tq]

    kf = jnp.dot(ckv, wuk_ref[...], preferred_element_type=F32)
    zt = zkr.T
    k1, k2 = zt[NOPE:NOPE + half], zt[NOPE + half:NOPE + ROPE]
    kr = jnp.concatenate([jnp.zeros((NOPE, t), F32), k1 * cos - k2 * sin, k1 * sin + k2 * cos,
                          jnp.zeros((SLAB - NOPE - ROPE, t), F32)], axis=0).T
    for h in range(HEADS):
        sl = slice(h * SLAB, (h + 1) * SLAB)
        k_out[0, :, sl] = (kf[:, sl] + kr).astype(BF16)

    vt = lax.dot_general(wuvt_ref[...], ckv, NT_DIMS, preferred_element_type=F32)
    for c in range(t // ATTN_TK):
        cols = slice(c * ATTN_TK, (c + 1) * ATTN_TK)
        for h in range(HEADS):
            pr, r0 = h // HEADS_PER_STEP, (h % HEADS_PER_STEP) * VH_ROWS
            vt_out[0, pr, c, r0:r0 + VDIM, :] = vt[h * VDIM:(h + 1) * VDIM, cols].astype(BF16)
            vt_out[0, pr, c, r0 + VDIM:r0 + VH_ROWS, :] = jnp.ones((ONES_ROWS, ATTN_TK), BF16)

    zg = jnp.dot(hn, wg_ref[:, :3 * POOL_WIDTH], preferred_element_type=F32)
    g_attn = zg[:, :MLA_WIDTH]
    u = zg[:, MLA_WIDTH:MLA_WIDTH + POOL_WIDTH]
    g_pool = zg[:, MLA_WIDTH + POOL_WIDTH:]
    sga_out[0] = (g_attn * _sigmoid(g_attn)).astype(BF16)
    for c in range(2 * D_MODEL // GATE_COLS):
        lo = 3 * POOL_WIDTH + c * GATE_COLS
        gm = jnp.dot(hn, wg_ref[:, lo:lo + GATE_COLS], preferred_element_type=F32)
        gates_out[0, :, c * GATE_COLS:(c + 1) * GATE_COLS] = _sigmoid(gm).astype(BF16)

    ue = jnp.concatenate([carry_ref[...], u], axis=0)
    carry_ref[...] = u[t - HALO:, :]
    pos = si * t + lax.broadcasted_iota(jnp.int32, (t, 1), 0) + 1
    outs = []
    for gi, w in enumerate(POOL_WINDOWS):
        sl = slice(gi * POOL_GROUP, (gi + 1) * POOL_GROUP)
        acc = ue[:, sl]
        step = 1
        while step < w:
            acc = acc + pltpu.roll(acc, step, axis=0)
            step *= 2
        inv = 1.0 / jnp.minimum(pos, w).astype(F32)
        d = acc[HALO:, :] * inv - u[:, sl]
        outs.append(jnp.dot(d.astype(BF16), poolw_ref[gi], preferred_element_type=F32))
    y = jnp.concatenate(outs, axis=-1) * pscale_ref[...]
    ypool_out[0] = (y * (g_pool * _sigmoid(g_pool))).astype(BF16)


def _attn_kernel(qt_ref, k_ref, vt_ref, sga_ref, o_ref, sa_ref, sb_ref, mxa_ref, mxb_ref,
                 m_ref, l_ref, acc_ref):
    tq, tk = ATTN_TQ, ATTN_TK
    nq = HEADS_PER_STEP * tq
    n_qt = qt_ref.shape[2]
    kblocks = tk // CHUNK
    qry_chunk = lax.shift_right_logical(lax.broadcasted_iota(jnp.int32, (1, 1, nq), 2) & (tq - 1),
                                        CHUNK.bit_length() - 1)
    key_block = lax.broadcasted_iota(jnp.int32, (kblocks, 1, 1), 0)

    def score_tile(s_ref, mx_ref, qi, j, mask_for, diagonal=False):
        split = diagonal and tq == tk
        hk, hq, hb = tk // 2, tq // 2, kblocks // 2
        if mask_for is not None:
            allowed = key_block + (j * kblocks - mask_for * (tq // CHUNK)) <= qry_chunk
            bias = jnp.where(allowed, 0.0, -jnp.inf).astype(F32)
        for g in range(PAIRS_PER_STEP):
            for h in range(HEADS_PER_STEP):
                klanes = slice((g * HEADS_PER_STEP + h) * SLAB, (g * HEADS_PER_STEP + h + 1) * SLAB)
                qrows = slice(h * SLAB, (h + 1) * SLAB)
                cols = slice(h * tq, (h + 1) * tq)
                if not split:
                    kt = k_ref[0, pl.ds(pl.multiple_of(j * tk, tk), tk), klanes]
                    s = jnp.dot(kt, qt_ref[0, g, qi, qrows, :], preferred_element_type=F32)
                    if mask_for is not None:
                        s = (s.reshape(kblocks, CHUNK, tq) + bias[:, :, cols]).reshape(tk, tq)
                    s_ref[g, :, cols] = s
                    mx_ref[g, :, cols] = jnp.max(s, axis=0, keepdims=True)
                    continue
                late = slice(h * tq + hq, (h + 1) * tq)
                k_top = k_ref[0, pl.ds(pl.multiple_of(j * tk, tk), hk), klanes]
                k_bot = k_ref[0, pl.ds(pl.multiple_of(j * tk + hk, hk), hk), klanes]
                s = jnp.dot(k_top, qt_ref[0, g, qi, qrows, :], preferred_element_type=F32)
                s = (s.reshape(hb, CHUNK, tq) + bias[:hb, :, cols]).reshape(hk, tq)
                s_ref[g, :hk, cols] = s
                mx = jnp.max(s, axis=0, keepdims=True)
                s = jnp.dot(k_bot, qt_ref[0, g, qi, qrows, hq:], preferred_element_type=F32)
                s = (s.reshape(hb, CHUNK, hq) + bias[hb:, :, late]).reshape(hk, hq)
                s_ref[g, hk:, late] = s
                mx_ref[g, :, cols] = jnp.concatenate(
                    [mx[:, :hq], jnp.maximum(mx[:, hq:], jnp.max(s, axis=0, keepdims=True))], axis=1)

    def process(s_ref, mx_ref, j, diagonal=False):
        split = diagonal and tq == tk
        hk, hq = tk // 2, tq // 2
        for g in range(PAIRS_PER_STEP):
            m_old = m_ref[g]
            m_new = jnp.maximum(m_old, mx_ref[g])
            p = jnp.exp2((s_ref[g, :hk, :] if split else s_ref[g]) - m_new).astype(BF16)
            alpha = jnp.exp2(m_old - m_new)
            m_ref[g] = m_new
            for h in range(HEADS_PER_STEP):
                cols = slice(h * tq, (h + 1) * tq)
                vrows = slice(h * VH_ROWS, (h + 1) * VH_ROWS)
                if split:
                    late = slice(h * tq + hq, (h + 1) * tq)
                    p_late = jnp.exp2(s_ref[g, hk:, late] - m_new[:, late]).astype(BF16)
                    pv = jnp.dot(vt_ref[0, g, j, vrows, :hk], p[:, cols], preferred_element_type=F32)
                    pv_late = jnp.dot(vt_ref[0, g, j, vrows, hk:], p_late, preferred_element_type=F32)
                    pv = jnp.concatenate([pv[:, :hq], pv[:, hq:] + pv_late], axis=1)
                else:
                    pv = jnp.dot(vt_ref[0, g, j, vrows, :], p[:, cols],
                                 preferred_element_type=F32)
                l_ref[g, :, cols] = alpha[:, cols] * l_ref[g, :, cols] + pv[VDIM:VDIM + 1, :]
                acc_ref[g, h] = alpha[:, cols] * acc_ref[g, h] + pv[:VDIM, :]

    score_tile(sa_ref, mxa_ref, 0, 0, 0, diagonal=True)

    def query_tile(i, carry):
        last = ((i + 1) * tq + tk - 1) // tk - 1
        i_next = jnp.minimum(i + 1, n_qt - 1)
        below_mask = None if tq == tk else i

        def pair(j, scores_last):
            score_tile(sb_ref, mxb_ref, i, j + 1, below_mask)
            process(sa_ref, mxa_ref, j)
            score_tile(sa_ref, mxa_ref, i, j + 2, i if scores_last else below_mask, diagonal=scores_last)
            process(sb_ref, mxb_ref, j + 1)

        even = last % 2 == 0
        peel = even & (last >= 2)

        def pair_body(jj, c):
            pair(2 * jj, False)
            return c

        lax.fori_loop(0, last // 2 - jnp.where(peel, 1, 0), pair_body, 0)

        @pl.when(peel)
        def _():
            pair(last - 2, True)

        @pl.when(last % 2 == 1)
        def _():
            score_tile(sb_ref, mxb_ref, i, last, i, diagonal=True)
            process(sa_ref, mxa_ref, last - 1)
            score_tile(sa_ref, mxa_ref, i_next, 0, i_next)
            process(sb_ref, mxb_ref, last, diagonal=True)
            finish(i)

        @pl.when(even)
        def _():
            process(sa_ref, mxa_ref, last, diagonal=True)
            score_tile(sa_ref, mxa_ref, i_next, 0, i_next)
            finish(i)

        return carry

    def reset():
        m_ref[...] = jnp.full_like(m_ref, -jnp.inf)
        l_ref[...] = jnp.zeros_like(l_ref)
        acc_ref[...] = jnp.zeros_like(acc_ref)

    def finish(i):
        rows = pl.ds(pl.multiple_of(i * tq, tq), tq)
        for g in range(PAIRS_PER_STEP):
            inv_l = 1.0 / l_ref[g]
            ot = jnp.concatenate([acc_ref[g, h] * inv_l[:, h * tq:(h + 1) * tq]
                                  for h in range(HEADS_PER_STEP)], axis=0)
            cols = slice(g * LANES, (g + 1) * LANES)
            o_ref[0, rows, cols] = (ot.T * sga_ref[0, rows, cols].astype(F32)).astype(BF16)
        reset()

    reset()
    lax.fori_loop(0, n_qt, query_tile, 0)


def _merge_kernel(x_ref, ya_ref, ypool_ref, gates_ref, wa_ref, wp_ref, wo_ref, nf_ref, out_ref):
    a = jnp.dot(ya_ref[0], wa_ref[...], preferred_element_type=F32)
    p = jnp.dot(ypool_ref[0], wp_ref[...], preferred_element_type=F32)
    ga = gates_ref[0, :, :D_MODEL].astype(F32)
    gp = gates_ref[0, :, D_MODEL:].astype(F32)
    merged = (ga * a + gp * p).astype(BF16)
    h = x_ref[0] + jnp.dot(merged, wo_ref[...], preferred_element_type=F32)
    out_ref[0] = _rms(h, nf_ref[...])


def _rope_tables(seq):
    half = ROPE // 2
    inv_freq = ROPE_THETA ** (-jnp.arange(half, dtype=F32) / half)
    ang = jnp.arange(seq, dtype=F32)[None, :] * inv_freq[:, None]
    return jnp.cos(ang), jnp.sin(ang)


def _const_spec(shape):
    return pl.BlockSpec(shape, lambda *_: (0,) * len(shape), pipeline_mode=pl.Buffered(1))


def kernel(x, norm_in, w_in, q_norm, w_uq, kv_norm, w_ukv, pool_w, pool_scale,
           w_branch_attn, w_branch_pool, w_out, norm_final):
    b, s, d = x.shape
    tq, tk, t1, t3 = ATTN_TQ, ATTN_TK, PROJ_TILE, MERGE_TILE
    assert d == D_MODEL and s % t1 == 0 and t1 % tk == 0 and t1 % tq == 0 and s % t3 == 0
    hp = HEADS // HEADS_PER_STEP
    nq = HEADS_PER_STEP * tq
    g2 = PAIRS_PER_STEP

    n_gate = w_in.shape[1] - (Q_RANK + KV_RANK + ROPE)
    n_prep = d // PREP_ROWS
    rows_spec = lambda rows: pl.BlockSpec((rows // n_prep, d), lambda i: (i, 0))
    w_a, w_g, w_ba16, w_bp16, w_o16 = pl.pallas_call(
        _weight_prep_kernel,
        grid=(n_prep,),
        in_specs=[pl.BlockSpec((w_in.shape[1], PREP_ROWS), lambda i: (0, i)),
                  rows_spec(MLA_WIDTH), rows_spec(POOL_WIDTH), rows_spec(d)],
        out_specs=[pl.BlockSpec((PREP_ROWS, LATENT_COLS), lambda i: (i, 0)),
                   pl.BlockSpec((PREP_ROWS, n_gate), lambda i: (i, 0)),
                   rows_spec(MLA_WIDTH), rows_spec(POOL_WIDTH), rows_spec(d)],
        out_shape=[jax.ShapeDtypeStruct((d, LATENT_COLS), BF16),
                   jax.ShapeDtypeStruct((d, n_gate), BF16),
                   jax.ShapeDtypeStruct((MLA_WIDTH, d), BF16),
                   jax.ShapeDtypeStruct((POOL_WIDTH, d), BF16),
                   jax.ShapeDtypeStruct((d, d), BF16)],
        name="weight_prep",
    )(w_in.T, w_branch_attn, w_branch_pool, w_out)
    w_uq_p = jnp.pad(w_uq, ((0, 0), (0, 0), (0, SLAB - NOPE - ROPE)))
    w_uqt = w_uq_p.reshape(Q_RANK, HEADS * SLAB).T.astype(BF16)
    w_uk_p = jnp.pad(w_ukv[:, :, :NOPE], ((0, 0), (0, 0), (0, SLAB - NOPE)))
    w_uk = w_uk_p.reshape(KV_RANK, HEADS * SLAB).astype(BF16)
    w_uvt = w_ukv[:, :, NOPE:].reshape(KV_RANK, MLA_WIDTH).T.astype(BF16)
    cos_t, sin_t = _rope_tables(s)

    row = lambda v: v.reshape(1, -1).astype(F32)
    tok = lambda width: pl.BlockSpec((1, t1, width), lambda bi, si: (bi, si, 0))
    tabt = pl.BlockSpec((ROPE // 2, t1), lambda bi, si: (0, si))
    qt, k, vt, sga, ypool, gates = pl.pallas_call(
        _proj_kernel,
        grid=(b, s // t1),
        in_specs=[tok(d), _const_spec((1, d)), _const_spec(w_a.shape), _const_spec(w_g.shape),
                  _const_spec((1, Q_RANK)), _const_spec(w_uqt.shape), _const_spec((1, KV_RANK)),
                  _const_spec(w_uk.shape), _const_spec(w_uvt.shape),
                  _const_spec(pool_w.shape), _const_spec((1, POOL_WIDTH)),
                  tabt, tabt],
        out_specs=[pl.BlockSpec((1, hp, t1 // tq, HEADS_PER_STEP * SLAB, tq),
                                lambda bi, si: (bi, 0, si, 0, 0)),
                   tok(HEADS * SLAB),
                   pl.BlockSpec((1, hp, t1 // tk, VT_ROWS, tk), lambda bi, si: (bi, 0, si, 0, 0)),
                   tok(MLA_WIDTH), tok(POOL_WIDTH), tok(2 * d)],
        out_shape=[jax.ShapeDtypeStruct((b, hp, s // tq, HEADS_PER_STEP * SLAB, tq), BF16),
                   jax.ShapeDtypeStruct((b, s, HEADS * SLAB), BF16),
                   jax.ShapeDtypeStruct((b, hp, s // tk, VT_ROWS, tk), BF16),
                   jax.ShapeDtypeStruct((b, s, MLA_WIDTH), BF16),
                   jax.ShapeDtypeStruct((b, s, POOL_WIDTH), BF16),
                   jax.ShapeDtypeStruct((b, s, 2 * d), BF16)],
        scratch_shapes=[pltpu.VMEM((HALO, POOL_WIDTH), F32)],
        compiler_params=pltpu.CompilerParams(
            dimension_semantics=("arbitrary", "arbitrary"), vmem_limit_bytes=VMEM_LIMIT),
        name="proj",
    )(x, row(norm_in), w_a, w_g, row(q_norm), w_uqt, row(kv_norm), w_uk, w_uvt,
      pool_w.astype(BF16), row(pool_scale), cos_t, sin_t)

    o = pl.pallas_call(
        _attn_kernel,
        grid=(b, hp // g2),
        in_specs=[pl.BlockSpec((1, g2, s // tq, HEADS_PER_STEP * SLAB, tq), lambda bi, pi: (bi, pi, 0, 0, 0)),
                  pl.BlockSpec((1, s, g2 * HEADS_PER_STEP * SLAB), lambda bi, pi: (bi, 0, pi)),
                  pl.BlockSpec((1, g2, s // tk, VT_ROWS, tk), lambda bi, pi: (bi, pi, 0, 0, 0)),
                  pl.BlockSpec((1, s, g2 * LANES), lambda bi, pi: (bi, 0, pi))],
        out_specs=pl.BlockSpec((1, s, g2 * LANES), lambda bi, pi: (bi, 0, pi)),
        out_shape=jax.ShapeDtypeStruct((b, s, MLA_WIDTH), BF16),
        scratch_shapes=[pltpu.VMEM((g2, tk, nq), F32), pltpu.VMEM((g2, tk, nq), F32),
                        pltpu.VMEM((g2, 1, nq), F32), pltpu.VMEM((g2, 1, nq), F32),
                        pltpu.VMEM((g2, 1, nq), F32), pltpu.VMEM((g2, 1, nq), F32),
                        pltpu.VMEM((g2, HEADS_PER_STEP, VDIM, tq), F32)],
        compiler_params=pltpu.CompilerParams(
            dimension_semantics=("arbitrary", "arbitrary"), vmem_limit_bytes=VMEM_LIMIT),
        name="attn",
    )(qt, k, vt, sga)

    tok3 = lambda width: pl.BlockSpec((1, t3, width), lambda bi, si: (bi, si, 0))
    out = pl.pallas_call(
        _merge_kernel,
        grid=(b, s // t3),
        in_specs=[tok3(d), tok3(MLA_WIDTH), tok3(POOL_WIDTH), tok3(2 * d),
                  _const_spec((MLA_WIDTH, d)), _const_spec((POOL_WIDTH, d)), _const_spec((d, d)),
                  _const_spec((1, d))],
        out_specs=tok3(d),
        out_shape=jax.ShapeDtypeStruct((b, s, d), x.dtype),
        compiler_params=pltpu.CompilerParams(
            dimension_semantics=("arbitrary", "arbitrary"), vmem_limit_bytes=VMEM_LIMIT),
        name="merge",
    )(x, o, ypool, gates, w_ba16, w_bp16, w_o16, row(norm_final))
    return out
```

```python
import jax
import jax.numpy as jnp
from jax import lax
from jax.experimental import pallas as pl
from jax.experimental.pallas import tpu as pltpu

D_MODEL = 1024
CHUNK = 64
HEADS = 8
NOPE = 64
ROPE = 32
VDIM = 64
Q_RANK = 384
KV_RANK = 256
MLA_WIDTH = HEADS * VDIM
ROPE_THETA = 10000.0
POOL_WINDOWS = (2, 4, 8, 16)
POOL_WIDTH = D_MODEL // 2
POOL_GROUP = POOL_WIDTH // len(POOL_WINDOWS)
EPS = 1e-6
QK_SCALE = (NOPE + ROPE) ** -0.5 * 1.4426950408889634

LANES = 128
HALO = 16
SLAB = LANES
LATENT_COLS = Q_RANK + KV_RANK + SLAB
PREP_ROWS = 256
HEADS_PER_STEP = 2
PAIRS_PER_STEP = 2
ONES_ROWS = 16
VH_ROWS = VDIM + ONES_ROWS
VT_ROWS = HEADS_PER_STEP * VH_ROWS
PROJ_TILE = 512
GATE_COLS = 512
ATTN_TQ = 512
ATTN_TK = 512
MERGE_TILE = 1024
VMEM_LIMIT = 56 * 1024 * 1024

BF16 = jnp.bfloat16
F32 = jnp.float32
NT_DIMS = (((1,), (1,)), ((), ()))


def _rms(x, g):
    return x * lax.rsqrt(jnp.mean(x * x, axis=-1, keepdims=True) + EPS) * g


def _sigmoid(x):
    return 1.0 / (1.0 + jnp.exp(-x))


def _weight_prep_kernel(wt_ref, wba_ref, wbp_ref, wo_ref, wa_ref, wg_ref, wba_out, wbp_out, wo_out):
    wba_out[...] = wba_ref[...].astype(BF16)
    wbp_out[...] = wbp_ref[...].astype(BF16)
    wo_out[...] = wo_ref[...].astype(BF16)
    o_kr = Q_RANK + KV_RANK
    wa_ref[:, :o_kr] = wt_ref[:o_kr, :].T.astype(BF16)
    kr = jnp.concatenate([jnp.zeros((NOPE, PREP_ROWS), F32), wt_ref[o_kr:o_kr + ROPE, :],
                          jnp.zeros((SLAB - NOPE - ROPE, PREP_ROWS), F32)], axis=0)
    wa_ref[:, o_kr:] = kr.T.astype(BF16)
    wg_ref[...] = wt_ref[o_kr + ROPE:, :].T.astype(BF16)


def _proj_kernel(x_ref, nin_ref, wa_ref, wg_ref, qn_ref, wuqt_ref, kvn_ref, wuk_ref, wuvt_ref,
                 poolw_ref, pscale_ref, cos_ref, sin_ref,
                 qt_out, k_out, vt_out, sga_out, ypool_out, gates_out, carry_ref):
    t = PROJ_TILE
    si = pl.program_id(1)

    @pl.when(si == 0)
    def _():
        carry_ref[...] = jnp.zeros_like(carry_ref)

    x = x_ref[0]
    hn = _rms(x, nin_ref[...]).astype(BF16)

    za = jnp.dot(hn, wa_ref[...], preferred_element_type=F32)
    zq = za[:, :Q_RANK]
    zkv = za[:, Q_RANK:Q_RANK + KV_RANK]
    zkr = za[:, Q_RANK + KV_RANK:]
    cq = _rms(zq, qn_ref[...]).astype(BF16)
    ckv = _rms(zkv, kvn_ref[...]).astype(BF16)

    qt = lax.dot_general(wuqt_ref[...], cq, NT_DIMS, preferred_element_type=F32)
    scale = QK_SCALE
    cos, sin = cos_ref[...], sin_ref[...]
    qcos, qsin = cos * scale, sin * scale
    half = ROPE // 2
    tq = ATTN_TQ
    for h in range(HEADS):
        r0 = h * SLAB
        x1 = qt[r0 + NOPE:r0 + NOPE + half]
        x2 = qt[r0 + NOPE + half:r0 + NOPE + ROPE]
        slab = jnp.concatenate([qt[r0:r0 + NOPE] * scale, x1 * qcos - x2 * qsin, x1 * qsin + x2 * qcos,
                                jnp.zeros((SLAB - NOPE - ROPE, t), F32)], axis=0).astype(BF16)
        pr, hs = h // HEADS_PER_STEP, h % HEADS_PER_STEP
        for c in range(t // tq):
            qt_out[0, pr, c, hs * SLAB:(hs + 1) * SLAB, :] = slab[:, c * tq:(c + 1) * tq]

    kf = jnp.dot(ckv, wuk_ref[...], preferred_element_type=F32)
    zt = zkr.T
    k1, k2 = zt[NOPE:NOPE + half], zt[NOPE + half:NOPE + ROPE]
    kr = jnp.concatenate([jnp.zeros((NOPE, t), F32), k1 * cos - k2 * sin, k1 * sin + k2 * cos,
                          jnp.zeros((SLAB - NOPE - ROPE, t), F32)], axis=0).T
    for h in range(HEADS):
        sl = slice(h * SLAB, (h + 1) * SLAB)
        k_out[0, :, sl] = (kf[:, sl] + kr).astype(BF16)

    vt = lax.dot_general(wuvt_ref[...], ckv, NT_DIMS, preferred_element_type=F32)
    for c in range(t // ATTN_TK):
        cols = slice(c * ATTN_TK, (c + 1) * ATTN_TK)
        for h in range(HEADS):
            pr, r0 = h // HEADS_PER_STEP, (h % HEADS_PER_STEP) * VH_ROWS
            vt_out[0, pr, c, r0:r0 + VDIM, :] = vt[h * VDIM:(h + 1) * VDIM, cols].astype(BF16)
            vt_out[0, pr, c, r0 + VDIM:r0 + VH_ROWS, :] = jnp.ones((ONES_ROWS, ATTN_TK), BF16)

    zg = jnp.dot(hn, wg_ref[:, :3 * POOL_WIDTH], preferred_element_type=F32)
    g_attn = zg[:, :MLA_WIDTH]
    u = zg[:, MLA_WIDTH:MLA_WIDTH + POOL_WIDTH]
    g_pool = zg[:, MLA_WIDTH + POOL_WIDTH:]
    sga_out[0] = (g_attn * _sigmoid(g_attn)).astype(BF16)
    for c in range(2 * D_MODEL // GATE_COLS):
        lo = 3 * POOL_WIDTH + c * GATE_COLS
        gm = jnp.dot(hn, wg_ref[:, lo:lo + GATE_COLS], preferred_element_type=F32)
        gates_out[0, :, c * GATE_COLS:(c + 1) * GATE_COLS] = _sigmoid(gm).astype(BF16)

    ue = jnp.concatenate([carry_ref[...], u], axis=0)
    carry_ref[...] = u[t - HALO:, :]
    pos = si * t + lax.broadcasted_iota(jnp.int32, (t, 1), 0) + 1
    outs = []
    for gi, w in enumerate(POOL_WINDOWS):
        sl = slice(gi * POOL_GROUP, (gi + 1) * POOL_GROUP)
        acc = ue[:, sl]
        step = 1
        while step < w:
            acc = acc + pltpu.roll(acc, step, axis=0)
            step *= 2
        inv = 1.0 / jnp.minimum(pos, w).astype(F32)
        d = acc[HALO:, :] * inv - u[:, sl]
        outs.append(jnp.dot(d.astype(BF16), poolw_ref[gi], preferred_element_type=F32))
    y = jnp.concatenate(outs, axis=-1) * pscale_ref[...]
    ypool_out[0] = (y * (g_pool * _sigmoid(g_pool))).astype(BF16)


def _attn_kernel(qt_ref, k_ref, vt_ref, sga_ref, o_ref, sa_ref, sb_ref, mxa_ref, mxb_ref,
                 m_ref, l_ref, acc_ref):
    tq, tk = ATTN_TQ, ATTN_TK
    nq = HEADS_PER_STEP * tq
    n_qt = qt_ref.shape[2]
    kblocks = tk // CHUNK
    qry_chunk = lax.shift_right_logical(lax.broadcasted_iota(jnp.int32, (1, 1, nq), 2) & (tq - 1),
                                        CHUNK.bit_length() - 1)
    key_block = lax.broadcasted_iota(jnp.int32, (kblocks, 1, 1), 0)

    def score_tile(s_ref, mx_ref, qi, j, mask_for, diagonal=False):
        split = diagonal and tq == tk
        hk, hq, hb = tk // 2, tq // 2, kblocks // 2
        if mask_for is not None:
            allowed = key_block + (j * kblocks - mask_for * (tq // CHUNK)) <= qry_chunk
            bias = jnp.where(allowed, 0.0, -jnp.inf).astype(F32)
        for g in range(PAIRS_PER_STEP):
            for h in range(HEADS_PER_STEP):
                klanes = slice((g * HEADS_PER_STEP + h) * SLAB, (g * HEADS_PER_STEP + h + 1) * SLAB)
                qrows = slice(h * SLAB, (h + 1) * SLAB)
                cols = slice(h * tq, (h + 1) * tq)
                if not split:
                    kt = k_ref[0, pl.ds(pl.multiple_of(j * tk, tk), tk), klanes]
                    s = jnp.dot(kt, qt_ref[0, g, qi, qrows, :], preferred_element_type=F32)
                    if mask_for is not None:
                        s = (s.reshape(kblocks, CHUNK, tq) + bias[:, :, cols]).reshape(tk, tq)
                    s_ref[g, :, cols] = s
                    mx_ref[g, :, cols] = jnp.max(s, axis=0, keepdims=True)
                    continue
                late = slice(h * tq + hq, (h + 1) * tq)
                k_top = k_ref[0, pl.ds(pl.multiple_of(j * tk, tk), hk), klanes]
                k_bot = k_ref[0, pl.ds(pl.multiple_of(j * tk + hk, hk), hk), klanes]
                s = jnp.dot(k_top, qt_ref[0, g, qi, qrows, :], preferred_element_type=F32)
                s = (s.reshape(hb, CHUNK, tq) + bias[:hb, :, cols]).reshape(hk, tq)
                s_ref[g, :hk, cols] = s
                mx = jnp.max(s, axis=0, keepdims=True)
                s = jnp.dot(k_bot, qt_ref[0, g, qi, qrows, hq:], preferred_element_type=F32)
                s = (s.reshape(hb, CHUNK, hq) + bias[hb:, :, late]).reshape(hk, hq)
                s_ref[g, hk:, late] = s
                mx_ref[g, :, cols] = jnp.concatenate(
                    [mx[:, :hq], jnp.maximum(mx[:, hq:], jnp.max(s, axis=0, keepdims=True))], axis=1)

    def process(s_ref, mx_ref, j, diagonal=False):
        split = diagonal and tq == tk
        hk, hq = tk // 2, tq // 2
        for g in range(PAIRS_PER_STEP):
            m_old = m_ref[g]
            m_new = jnp.maximum(m_old, mx_ref[g])
            p = jnp.exp2((s_ref[g, :hk, :] if split else s_ref[g]) - m_new).astype(BF16)
            alpha = jnp.exp2(m_old - m_new)
            m_ref[g] = m_new
            for h in range(HEADS_PER_STEP):
                cols = slice(h * tq, (h + 1) * tq)
                vrows = slice(h * VH_ROWS, (h + 1) * VH_ROWS)
                if split:
                    late = slice(h * tq + hq, (h + 1) * tq)
                    p_late = jnp.exp2(s_ref[g, hk:, late] - m_new[:, late]).astype(BF16)
                    pv = jnp.dot(vt_ref[0, g, j, vrows, :hk], p[:, cols], preferred_element_type=F32)
                    pv_late = jnp.dot(vt_ref[0, g, j, vrows, hk:], p_late, preferred_element_type=F32)
                    pv = jnp.concatenate([pv[:, :hq], pv[:, hq:] + pv_late], axis=1)
                else:
                    pv = jnp.dot(vt_ref[0, g, j, vrows, :], p[:, cols],
                                 preferred_element_type=F32)
                l_ref[g, :, cols] = alpha[:, cols] * l_ref[g, :, cols] + pv[VDIM:VDIM + 1, :]
                acc_ref[g, h] = alpha[:, cols] * acc_ref[g, h] + pv[:VDIM, :]

    score_tile(sa_ref, mxa_ref, 0, 0, 0, diagonal=True)

    def query_tile(i, carry):
        last = ((i + 1) * tq + tk - 1) // tk - 1
        i_next = jnp.minimum(i + 1, n_qt - 1)
        below_mask = None if tq == tk else i

        def pair(j, scores_last):
            score_tile(sb_ref, mxb_ref, i, j + 1, below_mask)
            process(sa_ref, mxa_ref, j)
            score_tile(sa_ref, mxa_ref, i, j + 2, i if scores_last else below_mask, diagonal=scores_last)
            process(sb_ref, mxb_ref, j + 1)

        even = last % 2 == 0
        peel = last >= 2

        def pair_body(jj, c):
            pair(2 * jj, False)
            return c

        lax.fori_loop(0, last // 2 - jnp.where(peel, 1, 0), pair_body, 0)

        def close_even():
            process(sa_ref, mxa_ref, last, diagonal=True)
            score_tile(sa_ref, mxa_ref, i_next, 0, i_next)
            finish(i)

        @pl.when(even & peel)
        def _():
            pair(jnp.maximum(last - 2, 0), True)
            close_even()

        def close_odd():
            score_tile(sb_ref, mxb_ref, i, last, i, diagonal=True)
            process(sa_ref, mxa_ref, jnp.maximum(last - 1, 0))
            score_tile(sa_ref, mxa_ref, i_next, 0, i_next)
            process(sb_ref, mxb_ref, last, diagonal=True)
            finish(i)

        @pl.when(~even & peel)
        def _():
            pair(jnp.maximum(last - 3, 0), False)
            close_odd()

        @pl.when(~even & ~peel)
        def _():
            close_odd()

        @pl.when(even & ~peel)
        def _():
            close_even()

        return carry

    def reset():
        m_ref[...] = jnp.full_like(m_ref, -jnp.inf)
        l_ref[...] = jnp.zeros_like(l_ref)
        acc_ref[...] = jnp.zeros_like(acc_ref)

    def finish(i):
        rows = pl.ds(pl.multiple_of(i * tq, tq), tq)
        for g in range(PAIRS_PER_STEP):
            inv_l = 1.0 / l_ref[g]
            ot = jnp.concatenate([acc_ref[g, h] * inv_l[:, h * tq:(h + 1) * tq]
                                  for h in range(HEADS_PER_STEP)], axis=0)
            cols = slice(g * LANES, (g + 1) * LANES)
            o_ref[0, rows, cols] = (ot.T * sga_ref[0, rows, cols].astype(F32)).astype(BF16)
        reset()

    reset()
    lax.fori_loop(0, n_qt, query_tile, 0)


def _merge_kernel(x_ref, ya_ref, ypool_ref, gates_ref, wa_ref, wp_ref, wo_ref, nf_ref, out_ref):
    a = jnp.dot(ya_ref[0], wa_ref[...], preferred_element_type=F32)
    p = jnp.dot(ypool_ref[0], wp_ref[...], preferred_element_type=F32)
    ga = gates_ref[0, :, :D_MODEL].astype(F32)
    gp = gates_ref[0, :, D_MODEL:].astype(F32)
    merged = (ga * a + gp * p).astype(BF16)
    h = x_ref[0] + jnp.dot(merged, wo_ref[...], preferred_element_type=F32)
    out_ref[0] = _rms(h, nf_ref[...])


def _rope_tables(seq):
    half = ROPE // 2
    inv_freq = ROPE_THETA ** (-jnp.arange(half, dtype=F32) / half)
    ang = jnp.arange(seq, dtype=F32)[None, :] * inv_freq[:, None]
    return jnp.cos(ang), jnp.sin(ang)


def _const_spec(shape):
    return pl.BlockSpec(shape, lambda *_: (0,) * len(shape), pipeline_mode=pl.Buffered(1))


def kernel(x, norm_in, w_in, q_norm, w_uq, kv_norm, w_ukv, pool_w, pool_scale,
           w_branch_attn, w_branch_pool, w_out, norm_final):
    b, s, d = x.shape
    tq, tk, t1, t3 = ATTN_TQ, ATTN_TK, PROJ_TILE, MERGE_TILE
    assert d == D_MODEL and s % t1 == 0 and t1 % tk == 0 and t1 % tq == 0 and s % t3 == 0
    hp = HEADS // HEADS_PER_STEP
    nq = HEADS_PER_STEP * tq
    g2 = PAIRS_PER_STEP

    n_gate = w_in.shape[1] - (Q_RANK + KV_RANK + ROPE)
    n_prep = d // PREP_ROWS
    rows_spec = lambda rows: pl.BlockSpec((rows // n_prep, d), lambda i: (i, 0))
    w_a, w_g, w_ba16, w_bp16, w_o16 = pl.pallas_call(
        _weight_prep_kernel,
        grid=(n_prep,),
        in_specs=[pl.BlockSpec((w_in.shape[1], PREP_ROWS), lambda i: (0, i)),
                  rows_spec(MLA_WIDTH), rows_spec(POOL_WIDTH), rows_spec(d)],
        out_specs=[pl.BlockSpec((PREP_ROWS, LATENT_COLS), lambda i: (i, 0)),
                   pl.BlockSpec((PREP_ROWS, n_gate), lambda i: (i, 0)),
                   rows_spec(MLA_WIDTH), rows_spec(POOL_WIDTH), rows_spec(d)],
        out_shape=[jax.ShapeDtypeStruct((d, LATENT_COLS), BF16),
                   jax.ShapeDtypeStruct((d, n_gate), BF16),
                   jax.ShapeDtypeStruct((MLA_WIDTH, d), BF16),
                   jax.ShapeDtypeStruct((POOL_WIDTH, d), BF16),
                   jax.ShapeDtypeStruct((d, d), BF16)],
        name="weight_prep",
    )(w_in.T, w_branch_attn, w_branch_pool, w_out)
    w_uq_p = jnp.pad(w_uq, ((0, 0), (0, 0), (0, SLAB - NOPE - ROPE)))
    w_uqt = w_uq_p.reshape(Q_RANK, HEADS * SLAB).T.astype(BF16)
    w_uk_p = jnp.pad(w_ukv[:, :, :NOPE], ((0, 0), (0, 0), (0, SLAB - NOPE)))
    w_uk = w_uk_p.reshape(KV_RANK, HEADS * SLAB).astype(BF16)
    w_uvt = w_ukv[:, :, NOPE:].reshape(KV_RANK, MLA_WIDTH).T.astype(BF16)
    cos_t, sin_t = _rope_tables(s)

    row = lambda v: v.reshape(1, -1).astype(F32)
    tok = lambda width: pl.BlockSpec((1, t1, width), lambda bi, si: (bi, si, 0))
    tabt = pl.BlockSpec((ROPE // 2, t1), lambda bi, si: (0, si))
    qt, k, vt, sga, ypool, gates = pl.pallas_call(
        _proj_kernel,
        grid=(b, s // t1),
        in_specs=[tok(d), _const_spec((1, d)), _const_spec(w_a.shape), _const_spec(w_g.shape),
                  _const_spec((1, Q_RANK)), _const_spec(w_uqt.shape), _const_spec((1, KV_RANK)),
                  _const_spec(w_uk.shape), _const_spec(w_uvt.shape),
                  _const_spec(pool_w.shape), _const_spec((1, POOL_WIDTH)),
                  tabt, tabt],
        out_specs=[pl.BlockSpec((1, hp, t1 // tq, HEADS_PER_STEP * SLAB, tq),
                                lambda bi, si: (bi, 0, si, 0, 0)),
                   tok(HEADS * SLAB),
                   pl.BlockSpec((1, hp, t1 // tk, VT_ROWS, tk), lambda bi, si: (bi, 0, si, 0, 0)),
                   tok(MLA_WIDTH), tok(POOL_WIDTH), tok(2 * d)],
        out_shape=[jax.ShapeDtypeStruct((b, hp, s // tq, HEADS_PER_STEP * SLAB, tq), BF16),
                   jax.ShapeDtypeStruct((b, s, HEADS * SLAB), BF16),
                   jax.ShapeDtypeStruct((b, hp, s // tk, VT_ROWS, tk), BF16),
                   jax.ShapeDtypeStruct((b, s, MLA_WIDTH), BF16),
                   jax.ShapeDtypeStruct((b, s, POOL_WIDTH), BF16),
                   jax.ShapeDtypeStruct((b, s, 2 * d), BF16)],
        scratch_shapes=[pltpu.VMEM((HALO, POOL_WIDTH), F32)],
        compiler_params=pltpu.CompilerParams(
            dimension_semantics=("arbitrary", "arbitrary"), vmem_limit_bytes=VMEM_LIMIT),
        name="proj",
    )(x, row(norm_in), w_a, w_g, row(q_norm), w_uqt, row(kv_norm), w_uk, w_uvt,
      pool_w.astype(BF16), row(pool_scale), cos_t, sin_t)

    o = pl.pallas_call(
        _attn_kernel,
        grid=(b, hp // g2),
        in_specs=[pl.BlockSpec((1, g2, s // tq, HEADS_PER_STEP * SLAB, tq), lambda bi, pi: (bi, pi, 0, 0, 0)),
                  pl.BlockSpec((1, s, g2 * HEADS_PER_STEP * SLAB), lambda bi, pi: (bi, 0, pi)),
                  pl.BlockSpec((1, g2, s // tk, VT_ROWS, tk), lambda bi, pi: (bi, pi, 0, 0, 0)),
                  pl.BlockSpec((1, s, g2 * LANES), lambda bi, pi: (bi, 0, pi))],
        out_specs=pl.BlockSpec((1, s, g2 * LANES), lambda bi, pi: (bi, 0, pi)),
        out_shape=jax.ShapeDtypeStruct((b, s, MLA_WIDTH), BF16),
        scratch_shapes=[pltpu.VMEM((g2, tk, nq), F32), pltpu.VMEM((g2, tk, nq), F32),
                        pltpu.VMEM((g2, 1, nq), F32), pltpu.VMEM((g2, 1, nq), F32),
                        pltpu.VMEM((g2, 1, nq), F32), pltpu.VMEM((g2, 1, nq), F32),
                        pltpu.VMEM((g2, HEADS_PER_STEP, VDIM, tq), F32)],
        compiler_params=pltpu.CompilerParams(
            dimension_semantics=("arbitrary", "arbitrary"), vmem_limit_bytes=VMEM_LIMIT),
        name="attn",
    )(qt, k, vt, sga)

    tok3 = lambda width: pl.BlockSpec((1, t3, width), lambda bi, si: (bi, si, 0))
    out = pl.pallas_call(
        _merge_kernel,
        grid=(b, s // t3),
        in_specs=[tok3(d), tok3(MLA_WIDTH), tok3(POOL_WIDTH), tok3(2 * d),
                  _const_spec((MLA_WIDTH, d)), _const_spec((POOL_WIDTH, d)), _const_spec((d, d)),
                  _const_spec((1, d))],
        out_specs=tok3(d),
        out_shape=jax.ShapeDtypeStruct((b, s, d), x.dtype),
        compiler_params=pltpu.CompilerParams(
            dimension_semantics=("arbitrary", "arbitrary"), vmem_limit_bytes=VMEM_LIMIT),
        name="merge",
    )(x, o, ypool, gates, w_ba16, w_bp16, w_o16, row(norm_final))
    return out
```

```python
import functools

import jax
import jax.numpy as jnp
from jax import lax
from jax.experimental import pallas as pl
from jax.experimental.pallas import tpu as pltpu

D_MODEL = 1024
CHUNK = 64
HEADS = 8
NOPE = 64
ROPE = 32
VDIM = 64
Q_RANK = 384
KV_RANK = 256
MLA_WIDTH = HEADS * VDIM
ROPE_THETA = 10000.0
POOL_WINDOWS = (2, 4, 8, 16)
POOL_WIDTH = D_MODEL // 2
POOL_GROUP = POOL_WIDTH // len(POOL_WINDOWS)
EPS = 1e-6
QK_SCALE = (NOPE + ROPE) ** -0.5 * 1.4426950408889634

LANES = 128
HALO = 16
SLAB = LANES
LATENT_COLS = Q_RANK + KV_RANK + SLAB
PREP_ROWS = 256
HEADS_PER_STEP = 2
PAIRS_PER_STEP = 2
ONES_ROWS = 16
VH_ROWS = VDIM + ONES_ROWS
VT_ROWS = HEADS_PER_STEP * VH_ROWS
PROJ_TILE = 512
GATE_COLS = 512
ATTN_TQ = 512
ATTN_TK = 512
CLOSING_PAIRS = 2
MERGE_TILE = 1024
VMEM_LIMIT = 56 * 1024 * 1024

BF16 = jnp.bfloat16
F32 = jnp.float32
NT_DIMS = (((1,), (1,)), ((), ()))


def _rms(x, g):
    return x * lax.rsqrt(jnp.mean(x * x, axis=-1, keepdims=True) + EPS) * g


def _sigmoid(x):
    return 1.0 / (1.0 + jnp.exp(-x))


def _weight_prep_kernel(wt_ref, wba_ref, wbp_ref, wo_ref, wa_ref, wg_ref, wba_out, wbp_out, wo_out):
    wba_out[...] = wba_ref[...].astype(BF16)
    wbp_out[...] = wbp_ref[...].astype(BF16)
    wo_out[...] = wo_ref[...].astype(BF16)
    o_kr = Q_RANK + KV_RANK
    wa_ref[:, :o_kr] = wt_ref[:o_kr, :].T.astype(BF16)
    kr = jnp.concatenate([jnp.zeros((NOPE, PREP_ROWS), F32), wt_ref[o_kr:o_kr + ROPE, :],
                          jnp.zeros((SLAB - NOPE - ROPE, PREP_ROWS), F32)], axis=0)
    wa_ref[:, o_kr:] = kr.T.astype(BF16)
    wg_ref[...] = wt_ref[o_kr + ROPE:, :].T.astype(BF16)


def _proj_kernel(x_ref, nin_ref, wa_ref, wg_ref, qn_ref, wuqt_ref, kvn_ref, wuk_ref, wuvt_ref,
                 poolw_ref, pscale_ref, cos_ref, sin_ref,
                 qt_out, k_out, vt_out, sga_out, ypool_out, gates_out, carry_ref):
    t = PROJ_TILE
    si = pl.program_id(1)

    @pl.when(si == 0)
    def _():
        carry_ref[...] = jnp.zeros_like(carry_ref)

    x = x_ref[0]
    hn = _rms(x, nin_ref[...]).astype(BF16)

    za = jnp.dot(hn, wa_ref[...], preferred_element_type=F32)
    zq = za[:, :Q_RANK]
    zkv = za[:, Q_RANK:Q_RANK + KV_RANK]
    zkr = za[:, Q_RANK + KV_RANK:]
    cq = _rms(zq, qn_ref[...]).astype(BF16)
    ckv = _rms(zkv, kvn_ref[...]).astype(BF16)

    qt = lax.dot_general(wuqt_ref[...], cq, NT_DIMS, preferred_element_type=F32)
    scale = QK_SCALE
    cos, sin = cos_ref[...], sin_ref[...]
    qcos, qsin = cos * scale, sin * scale
    half = ROPE // 2
    tq = ATTN_TQ
    for h in range(HEADS):
        r0 = h * SLAB
        x1 = qt[r0 + NOPE:r0 + NOPE + half]
        x2 = qt[r0 + NOPE + half:r0 + NOPE + ROPE]
        slab = jnp.concatenate([qt[r0:r0 + NOPE] * scale, x1 * qcos - x2 * qsin, x1 * qsin + x2 * qcos,
                                jnp.zeros((SLAB - NOPE - ROPE, t), F32)], axis=0).astype(BF16)
        pr, hs = h // HEADS_PER_STEP, h % HEADS_PER_STEP
        for c in range(t // tq):
            qt_out[0, pr, c, hs * SLAB:(hs + 1) * SLAB, :] = slab[:, c * tq:(c + 1) * tq]

    kf = jnp.dot(ckv, wuk_ref[...], preferred_element_type=F32)
    zt = zkr.T
    k1, k2 = zt[NOPE:NOPE + half], zt[NOPE + half:NOPE + ROPE]
    kr = jnp.concatenate([jnp.zeros((NOPE, t), F32), k1 * cos - k2 * sin, k1 * sin + k2 * cos,
                          jnp.zeros((SLAB - NOPE - ROPE, t), F32)], axis=0).T
    for h in range(HEADS):
        sl = slice(h * SLAB, (h + 1) * SLAB)
        k_out[0, :, sl] = (kf[:, sl] + kr).astype(BF16)

    vt = lax.dot_general(wuvt_ref[...], ckv, NT_DIMS, preferred_element_type=F32)
    for c in range(t // ATTN_TK):
        cols = slice(c * ATTN_TK, (c + 1) * ATTN_TK)
        for h in range(HEADS):
            pr, r0 = h // HEADS_PER_STEP, (h % HEADS_PER_STEP) * VH_ROWS
            vt_out[0, pr, c, r0:r0 + VDIM, :] = vt[h * VDIM:(h + 1) * VDIM, cols].astype(BF16)
            vt_out[0, pr, c, r0 + VDIM:r0 + VH_ROWS, :] = jnp.ones((ONES_ROWS, ATTN_TK), BF16)

    zg = jnp.dot(hn, wg_ref[:, :3 * POOL_WIDTH], preferred_element_type=F32)
    g_attn = zg[:, :MLA_WIDTH]
    u = zg[:, MLA_WIDTH:MLA_WIDTH + POOL_WIDTH]
    g_pool = zg[:, MLA_WIDTH + POOL_WIDTH:]
    sga_out[0] = (g_attn * _sigmoid(g_attn)).astype(BF16)
    for c in range(2 * D_MODEL // GATE_COLS):
        lo = 3 * POOL_WIDTH + c * GATE_COLS
        gm = jnp.dot(hn, wg_ref[:, lo:lo + GATE_COLS], preferred_element_type=F32)
        gates_out[0, :, c * GATE_COLS:(c + 1) * GATE_COLS] = _sigmoid(gm).astype(BF16)

    ue = jnp.concatenate([carry_ref[...], u], axis=0)
    carry_ref[...] = u[t - HALO:, :]
    pos = si * t + lax.broadcasted_iota(jnp.int32, (t, 1), 0) + 1
    outs = []
    for gi, w in enumerate(POOL_WINDOWS):
        sl = slice(gi * POOL_GROUP, (gi + 1) * POOL_GROUP)
        acc = ue[:, sl]
        step = 1
        while step < w:
            acc = acc + pltpu.roll(acc, step, axis=0)
            step *= 2
        inv = 1.0 / jnp.minimum(pos, w).astype(F32)
        d = acc[HALO:, :] * inv - u[:, sl]
        outs.append(jnp.dot(d.astype(BF16), poolw_ref[gi], preferred_element_type=F32))
    y = jnp.concatenate(outs, axis=-1) * pscale_ref[...]
    ypool_out[0] = (y * (g_pool * _sigmoid(g_pool))).astype(BF16)


def _attn_kernel(qt_ref, k_ref, vt_ref, sga_ref, o_ref, sa_ref, sb_ref, mxa_ref, mxb_ref,
                 m_ref, l_ref, acc_ref):
    tq, tk = ATTN_TQ, ATTN_TK
    nq = HEADS_PER_STEP * tq
    n_qt = qt_ref.shape[2]
    kblocks = tk // CHUNK
    qry_chunk = lax.shift_right_logical(lax.broadcasted_iota(jnp.int32, (1, 1, nq), 2) & (tq - 1),
                                        CHUNK.bit_length() - 1)
    key_block = lax.broadcasted_iota(jnp.int32, (kblocks, 1, 1), 0)

    def score_tile(s_ref, mx_ref, qi, j, mask_for, diagonal=False):
        split = diagonal and tq == tk
        hk, hq, hb = tk // 2, tq // 2, kblocks // 2
        if mask_for is not None:
            allowed = key_block + (j * kblocks - mask_for * (tq // CHUNK)) <= qry_chunk
            bias = jnp.where(allowed, 0.0, -jnp.inf).astype(F32)
        for g in range(PAIRS_PER_STEP):
            for h in range(HEADS_PER_STEP):
                klanes = slice((g * HEADS_PER_STEP + h) * SLAB, (g * HEADS_PER_STEP + h + 1) * SLAB)
                qrows = slice(h * SLAB, (h + 1) * SLAB)
                cols = slice(h * tq, (h + 1) * tq)
                if not split:
                    kt = k_ref[0, pl.ds(pl.multiple_of(j * tk, tk), tk), klanes]
                    s = jnp.dot(kt, qt_ref[0, g, qi, qrows, :], preferred_element_type=F32)
                    if mask_for is not None:
                        s = (s.reshape(kblocks, CHUNK, tq) + bias[:, :, cols]).reshape(tk, tq)
                    s_ref[g, :, cols] = s
                    mx_ref[g, :, cols] = jnp.max(s, axis=0, keepdims=True)
                    continue
                late = slice(h * tq + hq, (h + 1) * tq)
                k_top = k_ref[0, pl.ds(pl.multiple_of(j * tk, tk), hk), klanes]
                k_bot = k_ref[0, pl.ds(pl.multiple_of(j * tk + hk, hk), hk), klanes]
                s = jnp.dot(k_top, qt_ref[0, g, qi, qrows, :], preferred_element_type=F32)
                s = (s.reshape(hb, CHUNK, tq) + bias[:hb, :, cols]).reshape(hk, tq)
                s_ref[g, :hk, cols] = s
                mx = jnp.max(s, axis=0, keepdims=True)
                s = jnp.dot(k_bot, qt_ref[0, g, qi, qrows, hq:], preferred_element_type=F32)
                s = (s.reshape(hb, CHUNK, hq) + bias[hb:, :, late]).reshape(hk, hq)
                s_ref[g, hk:, late] = s
                mx_ref[g, :, cols] = jnp.concatenate(
                    [mx[:, :hq], jnp.maximum(mx[:, hq:], jnp.max(s, axis=0, keepdims=True))], axis=1)

    def process(s_ref, mx_ref, j, diagonal=False):
        split = diagonal and tq == tk
        hk, hq = tk // 2, tq // 2
        for g in range(PAIRS_PER_STEP):
            m_old = m_ref[g]
            m_new = jnp.maximum(m_old, mx_ref[g])
            p = jnp.exp2((s_ref[g, :hk, :] if split else s_ref[g]) - m_new).astype(BF16)
            alpha = jnp.exp2(m_old - m_new)
            m_ref[g] = m_new
            for h in range(HEADS_PER_STEP):
                cols = slice(h * tq, (h + 1) * tq)
                vrows = slice(h * VH_ROWS, (h + 1) * VH_ROWS)
                if split:
                    late = slice(h * tq + hq, (h + 1) * tq)
                    p_late = jnp.exp2(s_ref[g, hk:, late] - m_new[:, late]).astype(BF16)
                    pv = jnp.dot(vt_ref[0, g, j, vrows, :hk], p[:, cols], preferred_element_type=F32)
                    pv_late = jnp.dot(vt_ref[0, g, j, vrows, hk:], p_late, preferred_element_type=F32)
                    pv = jnp.concatenate([pv[:, :hq], pv[:, hq:] + pv_late], axis=1)
                else:
                    pv = jnp.dot(vt_ref[0, g, j, vrows, :], p[:, cols],
                                 preferred_element_type=F32)
                l_ref[g, :, cols] = alpha[:, cols] * l_ref[g, :, cols] + pv[VDIM:VDIM + 1, :]
                acc_ref[g, h] = alpha[:, cols] * acc_ref[g, h] + pv[:VDIM, :]

    score_tile(sa_ref, mxa_ref, 0, 0, 0, diagonal=True)

    def query_tile(i, carry):
        last = ((i + 1) * tq + tk - 1) // tk - 1
        i_next = jnp.minimum(i + 1, n_qt - 1)
        below_mask = None if tq == tk else i

        def pair(j, scores_last):
            score_tile(sb_ref, mxb_ref, i, j + 1, below_mask)
            process(sa_ref, mxa_ref, j)
            score_tile(sa_ref, mxa_ref, i, j + 2, i if scores_last else below_mask, diagonal=scores_last)
            process(sb_ref, mxb_ref, j + 1)

        even = last % 2 == 0
        peeled = jnp.minimum(last // 2, CLOSING_PAIRS)

        def pair_body(jj, c):
            pair(2 * jj, False)
            return c

        lax.fori_loop(0, last // 2 - peeled, pair_body, 0)

        def close_even(n):
            for q in range(n):
                pair(jnp.maximum(last - 2 * (n - q), 0), q == n - 1)
            process(sa_ref, mxa_ref, last, diagonal=True)
            score_tile(sa_ref, mxa_ref, i_next, 0, i_next)
            finish(i)

        def close_odd(n):
            for q in range(n):
                pair(jnp.maximum(last - 1 - 2 * (n - q), 0), False)
            score_tile(sb_ref, mxb_ref, i, last, i, diagonal=True)
            process(sa_ref, mxa_ref, jnp.maximum(last - 1, 0))
            score_tile(sa_ref, mxa_ref, i_next, 0, i_next)
            process(sb_ref, mxb_ref, last, diagonal=True)
            finish(i)

        for n in range(CLOSING_PAIRS + 1):
            pl.when(even & (peeled == n))(functools.partial(close_even, n))
            pl.when(~even & (peeled == n))(functools.partial(close_odd, n))

        return carry

    def reset():
        m_ref[...] = jnp.full_like(m_ref, -jnp.inf)
        l_ref[...] = jnp.zeros_like(l_ref)
        acc_ref[...] = jnp.zeros_like(acc_ref)

    def finish(i):
        rows = pl.ds(pl.multiple_of(i * tq, tq), tq)
        for g in range(PAIRS_PER_STEP):
            inv_l = 1.0 / l_ref[g]
            ot = jnp.concatenate([acc_ref[g, h] * inv_l[:, h * tq:(h + 1) * tq]
                                  for h in range(HEADS_PER_STEP)], axis=0)
            cols = slice(g * LANES, (g + 1) * LANES)
            o_ref[0, rows, cols] = (ot.T * sga_ref[0, rows, cols].astype(F32)).astype(BF16)
        reset()

    reset()
    lax.fori_loop(0, n_qt, query_tile, 0)


def _merge_kernel(x_ref, ya_ref, ypool_ref, gates_ref, wa_ref, wp_ref, wo_ref, nf_ref, out_ref):
    a = jnp.dot(ya_ref[0], wa_ref[...], preferred_element_type=F32)
    p = jnp.dot(ypool_ref[0], wp_ref[...], preferred_element_type=F32)
    ga = gates_ref[0, :, :D_MODEL].astype(F32)
    gp = gates_ref[0, :, D_MODEL:].astype(F32)
    merged = (ga * a + gp * p).astype(BF16)
    h = x_ref[0] + jnp.dot(merged, wo_ref[...], preferred_element_type=F32)
    out_ref[0] = _rms(h, nf_ref[...])


def _rope_tables(seq):
    half = ROPE // 2
    inv_freq = ROPE_THETA ** (-jnp.arange(half, dtype=F32) / half)
    ang = jnp.arange(seq, dtype=F32)[None, :] * inv_freq[:, None]
    return jnp.cos(ang), jnp.sin(ang)


def _const_spec(shape):
    return pl.BlockSpec(shape, lambda *_: (0,) * len(shape), pipeline_mode=pl.Buffered(1))


def kernel(x, norm_in, w_in, q_norm, w_uq, kv_norm, w_ukv, pool_w, pool_scale,
           w_branch_attn, w_branch_pool, w_out, norm_final):
    b, s, d = x.shape
    tq, tk, t1, t3 = ATTN_TQ, ATTN_TK, PROJ_TILE, MERGE_TILE
    assert d == D_MODEL and s % t1 == 0 and t1 % tk == 0 and t1 % tq == 0 and s % t3 == 0
    hp = HEADS // HEADS_PER_STEP
    nq = HEADS_PER_STEP * tq
    g2 = PAIRS_PER_STEP

    n_gate = w_in.shape[1] - (Q_RANK + KV_RANK + ROPE)
    n_prep = d // PREP_ROWS
    rows_spec = lambda rows: pl.BlockSpec((rows // n_prep, d), lambda i: (i, 0))
    w_a, w_g, w_ba16, w_bp16, w_o16 = pl.pallas_call(
        _weight_prep_kernel,
        grid=(n_prep,),
        in_specs=[pl.BlockSpec((w_in.shape[1], PREP_ROWS), lambda i: (0, i)),
                  rows_spec(MLA_WIDTH), rows_spec(POOL_WIDTH), rows_spec(d)],
        out_specs=[pl.BlockSpec((PREP_ROWS, LATENT_COLS), lambda i: (i, 0)),
                   pl.BlockSpec((PREP_ROWS, n_gate), lambda i: (i, 0)),
                   rows_spec(MLA_WIDTH), rows_spec(POOL_WIDTH), rows_spec(d)],
        out_shape=[jax.ShapeDtypeStruct((d, LATENT_COLS), BF16),
                   jax.ShapeDtypeStruct((d, n_gate), BF16),
                   jax.ShapeDtypeStruct((MLA_WIDTH, d), BF16),
                   jax.ShapeDtypeStruct((POOL_WIDTH, d), BF16),
                   jax.ShapeDtypeStruct((d, d), BF16)],
        name="weight_prep",
    )(w_in.T, w_branch_attn, w_branch_pool, w_out)
    w_uq_p = jnp.pad(w_uq, ((0, 0), (0, 0), (0, SLAB - NOPE - ROPE)))
    w_uqt = w_uq_p.reshape(Q_RANK, HEADS * SLAB).T.astype(BF16)
    w_uk_p = jnp.pad(w_ukv[:, :, :NOPE], ((0, 0), (0, 0), (0, SLAB - NOPE)))
    w_uk = w_uk_p.reshape(KV_RANK, HEADS * SLAB).astype(BF16)
    w_uvt = w_ukv[:, :, NOPE:].reshape(KV_RANK, MLA_WIDTH).T.astype(BF16)
    cos_t, sin_t = _rope_tables(s)

    row = lambda v: v.reshape(1, -1).astype(F32)
    tok = lambda width: pl.BlockSpec((1, t1, width), lambda bi, si: (bi, si, 0))
    tabt = pl.BlockSpec((ROPE // 2, t1), lambda bi, si: (0, si))
    qt, k, vt, sga, ypool, gates = pl.pallas_call(
        _proj_kernel,
        grid=(b, s // t1),
        in_specs=[tok(d), _const_spec((1, d)), _const_spec(w_a.shape), _const_spec(w_g.shape),
                  _const_spec((1, Q_RANK)), _const_spec(w_uqt.shape), _const_spec((1, KV_RANK)),
                  _const_spec(w_uk.shape), _const_spec(w_uvt.shape),
                  _const_spec(pool_w.shape), _const_spec((1, POOL_WIDTH)),
                  tabt, tabt],
        out_specs=[pl.BlockSpec((1, hp, t1 // tq, HEADS_PER_STEP * SLAB, tq),
                                lambda bi, si: (bi, 0, si, 0, 0)),
                   tok(HEADS * SLAB),
                   pl.BlockSpec((1, hp, t1 // tk, VT_ROWS, tk), lambda bi, si: (bi, 0, si, 0, 0)),
                   tok(MLA_WIDTH), tok(POOL_WIDTH), tok(2 * d)],
        out_shape=[jax.ShapeDtypeStruct((b, hp, s // tq, HEADS_PER_STEP * SLAB, tq), BF16),
                   jax.ShapeDtypeStruct((b, s, HEADS * SLAB), BF16),
                   jax.ShapeDtypeStruct((b, hp, s // tk, VT_ROWS, tk), BF16),
                   jax.ShapeDtypeStruct((b, s, MLA_WIDTH), BF16),
                   jax.ShapeDtypeStruct((b, s, POOL_WIDTH), BF16),
                   jax.ShapeDtypeStruct((b, s, 2 * d), BF16)],
        scratch_shapes=[pltpu.VMEM((HALO, POOL_WIDTH), F32)],
        compiler_params=pltpu.CompilerParams(
            dimension_semantics=("arbitrary", "arbitrary"), vmem_limit_bytes=VMEM_LIMIT),
        name="proj",
    )(x, row(norm_in), w_a, w_g, row(q_norm), w_uqt, row(kv_norm), w_uk, w_uvt,
      pool_w.astype(BF16), row(pool_scale), cos_t, sin_t)

    o = pl.pallas_call(
        _attn_kernel,
        grid=(b, hp // g2),
        in_specs=[pl.BlockSpec((1, g2, s // tq, HEADS_PER_STEP * SLAB, tq), lambda bi, pi: (bi, pi, 0, 0, 0)),
                  pl.BlockSpec((1, s, g2 * HEADS_PER_STEP * SLAB), lambda bi, pi: (bi, 0, pi)),
                  pl.BlockSpec((1, g2, s // tk, VT_ROWS, tk), lambda bi, pi: (bi, pi, 0, 0, 0)),
                  pl.BlockSpec((1, s, g2 * LANES), lambda bi, pi: (bi, 0, pi))],
        out_specs=pl.BlockSpec((1, s, g2 * LANES), lambda bi, pi: (bi, 0, pi)),
        out_shape=jax.ShapeDtypeStruct((b, s, MLA_WIDTH), BF16),
        scratch_shapes=[pltpu.VMEM((g2, tk, nq), F32), pltpu.VMEM((g2, tk, nq), F32),
                        pltpu.VMEM((g2, 1, nq), F32), pltpu.VMEM((g2, 1, nq), F32),
                        pltpu.VMEM((g2, 1, nq), F32), pltpu.VMEM((g2, 1, nq), F32),
                        pltpu.VMEM((g2, HEADS_PER_STEP, VDIM, tq), F32)],
        compiler_params=pltpu.CompilerParams(
            dimension_semantics=("arbitrary", "arbitrary"), vmem_limit_bytes=VMEM_LIMIT),
        name="attn",
    )(qt, k, vt, sga)

    tok3 = lambda width: pl.BlockSpec((1, t3, width), lambda bi, si: (bi, si, 0))
    out = pl.pallas_call(
        _merge_kernel,
        grid=(b, s // t3),
        in_specs=[tok3(d), tok3(MLA_WIDTH), tok3(POOL_WIDTH), tok3(2 * d),
                  _const_spec((MLA_WIDTH, d)), _const_spec((POOL_WIDTH, d)), _const_spec((d, d)),
                  _const_spec((1, d))],
        out_specs=tok3(d),
        out_shape=jax.ShapeDtypeStruct((b, s, d), x.dtype),
        compiler_params=pltpu.CompilerParams(
            dimension_semantics=("arbitrary", "arbitrary"), vmem_limit_bytes=VMEM_LIMIT),
        name="merge",
    )(x, o, ypool, gates, w_ba16, w_bp16, w_o16, row(norm_final))
    return out
```

```python
import functools

import jax
import jax.numpy as jnp
from jax import lax
from jax.experimental import pallas as pl
from jax.experimental.pallas import tpu as pltpu

D_MODEL = 1024
CHUNK = 64
HEADS = 8
NOPE = 64
ROPE = 32
VDIM = 64
Q_RANK = 384
KV_RANK = 256
MLA_WIDTH = HEADS * VDIM
ROPE_THETA = 10000.0
POOL_WINDOWS = (2, 4, 8, 16)
POOL_WIDTH = D_MODEL // 2
POOL_GROUP = POOL_WIDTH // len(POOL_WINDOWS)
EPS = 1e-6
QK_SCALE = (NOPE + ROPE) ** -0.5 * 1.4426950408889634

LANES = 128
HALO = 16
SLAB = LANES
LATENT_COLS = Q_RANK + KV_RANK + SLAB
PREP_ROWS = 256
HEADS_PER_STEP = 2
PAIRS_PER_STEP = 2
ONES_ROWS = 16
VH_ROWS = VDIM + ONES_ROWS
VT_ROWS = HEADS_PER_STEP * VH_ROWS
PROJ_TILE = 512
GATE_COLS = 512
ATTN_TQ = 512
ATTN_TK = 512
CLOSING_PAIRS = 2
MERGE_TILE = 1024
VMEM_LIMIT = 56 * 1024 * 1024

BF16 = jnp.bfloat16
F32 = jnp.float32
NT_DIMS = (((1,), (1,)), ((), ()))


def _rms(x, g):
    return x * lax.rsqrt(jnp.mean(x * x, axis=-1, keepdims=True) + EPS) * g


def _sigmoid(x):
    return 1.0 / (1.0 + jnp.exp(-x))


def _weight_prep_kernel(wt_ref, wba_ref, wbp_ref, wo_ref, wa_ref, wg_ref, wba_out, wbp_out, wo_out):
    wba_out[...] = wba_ref[...].astype(BF16)
    wbp_out[...] = wbp_ref[...].astype(BF16)
    wo_out[...] = wo_ref[...].astype(BF16)
    o_kr = Q_RANK + KV_RANK
    wa_ref[:, :o_kr] = wt_ref[:o_kr, :].T.astype(BF16)
    kr = jnp.concatenate([jnp.zeros((NOPE, PREP_ROWS), F32), wt_ref[o_kr:o_kr + ROPE, :],
                          jnp.zeros((SLAB - NOPE - ROPE, PREP_ROWS), F32)], axis=0)
    wa_ref[:, o_kr:] = kr.T.astype(BF16)
    wg_ref[...] = wt_ref[o_kr + ROPE:, :].T.astype(BF16)


def _proj_kernel(x_ref, nin_ref, wa_ref, wg_ref, qn_ref, wuqt_ref, kvn_ref, wuk_ref, wuvt_ref,
                 poolw_ref, pscale_ref, cos_ref, sin_ref,
                 qt_out, k_out, vt_out, sga_out, ypool_out, gates_out, carry_ref):
    t = PROJ_TILE
    si = pl.program_id(1)

    @pl.when(si == 0)
    def _():
        carry_ref[...] = jnp.zeros_like(carry_ref)

    x = x_ref[0]
    hn = _rms(x, nin_ref[...]).astype(BF16)

    za = jnp.dot(hn, wa_ref[...], preferred_element_type=F32)
    zq = za[:, :Q_RANK]
    zkv = za[:, Q_RANK:Q_RANK + KV_RANK]
    zkr = za[:, Q_RANK + KV_RANK:]
    cq = _rms(zq, qn_ref[...]).astype(BF16)
    ckv = _rms(zkv, kvn_ref[...]).astype(BF16)

    qt = lax.dot_general(wuqt_ref[...], cq, NT_DIMS, preferred_element_type=F32)
    scale = QK_SCALE
    cos, sin = cos_ref[...], sin_ref[...]
    qcos, qsin = cos * scale, sin * scale
    half = ROPE // 2
    tq = ATTN_TQ
    for h in range(HEADS):
        r0 = h * SLAB
        x1 = qt[r0 + NOPE:r0 + NOPE + half]
        x2 = qt[r0 + NOPE + half:r0 + NOPE + ROPE]
        slab = jnp.concatenate([qt[r0:r0 + NOPE] * scale, x1 * qcos - x2 * qsin, x1 * qsin + x2 * qcos,
                                jnp.zeros((SLAB - NOPE - ROPE, t), F32)], axis=0).astype(BF16)
        pr, hs = h // HEADS_PER_STEP, h % HEADS_PER_STEP
        for c in range(t // tq):
            qt_out[0, pr, c, hs * SLAB:(hs + 1) * SLAB, :] = slab[:, c * tq:(c + 1) * tq]

    kf = jnp.dot(ckv, wuk_ref[...], preferred_element_type=F32)
    zt = zkr.T
    k1, k2 = zt[NOPE:NOPE + half], zt[NOPE + half:NOPE + ROPE]
    kr = jnp.concatenate([jnp.zeros((NOPE, t), F32), k1 * cos - k2 * sin, k1 * sin + k2 * cos,
                          jnp.zeros((SLAB - NOPE - ROPE, t), F32)], axis=0).T
    for h in range(HEADS):
        sl = slice(h * SLAB, (h + 1) * SLAB)
        k_out[0, :, sl] = (kf[:, sl] + kr).astype(BF16)

    vt = lax.dot_general(wuvt_ref[...], ckv, NT_DIMS, preferred_element_type=F32)
    for c in range(t // ATTN_TK):
        cols = slice(c * ATTN_TK, (c + 1) * ATTN_TK)
        for h in range(HEADS):
            pr, r0 = h // HEADS_PER_STEP, (h % HEADS_PER_STEP) * VH_ROWS
            vt_out[0, pr, c, r0:r0 + VDIM, :] = vt[h * VDIM:(h + 1) * VDIM, cols].astype(BF16)
            vt_out[0, pr, c, r0 + VDIM:r0 + VH_ROWS, :] = jnp.ones((ONES_ROWS, ATTN_TK), BF16)

    zg = jnp.dot(hn, wg_ref[:, :3 * POOL_WIDTH], preferred_element_type=F32)
    g_attn = zg[:, :MLA_WIDTH]
    u = zg[:, MLA_WIDTH:MLA_WIDTH + POOL_WIDTH]
    g_pool = zg[:, MLA_WIDTH + POOL_WIDTH:]
    sga_out[0] = (g_attn * _sigmoid(g_attn)).astype(BF16)
    for c in range(2 * D_MODEL // GATE_COLS):
        lo = 3 * POOL_WIDTH + c * GATE_COLS
        gm = jnp.dot(hn, wg_ref[:, lo:lo + GATE_COLS], preferred_element_type=F32)
        gates_out[0, :, c * GATE_COLS:(c + 1) * GATE_COLS] = _sigmoid(gm).astype(BF16)

    ue = jnp.concatenate([carry_ref[...], u], axis=0)
    carry_ref[...] = u[t - HALO:, :]
    pos = si * t + lax.broadcasted_iota(jnp.int32, (t, 1), 0) + 1
    outs = []
    for gi, w in enumerate(POOL_WINDOWS):
        sl = slice(gi * POOL_GROUP, (gi + 1) * POOL_GROUP)
        acc = ue[:, sl]
        step = 1
        while step < w:
            acc = acc + pltpu.roll(acc, step, axis=0)
            step *= 2
        inv = 1.0 / jnp.minimum(pos, w).astype(F32)
        d = acc[HALO:, :] * inv - u[:, sl]
        outs.append(jnp.dot(d.astype(BF16), poolw_ref[gi], preferred_element_type=F32))
    y = jnp.concatenate(outs, axis=-1) * pscale_ref[...]
    ypool_out[0] = (y * (g_pool * _sigmoid(g_pool))).astype(BF16)


def _attn_kernel(qt_ref, k_ref, vt_ref, sga_ref, o_ref, sa_ref, sb_ref, mxa_ref, mxb_ref,
                 m_ref, l_ref, acc_ref):
    tq, tk = ATTN_TQ, ATTN_TK
    nq = HEADS_PER_STEP * tq
    n_qt = qt_ref.shape[2]
    kblocks = tk // CHUNK
    qry_chunk = lax.shift_right_logical(lax.broadcasted_iota(jnp.int32, (1, 1, nq), 2) & (tq - 1),
                                        CHUNK.bit_length() - 1)
    key_block = lax.broadcasted_iota(jnp.int32, (kblocks, 1, 1), 0)

    def score_tile(s_ref, mx_ref, qi, j, mask_for, diagonal=False):
        split = diagonal and tq == tk
        hk, hq, hb = tk // 2, tq // 2, kblocks // 2
        if mask_for is not None:
            allowed = key_block + (j * kblocks - mask_for * (tq // CHUNK)) <= qry_chunk
            bias = jnp.where(allowed, 0.0, -jnp.inf).astype(F32)
        for g in range(PAIRS_PER_STEP):
            for h in range(HEADS_PER_STEP):
                klanes = slice((g * HEADS_PER_STEP + h) * SLAB, (g * HEADS_PER_STEP + h + 1) * SLAB)
                qrows = slice(h * SLAB, (h + 1) * SLAB)
                cols = slice(h * tq, (h + 1) * tq)
                if not split:
                    kt = k_ref[0, pl.ds(pl.multiple_of(j * tk, tk), tk), klanes]
                    s = jnp.dot(kt, qt_ref[0, g, qi, qrows, :], preferred_element_type=F32)
                    if mask_for is not None:
                        s = (s.reshape(kblocks, CHUNK, tq) + bias[:, :, cols]).reshape(tk, tq)
                    s_ref[g, :, cols] = s
                    mx_ref[g, :, cols] = jnp.max(s, axis=0, keepdims=True)
                    continue
                late = slice(h * tq + hq, (h + 1) * tq)
                k_top = k_ref[0, pl.ds(pl.multiple_of(j * tk, tk), hk), klanes]
                k_bot = k_ref[0, pl.ds(pl.multiple_of(j * tk + hk, hk), hk), klanes]
                s = jnp.dot(k_top, qt_ref[0, g, qi, qrows, :], preferred_element_type=F32)
                s = (s.reshape(hb, CHUNK, tq) + bias[:hb, :, cols]).reshape(hk, tq)
                s_ref[g, :hk, cols] = s
                mx = jnp.max(s, axis=0, keepdims=True)
                s = jnp.dot(k_bot, qt_ref[0, g, qi, qrows, hq:], preferred_element_type=F32)
                s = (s.reshape(hb, CHUNK, hq) + bias[hb:, :, late]).reshape(hk, hq)
                s_ref[g, hk:, late] = s
                mx_ref[g, :, cols] = jnp.concatenate(
                    [mx[:, :hq], jnp.maximum(mx[:, hq:], jnp.max(s, axis=0, keepdims=True))], axis=1)

    def process(s_ref, mx_ref, j, diagonal=False):
        split = diagonal and tq == tk
        hk, hq = tk // 2, tq // 2
        for g in range(PAIRS_PER_STEP):
            m_old = m_ref[g]
            m_new = jnp.maximum(m_old, mx_ref[g])
            p = jnp.exp2((s_ref[g, :hk, :] if split else s_ref[g]) - m_new).astype(BF16)
            alpha = jnp.exp2(m_old - m_new)
            m_ref[g] = m_new
            for h in range(HEADS_PER_STEP):
                cols = slice(h * tq, (h + 1) * tq)
                vrows = slice(h * VH_ROWS, (h + 1) * VH_ROWS)
                if split:
                    late = slice(h * tq + hq, (h + 1) * tq)
                    p_late = jnp.exp2(s_ref[g, hk:, late] - m_new[:, late]).astype(BF16)
                    pv = jnp.dot(vt_ref[0, g, j, vrows, :hk], p[:, cols], preferred_element_type=F32)
                    pv_late = jnp.dot(vt_ref[0, g, j, vrows, hk:], p_late, preferred_element_type=F32)
                    pv = jnp.concatenate([pv[:, :hq], pv[:, hq:] + pv_late], axis=1)
                else:
                    pv = jnp.dot(vt_ref[0, g, j, vrows, :], p[:, cols],
                                 preferred_element_type=F32)
                l_ref[g, :, cols] = alpha[:, cols] * l_ref[g, :, cols] + pv[VDIM:VDIM + 1, :]
                acc_ref[g, h] = alpha[:, cols] * acc_ref[g, h] + pv[:VDIM, :]

    score_tile(sa_ref, mxa_ref, 0, 0, 0, diagonal=True)

    def query_tile(i, carry):
        last = ((i + 1) * tq + tk - 1) // tk - 1
        i_next = jnp.minimum(i + 1, n_qt - 1)
        below_mask = None if tq == tk else i
        next_mask = None if tq == tk else i_next

        def pair(j, scores_last):
            score_tile(sb_ref, mxb_ref, i, j + 1, below_mask)
            process(sa_ref, mxa_ref, j)
            score_tile(sa_ref, mxa_ref, i, j + 2, i if scores_last else below_mask, diagonal=scores_last)
            process(sb_ref, mxb_ref, j + 1)

        even = last % 2 == 0
        peeled = jnp.minimum(last // 2, CLOSING_PAIRS)

        def pair_body(jj, c):
            pair(2 * jj, False)
            return c

        lax.fori_loop(0, last // 2 - peeled, pair_body, 0)

        def close_even(n):
            for q in range(n):
                pair(jnp.maximum(last - 2 * (n - q), 0), q == n - 1)
            process(sa_ref, mxa_ref, last, diagonal=True)
            score_tile(sa_ref, mxa_ref, i_next, 0, next_mask)
            finish(i)

        def close_odd(n):
            for q in range(n):
                pair(jnp.maximum(last - 1 - 2 * (n - q), 0), False)
            score_tile(sb_ref, mxb_ref, i, last, i, diagonal=True)
            process(sa_ref, mxa_ref, jnp.maximum(last - 1, 0))
            score_tile(sa_ref, mxa_ref, i_next, 0, next_mask)
            process(sb_ref, mxb_ref, last, diagonal=True)
            finish(i)

        for n in range(CLOSING_PAIRS + 1):
            pl.when(even & (peeled == n))(functools.partial(close_even, n))
            pl.when(~even & (peeled == n))(functools.partial(close_odd, n))

        return carry

    def reset():
        m_ref[...] = jnp.full_like(m_ref, -jnp.inf)
        l_ref[...] = jnp.zeros_like(l_ref)
        acc_ref[...] = jnp.zeros_like(acc_ref)

    def finish(i):
        rows = pl.ds(pl.multiple_of(i * tq, tq), tq)
        for g in range(PAIRS_PER_STEP):
            inv_l = 1.0 / l_ref[g]
            ot = jnp.concatenate([acc_ref[g, h] * inv_l[:, h * tq:(h + 1) * tq]
                                  for h in range(HEADS_PER_STEP)], axis=0)
            cols = slice(g * LANES, (g + 1) * LANES)
            o_ref[0, rows, cols] = (ot.T * sga_ref[0, rows, cols].astype(F32)).astype(BF16)
        reset()

    reset()
    lax.fori_loop(0, n_qt, query_tile, 0)


def _merge_kernel(x_ref, ya_ref, ypool_ref, gates_ref, wa_ref, wp_ref, wo_ref, nf_ref, out_ref):
    a = jnp.dot(ya_ref[0], wa_ref[...], preferred_element_type=F32)
    p = jnp.dot(ypool_ref[0], wp_ref[...], preferred_element_type=F32)
    ga = gates_ref[0, :, :D_MODEL].astype(F32)
    gp = gates_ref[0, :, D_MODEL:].astype(F32)
    merged = (ga * a + gp * p).astype(BF16)
    h = x_ref[0] + jnp.dot(merged, wo_ref[...], preferred_element_type=F32)
    out_ref[0] = _rms(h, nf_ref[...])


def _rope_tables(seq):
    half = ROPE // 2
    inv_freq = ROPE_THETA ** (-jnp.arange(half, dtype=F32) / half)
    ang = jnp.arange(seq, dtype=F32)[None, :] * inv_freq[:, None]
    return jnp.cos(ang), jnp.sin(ang)


def _const_spec(shape):
    return pl.BlockSpec(shape, lambda *_: (0,) * len(shape), pipeline_mode=pl.Buffered(1))


def kernel(x, norm_in, w_in, q_norm, w_uq, kv_norm, w_ukv, pool_w, pool_scale,
           w_branch_attn, w_branch_pool, w_out, norm_final):
    b, s, d = x.shape
    tq, tk, t1, t3 = ATTN_TQ, ATTN_TK, PROJ_TILE, MERGE_TILE
    assert d == D_MODEL and s % t1 == 0 and t1 % tk == 0 and t1 % tq == 0 and s % t3 == 0
    hp = HEADS // HEADS_PER_STEP
    nq = HEADS_PER_STEP * tq
    g2 = PAIRS_PER_STEP

    n_gate = w_in.shape[1] - (Q_RANK + KV_RANK + ROPE)
    n_prep = d // PREP_ROWS
    rows_spec = lambda rows: pl.BlockSpec((rows // n_prep, d), lambda i: (i, 0))
    w_a, w_g, w_ba16, w_bp16, w_o16 = pl.pallas_call(
        _weight_prep_kernel,
        grid=(n_prep,),
        in_specs=[pl.BlockSpec((w_in.shape[1], PREP_ROWS), lambda i: (0, i)),
                  rows_spec(MLA_WIDTH), rows_spec(POOL_WIDTH), rows_spec(d)],
        out_specs=[pl.BlockSpec((PREP_ROWS, LATENT_COLS), lambda i: (i, 0)),
                   pl.BlockSpec((PREP_ROWS, n_gate), lambda i: (i, 0)),
                   rows_spec(MLA_WIDTH), rows_spec(POOL_WIDTH), rows_spec(d)],
        out_shape=[jax.ShapeDtypeStruct((d, LATENT_COLS), BF16),
                   jax.ShapeDtypeStruct((d, n_gate), BF16),
                   jax.ShapeDtypeStruct((MLA_WIDTH, d), BF16),
                   jax.ShapeDtypeStruct((POOL_WIDTH, d), BF16),
                   jax.ShapeDtypeStruct((d, d), BF16)],
        name="weight_prep",
    )(w_in.T, w_branch_attn, w_branch_pool, w_out)
    w_uq_p = jnp.pad(w_uq, ((0, 0), (0, 0), (0, SLAB - NOPE - ROPE)))
    w_uqt = w_uq_p.reshape(Q_RANK, HEADS * SLAB).T.astype(BF16)
    w_uk_p = jnp.pad(w_ukv[:, :, :NOPE], ((0, 0), (0, 0), (0, SLAB - NOPE)))
    w_uk = w_uk_p.reshape(KV_RANK, HEADS * SLAB).astype(BF16)
    w_uvt = w_ukv[:, :, NOPE:].reshape(KV_RANK, MLA_WIDTH).T.astype(BF16)
    cos_t, sin_t = _rope_tables(s)

    row = lambda v: v.reshape(1, -1).astype(F32)
    tok = lambda width: pl.BlockSpec((1, t1, width), lambda bi, si: (bi, si, 0))
    tabt = pl.BlockSpec((ROPE // 2, t1), lambda bi, si: (0, si))
    qt, k, vt, sga, ypool, gates = pl.pallas_call(
        _proj_kernel,
        grid=(b, s // t1),
        in_specs=[tok(d), _const_spec((1, d)), _const_spec(w_a.shape), _const_spec(w_g.shape),
                  _const_spec((1, Q_RANK)), _const_spec(w_uqt.shape), _const_spec((1, KV_RANK)),
                  _const_spec(w_uk.shape), _const_spec(w_uvt.shape),
                  _const_spec(pool_w.shape), _const_spec((1, POOL_WIDTH)),
                  tabt, tabt],
        out_specs=[pl.BlockSpec((1, hp, t1 // tq, HEADS_PER_STEP * SLAB, tq),
                                lambda bi, si: (bi, 0, si, 0, 0)),
                   tok(HEADS * SLAB),
                   pl.BlockSpec((1, hp, t1 // tk, VT_ROWS, tk), lambda bi, si: (bi, 0, si, 0, 0)),
                   tok(MLA_WIDTH), tok(POOL_WIDTH), tok(2 * d)],
        out_shape=[jax.ShapeDtypeStruct((b, hp, s // tq, HEADS_PER_STEP * SLAB, tq), BF16),
                   jax.ShapeDtypeStruct((b, s, HEADS * SLAB), BF16),
                   jax.ShapeDtypeStruct((b, hp, s // tk, VT_ROWS, tk), BF16),
                   jax.ShapeDtypeStruct((b, s, MLA_WIDTH), BF16),
                   jax.ShapeDtypeStruct((b, s, POOL_WIDTH), BF16),
                   jax.ShapeDtypeStruct((b, s, 2 * d), BF16)],
        scratch_shapes=[pltpu.VMEM((HALO, POOL_WIDTH), F32)],
        compiler_params=pltpu.CompilerParams(
            dimension_semantics=("arbitrary", "arbitrary"), vmem_limit_bytes=VMEM_LIMIT),
        name="proj",
    )(x, row(norm_in), w_a, w_g, row(q_norm), w_uqt, row(kv_norm), w_uk, w_uvt,
      pool_w.astype(BF16), row(pool_scale), cos_t, sin_t)

    o = pl.pallas_call(
        _attn_kernel,
        grid=(b, hp // g2),
        in_specs=[pl.BlockSpec((1, g2, s // tq, HEADS_PER_STEP * SLAB, tq), lambda bi, pi: (bi, pi, 0, 0, 0)),
                  pl.BlockSpec((1, s, g2 * HEADS_PER_STEP * SLAB), lambda bi, pi: (bi, 0, pi)),
                  pl.BlockSpec((1, g2, s // tk, VT_ROWS, tk), lambda bi, pi: (bi, pi, 0, 0, 0)),
                  pl.BlockSpec((1, s, g2 * LANES), lambda bi, pi: (bi, 0, pi))],
        out_specs=pl.BlockSpec((1, s, g2 * LANES), lambda bi, pi: (bi, 0, pi)),
        out_shape=jax.ShapeDtypeStruct((b, s, MLA_WIDTH), BF16),
        scratch_shapes=[pltpu.VMEM((g2, tk, nq), F32), pltpu.VMEM((g2, tk, nq), F32),
                        pltpu.VMEM((g2, 1, nq), F32), pltpu.VMEM((g2, 1, nq), F32),
                        pltpu.VMEM((g2, 1, nq), F32), pltpu.VMEM((g2, 1, nq), F32),
                        pltpu.VMEM((g2, HEADS_PER_STEP, VDIM, tq), F32)],
        compiler_params=pltpu.CompilerParams(
            dimension_semantics=("arbitrary", "arbitrary"), vmem_limit_bytes=VMEM_LIMIT),
        name="attn",
    )(qt, k, vt, sga)

    tok3 = lambda width: pl.BlockSpec((1, t3, width), lambda bi, si: (bi, si, 0))
    out = pl.pallas_call(
        _merge_kernel,
        grid=(b, s // t3),
        in_specs=[tok3(d), tok3(MLA_WIDTH), tok3(POOL_WIDTH), tok3(2 * d),
                  _const_spec((MLA_WIDTH, d)), _const_spec((POOL_WIDTH, d)), _const_spec((d, d)),
                  _const_spec((1, d))],
        out_specs=tok3(d),
        out_shape=jax.ShapeDtypeStruct((b, s, d), x.dtype),
        compiler_params=pltpu.CompilerParams(
            dimension_semantics=("arbitrary", "arbitrary"), vmem_limit_bytes=VMEM_LIMIT),
        name="merge",
    )(x, o, ypool, gates, w_ba16, w_bp16, w_o16, row(norm_final))
    return out
```

```python
import functools

import jax
import jax.numpy as jnp
from jax import lax
from jax.experimental import pallas as pl
from jax.experimental.pallas import tpu as pltpu

D_MODEL = 1024
CHUNK = 64
HEADS = 8
NOPE = 64
ROPE = 32
VDIM = 64
Q_RANK = 384
KV_RANK = 256
MLA_WIDTH = HEADS * VDIM
ROPE_THETA = 10000.0
POOL_WINDOWS = (2, 4, 8, 16)
POOL_WIDTH = D_MODEL // 2
POOL_GROUP = POOL_WIDTH // len(POOL_WINDOWS)
EPS = 1e-6
QK_SCALE = (NOPE + ROPE) ** -0.5 * 1.4426950408889634

LANES = 128
HALO = 16
SLAB = LANES
LATENT_COLS = Q_RANK + KV_RANK + SLAB
PREP_ROWS = 256
HEADS_PER_STEP = 2
PAIRS_PER_STEP = 2
ONES_ROWS = 16
VH_ROWS = VDIM + ONES_ROWS
VT_ROWS = HEADS_PER_STEP * VH_ROWS
PROJ_TILE = 512
GATE_COLS = 512
ATTN_TQ = 512
ATTN_TK = 512
CLOSING_PAIRS = 2
STATIC_QUERY_TILES = 2
MERGE_TILE = 1024
VMEM_LIMIT = 56 * 1024 * 1024

BF16 = jnp.bfloat16
F32 = jnp.float32
NT_DIMS = (((1,), (1,)), ((), ()))


def _rms(x, g):
    return x * lax.rsqrt(jnp.mean(x * x, axis=-1, keepdims=True) + EPS) * g


def _sigmoid(x):
    return 1.0 / (1.0 + jnp.exp(-x))


def _weight_prep_kernel(wt_ref, wba_ref, wbp_ref, wo_ref, wa_ref, wg_ref, wba_out, wbp_out, wo_out):
    wba_out[...] = wba_ref[...].astype(BF16)
    wbp_out[...] = wbp_ref[...].astype(BF16)
    wo_out[...] = wo_ref[...].astype(BF16)
    o_kr = Q_RANK + KV_RANK
    wa_ref[:, :o_kr] = wt_ref[:o_kr, :].T.astype(BF16)
    kr = jnp.concatenate([jnp.zeros((NOPE, PREP_ROWS), F32), wt_ref[o_kr:o_kr + ROPE, :],
                          jnp.zeros((SLAB - NOPE - ROPE, PREP_ROWS), F32)], axis=0)
    wa_ref[:, o_kr:] = kr.T.astype(BF16)
    wg_ref[...] = wt_ref[o_kr + ROPE:, :].T.astype(BF16)


def _proj_kernel(x_ref, nin_ref, wa_ref, wg_ref, qn_ref, wuqt_ref, kvn_ref, wuk_ref, wuvt_ref,
                 poolw_ref, pscale_ref, cos_ref, sin_ref,
                 qt_out, k_out, vt_out, sga_out, ypool_out, gates_out, carry_ref):
    t = PROJ_TILE
    si = pl.program_id(1)

    @pl.when(si == 0)
    def _():
        carry_ref[...] = jnp.zeros_like(carry_ref)

    x = x_ref[0]
    hn = _rms(x, nin_ref[...]).astype(BF16)

    za = jnp.dot(hn, wa_ref[...], preferred_element_type=F32)
    zq = za[:, :Q_RANK]
    zkv = za[:, Q_RANK:Q_RANK + KV_RANK]
    zkr = za[:, Q_RANK + KV_RANK:]
    cq = _rms(zq, qn_ref[...]).astype(BF16)
    ckv = _rms(zkv, kvn_ref[...]).astype(BF16)

    qt = lax.dot_general(wuqt_ref[...], cq, NT_DIMS, preferred_element_type=F32)
    scale = QK_SCALE
    cos, sin = cos_ref[...], sin_ref[...]
    qcos, qsin = cos * scale, sin * scale
    half = ROPE // 2
    tq = ATTN_TQ
    for h in range(HEADS):
        r0 = h * SLAB
        x1 = qt[r0 + NOPE:r0 + NOPE + half]
        x2 = qt[r0 + NOPE + half:r0 + NOPE + ROPE]
        slab = jnp.concatenate([qt[r0:r0 + NOPE] * scale, x1 * qcos - x2 * qsin, x1 * qsin + x2 * qcos,
                                jnp.zeros((SLAB - NOPE - ROPE, t), F32)], axis=0).astype(BF16)
        pr, hs = h // HEADS_PER_STEP, h % HEADS_PER_STEP
        for c in range(t // tq):
            qt_out[0, pr, c, hs * SLAB:(hs + 1) * SLAB, :] = slab[:, c * tq:(c + 1) * tq]

    kf = jnp.dot(ckv, wuk_ref[...], preferred_element_type=F32)
    zt = zkr.T
    k1, k2 = zt[NOPE:NOPE + half], zt[NOPE + half:NOPE + ROPE]
    kr = jnp.concatenate([jnp.zeros((NOPE, t), F32), k1 * cos - k2 * sin, k1 * sin + k2 * cos,
                          jnp.zeros((SLAB - NOPE - ROPE, t), F32)], axis=0).T
    for h in range(HEADS):
        sl = slice(h * SLAB, (h + 1) * SLAB)
        k_out[0, :, sl] = (kf[:, sl] + kr).astype(BF16)

    vt = lax.dot_general(wuvt_ref[...], ckv, NT_DIMS, preferred_element_type=F32)
    for c in range(t // ATTN_TK):
        cols = slice(c * ATTN_TK, (c + 1) * ATTN_TK)
        for h in range(HEADS):
            pr, r0 = h // HEADS_PER_STEP, (h % HEADS_PER_STEP) * VH_ROWS
            vt_out[0, pr, c, r0:r0 + VDIM, :] = vt[h * VDIM:(h + 1) * VDIM, cols].astype(BF16)
            vt_out[0, pr, c, r0 + VDIM:r0 + VH_ROWS, :] = jnp.ones((ONES_ROWS, ATTN_TK), BF16)

    zg = jnp.dot(hn, wg_ref[:, :3 * POOL_WIDTH], preferred_element_type=F32)
    g_attn = zg[:, :MLA_WIDTH]
    u = zg[:, MLA_WIDTH:MLA_WIDTH + POOL_WIDTH]
    g_pool = zg[:, MLA_WIDTH + POOL_WIDTH:]
    sga_out[0] = (g_attn * _sigmoid(g_attn)).astype(BF16)
    for c in range(2 * D_MODEL // GATE_COLS):
        lo = 3 * POOL_WIDTH + c * GATE_COLS
        gm = jnp.dot(hn, wg_ref[:, lo:lo + GATE_COLS], preferred_element_type=F32)
        gates_out[0, :, c * GATE_COLS:(c + 1) * GATE_COLS] = _sigmoid(gm).astype(BF16)

    ue = jnp.concatenate([carry_ref[...], u], axis=0)
    carry_ref[...] = u[t - HALO:, :]
    pos = si * t + lax.broadcasted_iota(jnp.int32, (t, 1), 0) + 1
    outs = []
    for gi, w in enumerate(POOL_WINDOWS):
        sl = slice(gi * POOL_GROUP, (gi + 1) * POOL_GROUP)
        acc = ue[:, sl]
        step = 1
        while step < w:
            acc = acc + pltpu.roll(acc, step, axis=0)
            step *= 2
        inv = 1.0 / jnp.minimum(pos, w).astype(F32)
        d = acc[HALO:, :] * inv - u[:, sl]
        outs.append(jnp.dot(d.astype(BF16), poolw_ref[gi], preferred_element_type=F32))
    y = jnp.concatenate(outs, axis=-1) * pscale_ref[...]
    ypool_out[0] = (y * (g_pool * _sigmoid(g_pool))).astype(BF16)


def _attn_kernel(qt_ref, k_ref, vt_ref, sga_ref, o_ref, sa_ref, sb_ref, mxa_ref, mxb_ref,
                 m_ref, l_ref, acc_ref):
    tq, tk = ATTN_TQ, ATTN_TK
    nq = HEADS_PER_STEP * tq
    n_qt = qt_ref.shape[2]
    kblocks = tk // CHUNK
    qry_chunk = lax.shift_right_logical(lax.broadcasted_iota(jnp.int32, (1, 1, nq), 2) & (tq - 1),
                                        CHUNK.bit_length() - 1)
    key_block = lax.broadcasted_iota(jnp.int32, (kblocks, 1, 1), 0)

    def score_tile(s_ref, mx_ref, qi, j, mask_for, diagonal=False):
        split = diagonal and tq == tk
        hk, hq, hb = tk // 2, tq // 2, kblocks // 2
        if mask_for is not None:
            allowed = key_block + (j * kblocks - mask_for * (tq // CHUNK)) <= qry_chunk
            bias = jnp.where(allowed, 0.0, -jnp.inf).astype(F32)
        for g in range(PAIRS_PER_STEP):
            for h in range(HEADS_PER_STEP):
                klanes = slice((g * HEADS_PER_STEP + h) * SLAB, (g * HEADS_PER_STEP + h + 1) * SLAB)
                qrows = slice(h * SLAB, (h + 1) * SLAB)
                cols = slice(h * tq, (h + 1) * tq)
                if not split:
                    kt = k_ref[0, pl.ds(pl.multiple_of(j * tk, tk), tk), klanes]
                    s = jnp.dot(kt, qt_ref[0, g, qi, qrows, :], preferred_element_type=F32)
                    if mask_for is not None:
                        s = (s.reshape(kblocks, CHUNK, tq) + bias[:, :, cols]).reshape(tk, tq)
                    s_ref[g, :, cols] = s
                    mx_ref[g, :, cols] = jnp.max(s, axis=0, keepdims=True)
                    continue
                late = slice(h * tq + hq, (h + 1) * tq)
                k_top = k_ref[0, pl.ds(pl.multiple_of(j * tk, tk), hk), klanes]
                k_bot = k_ref[0, pl.ds(pl.multiple_of(j * tk + hk, hk), hk), klanes]
                s = jnp.dot(k_top, qt_ref[0, g, qi, qrows, :], preferred_element_type=F32)
                s = (s.reshape(hb, CHUNK, tq) + bias[:hb, :, cols]).reshape(hk, tq)
                s_ref[g, :hk, cols] = s
                mx = jnp.max(s, axis=0, keepdims=True)
                s = jnp.dot(k_bot, qt_ref[0, g, qi, qrows, hq:], preferred_element_type=F32)
                s = (s.reshape(hb, CHUNK, hq) + bias[hb:, :, late]).reshape(hk, hq)
                s_ref[g, hk:, late] = s
                mx_ref[g, :, cols] = jnp.concatenate(
                    [mx[:, :hq], jnp.maximum(mx[:, hq:], jnp.max(s, axis=0, keepdims=True))], axis=1)

    def process(s_ref, mx_ref, j, diagonal=False):
        split = diagonal and tq == tk
        hk, hq = tk // 2, tq // 2
        for g in range(PAIRS_PER_STEP):
            m_old = m_ref[g]
            m_new = jnp.maximum(m_old, mx_ref[g])
            p = jnp.exp2((s_ref[g, :hk, :] if split else s_ref[g]) - m_new).astype(BF16)
            alpha = jnp.exp2(m_old - m_new)
            m_ref[g] = m_new
            for h in range(HEADS_PER_STEP):
                cols = slice(h * tq, (h + 1) * tq)
                vrows = slice(h * VH_ROWS, (h + 1) * VH_ROWS)
                if split:
                    late = slice(h * tq + hq, (h + 1) * tq)
                    p_late = jnp.exp2(s_ref[g, hk:, late] - m_new[:, late]).astype(BF16)
                    pv = jnp.dot(vt_ref[0, g, j, vrows, :hk], p[:, cols], preferred_element_type=F32)
                    pv_late = jnp.dot(vt_ref[0, g, j, vrows, hk:], p_late, preferred_element_type=F32)
                    pv = jnp.concatenate([pv[:, :hq], pv[:, hq:] + pv_late], axis=1)
                else:
                    pv = jnp.dot(vt_ref[0, g, j, vrows, :], p[:, cols],
                                 preferred_element_type=F32)
                l_ref[g, :, cols] = alpha[:, cols] * l_ref[g, :, cols] + pv[VDIM:VDIM + 1, :]
                acc_ref[g, h] = alpha[:, cols] * acc_ref[g, h] + pv[:VDIM, :]

    score_tile(sa_ref, mxa_ref, 0, 0, 0, diagonal=True)

    def query_tile(i, carry):
        static = isinstance(i, int)
        last = ((i + 1) * tq + tk - 1) // tk - 1
        i_next = min(i + 1, n_qt - 1) if static else jnp.minimum(i + 1, n_qt - 1)
        clamp0 = (lambda v: max(v, 0)) if static else (lambda v: jnp.maximum(v, 0))
        below_mask = None if tq == tk else i
        next_mask = None if tq == tk else i_next

        def pair(j, scores_last):
            score_tile(sb_ref, mxb_ref, i, j + 1, below_mask)
            process(sa_ref, mxa_ref, j)
            score_tile(sa_ref, mxa_ref, i, j + 2, i if scores_last else below_mask, diagonal=scores_last)
            process(sb_ref, mxb_ref, j + 1)

        def close_even(n):
            for q in range(n):
                pair(clamp0(last - 2 * (n - q)), q == n - 1)
            process(sa_ref, mxa_ref, last, diagonal=True)
            score_tile(sa_ref, mxa_ref, i_next, 0, next_mask)
            finish(i)

        def close_odd(n):
            for q in range(n):
                pair(clamp0(last - 1 - 2 * (n - q)), False)
            score_tile(sb_ref, mxb_ref, i, last, i, diagonal=True)
            process(sa_ref, mxa_ref, clamp0(last - 1))
            score_tile(sa_ref, mxa_ref, i_next, 0, next_mask)
            process(sb_ref, mxb_ref, last, diagonal=True)
            finish(i)

        if static:
            n = min(last // 2, CLOSING_PAIRS)
            for jj in range(last // 2 - n):
                pair(2 * jj, False)
            (close_even if last % 2 == 0 else close_odd)(n)
            return carry

        even = last % 2 == 0
        peeled = jnp.minimum(last // 2, CLOSING_PAIRS)

        def pair_body(jj, c):
            pair(2 * jj, False)
            return c

        lax.fori_loop(0, last // 2 - peeled, pair_body, 0)
        for n in range(CLOSING_PAIRS + 1):
            pl.when(even & (peeled == n))(functools.partial(close_even, n))
            pl.when(~even & (peeled == n))(functools.partial(close_odd, n))
        return carry

    def reset():
        m_ref[...] = jnp.full_like(m_ref, -jnp.inf)
        l_ref[...] = jnp.zeros_like(l_ref)
        acc_ref[...] = jnp.zeros_like(acc_ref)

    def finish(i):
        rows = pl.ds(pl.multiple_of(i * tq, tq), tq)
        for g in range(PAIRS_PER_STEP):
            inv_l = 1.0 / l_ref[g]
            ot = jnp.concatenate([acc_ref[g, h] * inv_l[:, h * tq:(h + 1) * tq]
                                  for h in range(HEADS_PER_STEP)], axis=0)
            cols = slice(g * LANES, (g + 1) * LANES)
            o_ref[0, rows, cols] = (ot.T * sga_ref[0, rows, cols].astype(F32)).astype(BF16)
        reset()

    reset()
    n_static = min(STATIC_QUERY_TILES, n_qt)
    for i in range(n_static):
        query_tile(i, 0)
    lax.fori_loop(n_static, n_qt, query_tile, 0)


def _merge_kernel(x_ref, ya_ref, ypool_ref, gates_ref, wa_ref, wp_ref, wo_ref, nf_ref, out_ref):
    a = jnp.dot(ya_ref[0], wa_ref[...], preferred_element_type=F32)
    p = jnp.dot(ypool_ref[0], wp_ref[...], preferred_element_type=F32)
    ga = gates_ref[0, :, :D_MODEL].astype(F32)
    gp = gates_ref[0, :, D_MODEL:].astype(F32)
    merged = (ga * a + gp * p).astype(BF16)
    h = x_ref[0] + jnp.dot(merged, wo_ref[...], preferred_element_type=F32)
    out_ref[0] = _rms(h, nf_ref[...])


def _rope_tables(seq):
    half = ROPE // 2
    inv_freq = ROPE_THETA ** (-jnp.arange(half, dtype=F32) / half)
    ang = jnp.arange(seq, dtype=F32)[None, :] * inv_freq[:, None]
    return jnp.cos(ang), jnp.sin(ang)


def _const_spec(shape):
    return pl.BlockSpec(shape, lambda *_: (0,) * len(shape), pipeline_mode=pl.Buffered(1))


def kernel(x, norm_in, w_in, q_norm, w_uq, kv_norm, w_ukv, pool_w, pool_scale,
           w_branch_attn, w_branch_pool, w_out, norm_final):
    b, s, d = x.shape
    tq, tk, t1, t3 = ATTN_TQ, ATTN_TK, PROJ_TILE, MERGE_TILE
    assert d == D_MODEL and s % t1 == 0 and t1 % tk == 0 and t1 % tq == 0 and s % t3 == 0
    hp = HEADS // HEADS_PER_STEP
    nq = HEADS_PER_STEP * tq
    g2 = PAIRS_PER_STEP

    n_gate = w_in.shape[1] - (Q_RANK + KV_RANK + ROPE)
    n_prep = d // PREP_ROWS
    rows_spec = lambda rows: pl.BlockSpec((rows // n_prep, d), lambda i: (i, 0))
    w_a, w_g, w_ba16, w_bp16, w_o16 = pl.pallas_call(
        _weight_prep_kernel,
        grid=(n_prep,),
        in_specs=[pl.BlockSpec((w_in.shape[1], PREP_ROWS), lambda i: (0, i)),
                  rows_spec(MLA_WIDTH), rows_spec(POOL_WIDTH), rows_spec(d)],
        out_specs=[pl.BlockSpec((PREP_ROWS, LATENT_COLS), lambda i: (i, 0)),
                   pl.BlockSpec((PREP_ROWS, n_gate), lambda i: (i, 0)),
                   rows_spec(MLA_WIDTH), rows_spec(POOL_WIDTH), rows_spec(d)],
        out_shape=[jax.ShapeDtypeStruct((d, LATENT_COLS), BF16),
                   jax.ShapeDtypeStruct((d, n_gate), BF16),
                   jax.ShapeDtypeStruct((MLA_WIDTH, d), BF16),
                   jax.ShapeDtypeStruct((POOL_WIDTH, d), BF16),
                   jax.ShapeDtypeStruct((d, d), BF16)],
        name="weight_prep",
    )(w_in.T, w_branch_attn, w_branch_pool, w_out)
    w_uq_p = jnp.pad(w_uq, ((0, 0), (0, 0), (0, SLAB - NOPE - ROPE)))
    w_uqt = w_uq_p.reshape(Q_RANK, HEADS * SLAB).T.astype(BF16)
    w_uk_p = jnp.pad(w_ukv[:, :, :NOPE], ((0, 0), (0, 0), (0, SLAB - NOPE)))
    w_uk = w_uk_p.reshape(KV_RANK, HEADS * SLAB).astype(BF16)
    w_uvt = w_ukv[:, :, NOPE:].reshape(KV_RANK, MLA_WIDTH).T.astype(BF16)
    cos_t, sin_t = _rope_tables(s)

    row = lambda v: v.reshape(1, -1).astype(F32)
    tok = lambda width: pl.BlockSpec((1, t1, width), lambda bi, si: (bi, si, 0))
    tabt = pl.BlockSpec((ROPE // 2, t1), lambda bi, si: (0, si))
    qt, k, vt, sga, ypool, gates = pl.pallas_call(
        _proj_kernel,
        grid=(b, s // t1),
        in_specs=[tok(d), _const_spec((1, d)), _const_spec(w_a.shape), _const_spec(w_g.shape),
                  _const_spec((1, Q_RANK)), _const_spec(w_uqt.shape), _const_spec((1, KV_RANK)),
                  _const_spec(w_uk.shape), _const_spec(w_uvt.shape),
                  _const_spec(pool_w.shape), _const_spec((1, POOL_WIDTH)),
                  tabt, tabt],
        out_specs=[pl.BlockSpec((1, hp, t1 // tq, HEADS_PER_STEP * SLAB, tq),
                                lambda bi, si: (bi, 0, si, 0, 0)),
                   tok(HEADS * SLAB),
                   pl.BlockSpec((1, hp, t1 // tk, VT_ROWS, tk), lambda bi, si: (bi, 0, si, 0, 0)),
                   tok(MLA_WIDTH), tok(POOL_WIDTH), tok(2 * d)],
        out_shape=[jax.ShapeDtypeStruct((b, hp, s // tq, HEADS_PER_STEP * SLAB, tq), BF16),
                   jax.ShapeDtypeStruct((b, s, HEADS * SLAB), BF16),
                   jax.ShapeDtypeStruct((b, hp, s // tk, VT_ROWS, tk), BF16),
                   jax.ShapeDtypeStruct((b, s, MLA_WIDTH), BF16),
                   jax.ShapeDtypeStruct((b, s, POOL_WIDTH), BF16),
                   jax.ShapeDtypeStruct((b, s, 2 * d), BF16)],
        scratch_shapes=[pltpu.VMEM((HALO, POOL_WIDTH), F32)],
        compiler_params=pltpu.CompilerParams(
            dimension_semantics=("arbitrary", "arbitrary"), vmem_limit_bytes=VMEM_LIMIT),
        name="proj",
    )(x, row(norm_in), w_a, w_g, row(q_norm), w_uqt, row(kv_norm), w_uk, w_uvt,
      pool_w.astype(BF16), row(pool_scale), cos_t, sin_t)

    o = pl.pallas_call(
        _attn_kernel,
        grid=(b, hp // g2),
        in_specs=[pl.BlockSpec((1, g2, s // tq, HEADS_PER_STEP * SLAB, tq), lambda bi, pi: (bi, pi, 0, 0, 0)),
                  pl.BlockSpec((1, s, g2 * HEADS_PER_STEP * SLAB), lambda bi, pi: (bi, 0, pi)),
                  pl.BlockSpec((1, g2, s // tk, VT_ROWS, tk), lambda bi, pi: (bi, pi, 0, 0, 0)),
                  pl.BlockSpec((1, s, g2 * LANES), lambda bi, pi: (bi, 0, pi))],
        out_specs=pl.BlockSpec((1, s, g2 * LANES), lambda bi, pi: (bi, 0, pi)),
        out_shape=jax.ShapeDtypeStruct((b, s, MLA_WIDTH), BF16),
        scratch_shapes=[pltpu.VMEM((g2, tk, nq), F32), pltpu.VMEM((g2, tk, nq), F32),
                        pltpu.VMEM((g2, 1, nq), F32), pltpu.VMEM((g2, 1, nq), F32),
                        pltpu.VMEM((g2, 1, nq), F32), pltpu.VMEM((g2, 1, nq), F32),
                        pltpu.VMEM((g2, HEADS_PER_STEP, VDIM, tq), F32)],
        compiler_params=pltpu.CompilerParams(
            dimension_semantics=("arbitrary", "arbitrary"), vmem_limit_bytes=VMEM_LIMIT),
        name="attn",
    )(qt, k, vt, sga)

    tok3 = lambda width: pl.BlockSpec((1, t3, width), lambda bi, si: (bi, si, 0))
    out = pl.pallas_call(
        _merge_kernel,
        grid=(b, s // t3),
        in_specs=[tok3(d), tok3(MLA_WIDTH), tok3(POOL_WIDTH), tok3(2 * d),
                  _const_spec((MLA_WIDTH, d)), _const_spec((POOL_WIDTH, d)), _const_spec((d, d)),
                  _const_spec((1, d))],
        out_specs=tok3(d),
        out_shape=jax.ShapeDtypeStruct((b, s, d), x.dtype),
        compiler_params=pltpu.CompilerParams(
            dimension_semantics=("arbitrary", "arbitrary"), vmem_limit_bytes=VMEM_LIMIT),
        name="merge",
    )(x, o, ypool, gates, w_ba16, w_bp16, w_o16, row(norm_final))
    return out
```

```python
import functools

import jax
import jax.numpy as jnp
from jax import lax
from jax.experimental import pallas as pl
from jax.experimental.pallas import tpu as pltpu

D_MODEL = 1024
CHUNK = 64
HEADS = 8
NOPE = 64
ROPE = 32
VDIM = 64
Q_RANK = 384
KV_RANK = 256
MLA_WIDTH = HEADS * VDIM
ROPE_THETA = 10000.0
POOL_WINDOWS = (2, 4, 8, 16)
POOL_WIDTH = D_MODEL // 2
POOL_GROUP = POOL_WIDTH // len(POOL_WINDOWS)
EPS = 1e-6
QK_SCALE = (NOPE + ROPE) ** -0.5 * 1.4426950408889634

LANES = 128
HALO = 16
SLAB = LANES
LATENT_COLS = Q_RANK + KV_RANK + SLAB
PREP_ROWS = 256
HEADS_PER_STEP = 2
PAIRS_PER_STEP = 2
ONES_ROWS = 16
VH_ROWS = VDIM + ONES_ROWS
VT_ROWS = HEADS_PER_STEP * VH_ROWS
PROJ_TILE = 512
GATE_COLS = 512
ATTN_TQ = 512
ATTN_TK = 512
CLOSING_PAIRS = 2
MERGE_TILE = 1024
VMEM_LIMIT = 56 * 1024 * 1024

BF16 = jnp.bfloat16
F32 = jnp.float32
NT_DIMS = (((1,), (1,)), ((), ()))


def _rms(x, g):
    return x * lax.rsqrt(jnp.mean(x * x, axis=-1, keepdims=True) + EPS) * g


def _sigmoid(x):
    return 1.0 / (1.0 + jnp.exp(-x))


def _weight_prep_kernel(wt_ref, wba_ref, wbp_ref, wo_ref, wuq_ref, wukv_ref, pw_ref,
                        wa_ref, wg_ref, wba_out, wbp_out, wo_out, wuqt_out, wuk_out, wuvt_out, pw_out):
    @pl.when(pl.program_id(0) == 0)
    def _():
        pw_out[...] = pw_ref[...].astype(BF16)
        qk = NOPE + ROPE
        wuq_t = wuq_ref[...].T
        wukv = wukv_ref[...]
        lane = lax.broadcasted_iota(jnp.int32, wukv.shape, 1)
        wuk_out[...] = jnp.where(lane % (NOPE + VDIM) < NOPE, wukv, 0.0).astype(BF16)
        wukv_t = wukv.T
        for h in range(HEADS):
            wuqt_out[h * SLAB:h * SLAB + qk, :] = wuq_t[h * qk:(h + 1) * qk].astype(BF16)
            wuqt_out[h * SLAB + qk:(h + 1) * SLAB, :] = jnp.zeros((SLAB - qk, Q_RANK), BF16)
            wuvt_out[h * VDIM:(h + 1) * VDIM, :] = (
                wukv_t[h * (NOPE + VDIM) + NOPE:(h + 1) * (NOPE + VDIM)].astype(BF16))

    wba_out[...] = wba_ref[...].astype(BF16)
    wbp_out[...] = wbp_ref[...].astype(BF16)
    wo_out[...] = wo_ref[...].astype(BF16)
    o_kr = Q_RANK + KV_RANK
    wa_ref[:, :o_kr] = wt_ref[:o_kr, :].T.astype(BF16)
    kr = jnp.concatenate([jnp.zeros((NOPE, PREP_ROWS), F32), wt_ref[o_kr:o_kr + ROPE, :],
                          jnp.zeros((SLAB - NOPE - ROPE, PREP_ROWS), F32)], axis=0)
    wa_ref[:, o_kr:] = kr.T.astype(BF16)
    wg_ref[...] = wt_ref[o_kr + ROPE:, :].T.astype(BF16)


def _proj_kernel(x_ref, nin_ref, wa_ref, wg_ref, qn_ref, wuqt_ref, kvn_ref, wuk_ref, wuvt_ref,
                 poolw_ref, pscale_ref, cos_ref, sin_ref,
                 qt_out, k_out, vt_out, sga_out, ypool_out, gates_out, carry_ref):
    t = PROJ_TILE
    si = pl.program_id(1)

    @pl.when(si == 0)
    def _():
        carry_ref[...] = jnp.zeros_like(carry_ref)

    x = x_ref[0]
    hn = _rms(x, nin_ref[...]).astype(BF16)

    za = jnp.dot(hn, wa_ref[...], preferred_element_type=F32)
    zq = za[:, :Q_RANK]
    zkv = za[:, Q_RANK:Q_RANK + KV_RANK]
    zkr = za[:, Q_RANK + KV_RANK:]
    cq = _rms(zq, qn_ref[...]).astype(BF16)
    ckv = _rms(zkv, kvn_ref[...]).astype(BF16)

    qt = lax.dot_general(wuqt_ref[...], cq, NT_DIMS, preferred_element_type=F32)
    scale = QK_SCALE
    cos, sin = cos_ref[...], sin_ref[...]
    qcos, qsin = cos * scale, sin * scale
    half = ROPE // 2
    tq = ATTN_TQ
    for h in range(HEADS):
        r0 = h * SLAB
        x1 = qt[r0 + NOPE:r0 + NOPE + half]
        x2 = qt[r0 + NOPE + half:r0 + NOPE + ROPE]
        slab = jnp.concatenate([qt[r0:r0 + NOPE] * scale, x1 * qcos - x2 * qsin, x1 * qsin + x2 * qcos,
                                jnp.zeros((SLAB - NOPE - ROPE, t), F32)], axis=0).astype(BF16)
        pr, hs = h // HEADS_PER_STEP, h % HEADS_PER_STEP
        for c in range(t // tq):
            qt_out[0, pr, c, hs * SLAB:(hs + 1) * SLAB, :] = slab[:, c * tq:(c + 1) * tq]

    kf = jnp.dot(ckv, wuk_ref[...], preferred_element_type=F32)
    zt = zkr.T
    k1, k2 = zt[NOPE:NOPE + half], zt[NOPE + half:NOPE + ROPE]
    kr = jnp.concatenate([jnp.zeros((NOPE, t), F32), k1 * cos - k2 * sin, k1 * sin + k2 * cos,
                          jnp.zeros((SLAB - NOPE - ROPE, t), F32)], axis=0).T
    for h in range(HEADS):
        sl = slice(h * SLAB, (h + 1) * SLAB)
        k_out[0, :, sl] = (kf[:, sl] + kr).astype(BF16)

    vt = lax.dot_general(wuvt_ref[...], ckv, NT_DIMS, preferred_element_type=F32)
    for c in range(t // ATTN_TK):
        cols = slice(c * ATTN_TK, (c + 1) * ATTN_TK)
        for h in range(HEADS):
            pr, r0 = h // HEADS_PER_STEP, (h % HEADS_PER_STEP) * VH_ROWS
            vt_out[0, pr, c, r0:r0 + VDIM, :] = vt[h * VDIM:(h + 1) * VDIM, cols].astype(BF16)
            vt_out[0, pr, c, r0 + VDIM:r0 + VH_ROWS, :] = jnp.ones((ONES_ROWS, ATTN_TK), BF16)

    zg = jnp.dot(hn, wg_ref[:, :3 * POOL_WIDTH], preferred_element_type=F32)
    g_attn = zg[:, :MLA_WIDTH]
    u = zg[:, MLA_WIDTH:MLA_WIDTH + POOL_WIDTH]
    g_pool = zg[:, MLA_WIDTH + POOL_WIDTH:]
    sga_out[0] = (g_attn * _sigmoid(g_attn)).astype(BF16)
    for c in range(2 * D_MODEL // GATE_COLS):
        lo = 3 * POOL_WIDTH + c * GATE_COLS
        gm = jnp.dot(hn, wg_ref[:, lo:lo + GATE_COLS], preferred_element_type=F32)
        gates_out[0, :, c * GATE_COLS:(c + 1) * GATE_COLS] = _sigmoid(gm).astype(BF16)

    ue = jnp.concatenate([carry_ref[...], u], axis=0)
    carry_ref[...] = u[t - HALO:, :]
    pos = si * t + lax.broadcasted_iota(jnp.int32, (t, 1), 0) + 1
    outs = []
    for gi, w in enumerate(POOL_WINDOWS):
        sl = slice(gi * POOL_GROUP, (gi + 1) * POOL_GROUP)
        acc = ue[:, sl]
        step = 1
        while step < w:
            acc = acc + pltpu.roll(acc, step, axis=0)
            step *= 2
        inv = 1.0 / jnp.minimum(pos, w).astype(F32)
        d = acc[HALO:, :] * inv - u[:, sl]
        outs.append(jnp.dot(d.astype(BF16), poolw_ref[gi], preferred_element_type=F32))
    y = jnp.concatenate(outs, axis=-1) * pscale_ref[...]
    ypool_out[0] = (y * (g_pool * _sigmoid(g_pool))).astype(BF16)


def _attn_kernel(qt_ref, k_ref, vt_ref, sga_ref, o_ref, sa_ref, sb_ref, mxa_ref, mxb_ref,
                 m_ref, l_ref, acc_ref):
    tq, tk = ATTN_TQ, ATTN_TK
    nq = HEADS_PER_STEP * tq
    n_qt = qt_ref.shape[2]
    kblocks = tk // CHUNK
    qry_chunk = lax.shift_right_logical(lax.broadcasted_iota(jnp.int32, (1, 1, nq), 2) & (tq - 1),
                                        CHUNK.bit_length() - 1)
    key_block = lax.broadcasted_iota(jnp.int32, (kblocks, 1, 1), 0)

    def score_tile(s_ref, mx_ref, qi, j, mask_for, diagonal=False):
        split = diagonal and tq == tk
        hk, hq, hb = tk // 2, tq // 2, kblocks // 2
        if mask_for is not None:
            allowed = key_block + (j * kblocks - mask_for * (tq // CHUNK)) <= qry_chunk
            bias = jnp.where(allowed, 0.0, -jnp.inf).astype(F32)
        for g in range(PAIRS_PER_STEP):
            for h in range(HEADS_PER_STEP):
                klanes = slice((g * HEADS_PER_STEP + h) * SLAB, (g * HEADS_PER_STEP + h + 1) * SLAB)
                qrows = slice(h * SLAB, (h + 1) * SLAB)
                cols = slice(h * tq, (h + 1) * tq)
                if not split:
                    kt = k_ref[0, pl.ds(pl.multiple_of(j * tk, tk), tk), klanes]
                    s = jnp.dot(kt, qt_ref[0, g, qi, qrows, :], preferred_element_type=F32)
                    if mask_for is not None:
                        s = (s.reshape(kblocks, CHUNK, tq) + bias[:, :, cols]).reshape(tk, tq)
                    s_ref[g, :, cols] = s
                    mx_ref[g, :, cols] = jnp.max(s, axis=0, keepdims=True)
                    continue
                late = slice(h * tq + hq, (h + 1) * tq)
                k_top = k_ref[0, pl.ds(pl.multiple_of(j * tk, tk), hk), klanes]
                k_bot = k_ref[0, pl.ds(pl.multiple_of(j * tk + hk, hk), hk), klanes]
                s = jnp.dot(k_top, qt_ref[0, g, qi, qrows, :], preferred_element_type=F32)
                s = (s.reshape(hb, CHUNK, tq) + bias[:hb, :, cols]).reshape(hk, tq)
                s_ref[g, :hk, cols] = s
                mx = jnp.max(s, axis=0, keepdims=True)
                s = jnp.dot(k_bot, qt_ref[0, g, qi, qrows, hq:], preferred_element_type=F32)
                s = (s.reshape(hb, CHUNK, hq) + bias[hb:, :, late]).reshape(hk, hq)
                s_ref[g, hk:, late] = s
                mx_ref[g, :, cols] = jnp.concatenate(
                    [mx[:, :hq], jnp.maximum(mx[:, hq:], jnp.max(s, axis=0, keepdims=True))], axis=1)

    def process(s_ref, mx_ref, j, diagonal=False):
        split = diagonal and tq == tk
        hk, hq = tk // 2, tq // 2
        for g in range(PAIRS_PER_STEP):
            m_old = m_ref[g]
            m_new = jnp.maximum(m_old, mx_ref[g])
            p = jnp.exp2((s_ref[g, :hk, :] if split else s_ref[g]) - m_new).astype(BF16)
            alpha = jnp.exp2(m_old - m_new)
            m_ref[g] = m_new
            for h in range(HEADS_PER_STEP):
                cols = slice(h * tq, (h + 1) * tq)
                vrows = slice(h * VH_ROWS, (h + 1) * VH_ROWS)
                if split:
                    late = slice(h * tq + hq, (h + 1) * tq)
                    p_late = jnp.exp2(s_ref[g, hk:, late] - m_new[:, late]).astype(BF16)
                    pv = jnp.dot(vt_ref[0, g, j, vrows, :hk], p[:, cols], preferred_element_type=F32)
                    pv_late = jnp.dot(vt_ref[0, g, j, vrows, hk:], p_late, preferred_element_type=F32)
                    pv = jnp.concatenate([pv[:, :hq], pv[:, hq:] + pv_late], axis=1)
                else:
                    pv = jnp.dot(vt_ref[0, g, j, vrows, :], p[:, cols],
                                 preferred_element_type=F32)
                l_ref[g, :, cols] = alpha[:, cols] * l_ref[g, :, cols] + pv[VDIM:VDIM + 1, :]
                acc_ref[g, h] = alpha[:, cols] * acc_ref[g, h] + pv[:VDIM, :]

    score_tile(sa_ref, mxa_ref, 0, 0, 0, diagonal=True)

    def query_tile(i, carry):
        last = ((i + 1) * tq + tk - 1) // tk - 1
        i_next = jnp.minimum(i + 1, n_qt - 1)
        below_mask = None if tq == tk else i
        next_mask = None if tq == tk else i_next

        def pair(j, scores_last):
            score_tile(sb_ref, mxb_ref, i, j + 1, below_mask)
            process(sa_ref, mxa_ref, j)
            score_tile(sa_ref, mxa_ref, i, j + 2, i if scores_last else below_mask, diagonal=scores_last)
            process(sb_ref, mxb_ref, j + 1)

        even = last % 2 == 0
        peeled = jnp.minimum(last // 2, CLOSING_PAIRS)

        def pair_body(jj, c):
            pair(2 * jj, False)
            return c

        lax.fori_loop(0, last // 2 - peeled, pair_body, 0)

        def close_even(n):
            for q in range(n):
                pair(jnp.maximum(last - 2 * (n - q), 0), q == n - 1)
            process(sa_ref, mxa_ref, last, diagonal=True)
            score_tile(sa_ref, mxa_ref, i_next, 0, next_mask)
            finish(i)

        def close_odd(n):
            for q in range(n):
                pair(jnp.maximum(last - 1 - 2 * (n - q), 0), False)
            score_tile(sb_ref, mxb_ref, i, last, i, diagonal=True)
            process(sa_ref, mxa_ref, jnp.maximum(last - 1, 0))
            score_tile(sa_ref, mxa_ref, i_next, 0, next_mask)
            process(sb_ref, mxb_ref, last, diagonal=True)
            finish(i)

        for n in range(CLOSING_PAIRS + 1):
            pl.when(even & (peeled == n))(functools.partial(close_even, n))
            pl.when(~even & (peeled == n))(functools.partial(close_odd, n))

        return carry

    def reset():
        m_ref[...] = jnp.full_like(m_ref, -jnp.inf)
        l_ref[...] = jnp.zeros_like(l_ref)
        acc_ref[...] = jnp.zeros_like(acc_ref)

    def finish(i):
        rows = pl.ds(pl.multiple_of(i * tq, tq), tq)
        for g in range(PAIRS_PER_STEP):
            inv_l = 1.0 / l_ref[g]
            ot = jnp.concatenate([acc_ref[g, h] * inv_l[:, h * tq:(h + 1) * tq]
                                  for h in range(HEADS_PER_STEP)], axis=0)
            cols = slice(g * LANES, (g + 1) * LANES)
            o_ref[0, rows, cols] = (ot.T * sga_ref[0, rows, cols].astype(F32)).astype(BF16)
        reset()

    reset()
    lax.fori_loop(0, n_qt, query_tile, 0)


def _merge_kernel(x_ref, ya_ref, ypool_ref, gates_ref, wa_ref, wp_ref, wo_ref, nf_ref, out_ref):
    a = jnp.dot(ya_ref[0], wa_ref[...], preferred_element_type=F32)
    p = jnp.dot(ypool_ref[0], wp_ref[...], preferred_element_type=F32)
    ga = gates_ref[0, :, :D_MODEL].astype(F32)
    gp = gates_ref[0, :, D_MODEL:].astype(F32)
    merged = (ga * a + gp * p).astype(BF16)
    h = x_ref[0] + jnp.dot(merged, wo_ref[...], preferred_element_type=F32)
    out_ref[0] = _rms(h, nf_ref[...])


def _rope_tables(seq):
    half = ROPE // 2
    inv_freq = ROPE_THETA ** (-jnp.arange(half, dtype=F32) / half)
    ang = jnp.arange(seq, dtype=F32)[None, :] * inv_freq[:, None]
    return jnp.cos(ang), jnp.sin(ang)


def _const_spec(shape):
    return pl.BlockSpec(shape, lambda *_: (0,) * len(shape), pipeline_mode=pl.Buffered(1))


def kernel(x, norm_in, w_in, q_norm, w_uq, kv_norm, w_ukv, pool_w, pool_scale,
           w_branch_attn, w_branch_pool, w_out, norm_final):
    b, s, d = x.shape
    tq, tk, t1, t3 = ATTN_TQ, ATTN_TK, PROJ_TILE, MERGE_TILE
    assert d == D_MODEL and s % t1 == 0 and t1 % tk == 0 and t1 % tq == 0 and s % t3 == 0
    hp = HEADS // HEADS_PER_STEP
    nq = HEADS_PER_STEP * tq
    g2 = PAIRS_PER_STEP

    n_gate = w_in.shape[1] - (Q_RANK + KV_RANK + ROPE)
    n_prep = d // PREP_ROWS
    rows_spec = lambda rows: pl.BlockSpec((rows // n_prep, d), lambda i: (i, 0))
    assert NOPE + VDIM == SLAB
    whole = lambda shape: pl.BlockSpec(shape, lambda i: (0,) * len(shape))
    small = [(HEADS * SLAB, Q_RANK), (KV_RANK, HEADS * SLAB), (MLA_WIDTH, KV_RANK), pool_w.shape]
    w_a, w_g, w_ba16, w_bp16, w_o16, w_uqt, w_uk, w_uvt, pool_w16 = pl.pallas_call(
        _weight_prep_kernel,
        grid=(n_prep,),
        in_specs=[pl.BlockSpec((w_in.shape[1], PREP_ROWS), lambda i: (0, i)),
                  rows_spec(MLA_WIDTH), rows_spec(POOL_WIDTH), rows_spec(d),
                  whole((Q_RANK, HEADS * (NOPE + ROPE))), whole((KV_RANK, HEADS * SLAB)),
                  whole(pool_w.shape)],
        out_specs=[pl.BlockSpec((PREP_ROWS, LATENT_COLS), lambda i: (i, 0)),
                   pl.BlockSpec((PREP_ROWS, n_gate), lambda i: (i, 0)),
                   rows_spec(MLA_WIDTH), rows_spec(POOL_WIDTH), rows_spec(d)]
                  + [whole(shape) for shape in small],
        out_shape=[jax.ShapeDtypeStruct((d, LATENT_COLS), BF16),
                   jax.ShapeDtypeStruct((d, n_gate), BF16),
                   jax.ShapeDtypeStruct((MLA_WIDTH, d), BF16),
                   jax.ShapeDtypeStruct((POOL_WIDTH, d), BF16),
                   jax.ShapeDtypeStruct((d, d), BF16)]
                  + [jax.ShapeDtypeStruct(shape, BF16) for shape in small],
        name="weight_prep",
    )(w_in.T, w_branch_attn, w_branch_pool, w_out,
      w_uq.reshape(Q_RANK, HEADS * (NOPE + ROPE)), w_ukv.reshape(KV_RANK, HEADS * SLAB), pool_w)
    cos_t, sin_t = _rope_tables(s)

    row = lambda v: v.reshape(1, -1).astype(F32)
    tok = lambda width: pl.BlockSpec((1, t1, width), lambda bi, si: (bi, si, 0))
    tabt = pl.BlockSpec((ROPE // 2, t1), lambda bi, si: (0, si))
    qt, k, vt, sga, ypool, gates = pl.pallas_call(
        _proj_kernel,
        grid=(b, s // t1),
        in_specs=[tok(d), _const_spec((1, d)), _const_spec(w_a.shape), _const_spec(w_g.shape),
                  _const_spec((1, Q_RANK)), _const_spec(w_uqt.shape), _const_spec((1, KV_RANK)),
                  _const_spec(w_uk.shape), _const_spec(w_uvt.shape),
                  _const_spec(pool_w.shape), _const_spec((1, POOL_WIDTH)),
                  tabt, tabt],
        out_specs=[pl.BlockSpec((1, hp, t1 // tq, HEADS_PER_STEP * SLAB, tq),
                                lambda bi, si: (bi, 0, si, 0, 0)),
                   tok(HEADS * SLAB),
                   pl.BlockSpec((1, hp, t1 // tk, VT_ROWS, tk), lambda bi, si: (bi, 0, si, 0, 0)),
                   tok(MLA_WIDTH), tok(POOL_WIDTH), tok(2 * d)],
        out_shape=[jax.ShapeDtypeStruct((b, hp, s // tq, HEADS_PER_STEP * SLAB, tq), BF16),
                   jax.ShapeDtypeStruct((b, s, HEADS * SLAB), BF16),
                   jax.ShapeDtypeStruct((b, hp, s // tk, VT_ROWS, tk), BF16),
                   jax.ShapeDtypeStruct((b, s, MLA_WIDTH), BF16),
                   jax.ShapeDtypeStruct((b, s, POOL_WIDTH), BF16),
                   jax.ShapeDtypeStruct((b, s, 2 * d), BF16)],
        scratch_shapes=[pltpu.VMEM((HALO, POOL_WIDTH), F32)],
        compiler_params=pltpu.CompilerParams(
            dimension_semantics=("arbitrary", "arbitrary"), vmem_limit_bytes=VMEM_LIMIT),
        name="proj",
    )(x, row(norm_in), w_a, w_g, row(q_norm), w_uqt, row(kv_norm), w_uk, w_uvt,
      pool_w16, row(pool_scale), cos_t, sin_t)

    o = pl.pallas_call(
        _attn_kernel,
        grid=(b, hp // g2),
        in_specs=[pl.BlockSpec((1, g2, s // tq, HEADS_PER_STEP * SLAB, tq), lambda bi, pi: (bi, pi, 0, 0, 0)),
                  pl.BlockSpec((1, s, g2 * HEADS_PER_STEP * SLAB), lambda bi, pi: (bi, 0, pi)),
                  pl.BlockSpec((1, g2, s // tk, VT_ROWS, tk), lambda bi, pi: (bi, pi, 0, 0, 0)),
                  pl.BlockSpec((1, s, g2 * LANES), lambda bi, pi: (bi, 0, pi))],
        out_specs=pl.BlockSpec((1, s, g2 * LANES), lambda bi, pi: (bi, 0, pi)),
        out_shape=jax.ShapeDtypeStruct((b, s, MLA_WIDTH), BF16),
        scratch_shapes=[pltpu.VMEM((g2, tk, nq), F32), pltpu.VMEM((g2, tk, nq), F32),
                        pltpu.VMEM((g2, 1, nq), F32), pltpu.VMEM((g2, 1, nq), F32),
                        pltpu.VMEM((g2, 1, nq), F32), pltpu.VMEM((g2, 1, nq), F32),
                        pltpu.VMEM((g2, HEADS_PER_STEP, VDIM, tq), F32)],
        compiler_params=pltpu.CompilerParams(
            dimension_semantics=("arbitrary", "arbitrary"), vmem_limit_bytes=VMEM_LIMIT),
        name="attn",
    )(qt, k, vt, sga)

    tok3 = lambda width: pl.BlockSpec((1, t3, width), lambda bi, si: (bi, si, 0))
    out = pl.pallas_call(
        _merge_kernel,
        grid=(b, s // t3),
        in_specs=[tok3(d), tok3(MLA_WIDTH), tok3(POOL_WIDTH), tok3(2 * d),
                  _const_spec((MLA_WIDTH, d)), _const_spec((POOL_WIDTH, d)), _const_spec((d, d)),
                  _const_spec((1, d))],
        out_specs=tok3(d),
        out_shape=jax.ShapeDtypeStruct((b, s, d), x.dtype),
        compiler_params=pltpu.CompilerParams(
            dimension_semantics=("arbitrary", "arbitrary"), vmem_limit_bytes=VMEM_LIMIT),
        name="merge",
    )(x, o, ypool, gates, w_ba16, w_bp16, w_o16, row(norm_final))
    return out
```

```python
import functools

import jax
import jax.numpy as jnp
from jax import lax
from jax.experimental import pallas as pl
from jax.experimental.pallas import tpu as pltpu

D_MODEL = 1024
CHUNK = 64
HEADS = 8
NOPE = 64
ROPE = 32
VDIM = 64
Q_RANK = 384
KV_RANK = 256
MLA_WIDTH = HEADS * VDIM
ROPE_THETA = 10000.0
POOL_WINDOWS = (2, 4, 8, 16)
POOL_WIDTH = D_MODEL // 2
POOL_GROUP = POOL_WIDTH // len(POOL_WINDOWS)
EPS = 1e-6
QK_SCALE = (NOPE + ROPE) ** -0.5 * 1.4426950408889634

LANES = 128
HALO = 16
SLAB = LANES
LATENT_COLS = Q_RANK + KV_RANK + SLAB
PREP_ROWS = 256
HEADS_PER_STEP = 2
PAIRS_PER_STEP = 2
ONES_ROWS = 16
VH_ROWS = VDIM + ONES_ROWS
VT_ROWS = HEADS_PER_STEP * VH_ROWS
PROJ_TILE = 512
GATE_COLS = 512
ATTN_TQ = 512
ATTN_TK = 512
CLOSING_PAIRS = 2
MERGE_TILE = 1024
VMEM_LIMIT = 56 * 1024 * 1024

BF16 = jnp.bfloat16
F32 = jnp.float32
NT_DIMS = (((1,), (1,)), ((), ()))


def _rms(x, g):
    return x * lax.rsqrt(jnp.mean(x * x, axis=-1, keepdims=True) + EPS) * g


def _sigmoid(x):
    return 1.0 / (1.0 + jnp.exp(-x))


def _weight_prep_kernel(wt_ref, wba_ref, wbp_ref, wo_ref, wuq_ref, wukv_ref, pw_ref,
                        wa_ref, wg_ref, wba_out, wbp_out, wo_out, wuqt_out, wuk_out, wuvt_out, pw_out):
    @pl.when(pl.program_id(0) == 0)
    def _():
        pw_out[...] = pw_ref[...].astype(BF16)
        qk = NOPE + ROPE
        lane = lax.broadcasted_iota(jnp.int32, (KV_RANK, SLAB), 1)
        for h in range(HEADS):
            wuq_h = jnp.concatenate([wuq_ref[:, h, :], jnp.zeros((Q_RANK, SLAB - qk), F32)], axis=1)
            wuqt_out[h * SLAB:(h + 1) * SLAB, :] = wuq_h.T.astype(BF16)
            wukv_h = wukv_ref[:, h, :]
            wuk_out[:, h * SLAB:(h + 1) * SLAB] = jnp.where(lane < NOPE, wukv_h, 0.0).astype(BF16)
            wuvt_out[h * VDIM:(h + 1) * VDIM, :] = wukv_h.T[NOPE:].astype(BF16)

    wba_out[...] = wba_ref[...].astype(BF16)
    wbp_out[...] = wbp_ref[...].astype(BF16)
    wo_out[...] = wo_ref[...].astype(BF16)
    o_kr = Q_RANK + KV_RANK
    wa_ref[:, :o_kr] = wt_ref[:o_kr, :].T.astype(BF16)
    kr = jnp.concatenate([jnp.zeros((NOPE, PREP_ROWS), F32), wt_ref[o_kr:o_kr + ROPE, :],
                          jnp.zeros((SLAB - NOPE - ROPE, PREP_ROWS), F32)], axis=0)
    wa_ref[:, o_kr:] = kr.T.astype(BF16)
    wg_ref[...] = wt_ref[o_kr + ROPE:, :].T.astype(BF16)


def _proj_kernel(x_ref, nin_ref, wa_ref, wg_ref, qn_ref, wuqt_ref, kvn_ref, wuk_ref, wuvt_ref,
                 poolw_ref, pscale_ref, cos_ref, sin_ref,
                 qt_out, k_out, vt_out, sga_out, ypool_out, gates_out, carry_ref):
    t = PROJ_TILE
    si = pl.program_id(1)

    @pl.when(si == 0)
    def _():
        carry_ref[...] = jnp.zeros_like(carry_ref)

    x = x_ref[0]
    hn = _rms(x, nin_ref[...]).astype(BF16)

    za = jnp.dot(hn, wa_ref[...], preferred_element_type=F32)
    zq = za[:, :Q_RANK]
    zkv = za[:, Q_RANK:Q_RANK + KV_RANK]
    zkr = za[:, Q_RANK + KV_RANK:]
    cq = _rms(zq, qn_ref[...]).astype(BF16)
    ckv = _rms(zkv, kvn_ref[...]).astype(BF16)

    qt = lax.dot_general(wuqt_ref[...], cq, NT_DIMS, preferred_element_type=F32)
    scale = QK_SCALE
    cos, sin = cos_ref[...], sin_ref[...]
    qcos, qsin = cos * scale, sin * scale
    half = ROPE // 2
    tq = ATTN_TQ
    for h in range(HEADS):
        r0 = h * SLAB
        x1 = qt[r0 + NOPE:r0 + NOPE + half]
        x2 = qt[r0 + NOPE + half:r0 + NOPE + ROPE]
        slab = jnp.concatenate([qt[r0:r0 + NOPE] * scale, x1 * qcos - x2 * qsin, x1 * qsin + x2 * qcos,
                                jnp.zeros((SLAB - NOPE - ROPE, t), F32)], axis=0).astype(BF16)
        pr, hs = h // HEADS_PER_STEP, h % HEADS_PER_STEP
        for c in range(t // tq):
            qt_out[0, pr, c, hs * SLAB:(hs + 1) * SLAB, :] = slab[:, c * tq:(c + 1) * tq]

    kf = jnp.dot(ckv, wuk_ref[...], preferred_element_type=F32)
    zt = zkr.T
    k1, k2 = zt[NOPE:NOPE + half], zt[NOPE + half:NOPE + ROPE]
    kr = jnp.concatenate([jnp.zeros((NOPE, t), F32), k1 * cos - k2 * sin, k1 * sin + k2 * cos,
                          jnp.zeros((SLAB - NOPE - ROPE, t), F32)], axis=0).T
    for h in range(HEADS):
        sl = slice(h * SLAB, (h + 1) * SLAB)
        k_out[0, :, sl] = (kf[:, sl] + kr).astype(BF16)

    vt = lax.dot_general(wuvt_ref[...], ckv, NT_DIMS, preferred_element_type=F32)
    for c in range(t // ATTN_TK):
        cols = slice(c * ATTN_TK, (c + 1) * ATTN_TK)
        for h in range(HEADS):
            pr, r0 = h // HEADS_PER_STEP, (h % HEADS_PER_STEP) * VH_ROWS
            vt_out[0, pr, c, r0:r0 + VDIM, :] = vt[h * VDIM:(h + 1) * VDIM, cols].astype(BF16)
            vt_out[0, pr, c, r0 + VDIM:r0 + VH_ROWS, :] = jnp.ones((ONES_ROWS, ATTN_TK), BF16)

    zg = jnp.dot(hn, wg_ref[:, :3 * POOL_WIDTH], preferred_element_type=F32)
    g_attn = zg[:, :MLA_WIDTH]
    u = zg[:, MLA_WIDTH:MLA_WIDTH + POOL_WIDTH]
    g_pool = zg[:, MLA_WIDTH + POOL_WIDTH:]
    sga_out[0] = (g_attn * _sigmoid(g_attn)).astype(BF16)
    for c in range(2 * D_MODEL // GATE_COLS):
        lo = 3 * POOL_WIDTH + c * GATE_COLS
        gm = jnp.dot(hn, wg_ref[:, lo:lo + GATE_COLS], preferred_element_type=F32)
        gates_out[0, :, c * GATE_COLS:(c + 1) * GATE_COLS] = _sigmoid(gm).astype(BF16)

    ue = jnp.concatenate([carry_ref[...], u], axis=0)
    carry_ref[...] = u[t - HALO:, :]
    pos = si * t + lax.broadcasted_iota(jnp.int32, (t, 1), 0) + 1
    outs = []
    for gi, w in enumerate(POOL_WINDOWS):
        sl = slice(gi * POOL_GROUP, (gi + 1) * POOL_GROUP)
        acc = ue[:, sl]
        step = 1
        while step < w:
            acc = acc + pltpu.roll(acc, step, axis=0)
            step *= 2
        inv = 1.0 / jnp.minimum(pos, w).astype(F32)
        d = acc[HALO:, :] * inv - u[:, sl]
        outs.append(jnp.dot(d.astype(BF16), poolw_ref[gi], preferred_element_type=F32))
    y = jnp.concatenate(outs, axis=-1) * pscale_ref[...]
    ypool_out[0] = (y * (g_pool * _sigmoid(g_pool))).astype(BF16)


def _attn_kernel(qt_ref, k_ref, vt_ref, sga_ref, o_ref, sa_ref, sb_ref, mxa_ref, mxb_ref,
                 m_ref, l_ref, acc_ref):
    tq, tk = ATTN_TQ, ATTN_TK
    nq = HEADS_PER_STEP * tq
    n_qt = qt_ref.shape[2]
    kblocks = tk // CHUNK
    qry_chunk = lax.shift_right_logical(lax.broadcasted_iota(jnp.int32, (1, 1, nq), 2) & (tq - 1),
                                        CHUNK.bit_length() - 1)
    key_block = lax.broadcasted_iota(jnp.int32, (kblocks, 1, 1), 0)

    def score_tile(s_ref, mx_ref, qi, j, mask_for, diagonal=False):
        split = diagonal and tq == tk
        hk, hq, hb = tk // 2, tq // 2, kblocks // 2
        if mask_for is not None:
            allowed = key_block + (j * kblocks - mask_for * (tq // CHUNK)) <= qry_chunk
            bias = jnp.where(allowed, 0.0, -jnp.inf).astype(F32)
        for g in range(PAIRS_PER_STEP):
            for h in range(HEADS_PER_STEP):
                klanes = slice((g * HEADS_PER_STEP + h) * SLAB, (g * HEADS_PER_STEP + h + 1) * SLAB)
                qrows = slice(h * SLAB, (h + 1) * SLAB)
                cols = slice(h * tq, (h + 1) * tq)
                if not split:
                    kt = k_ref[0, pl.ds(pl.multiple_of(j * tk, tk), tk), klanes]
                    s = jnp.dot(kt, qt_ref[0, g, qi, qrows, :], preferred_element_type=F32)
                    if mask_for is not None:
                        s = (s.reshape(kblocks, CHUNK, tq) + bias[:, :, cols]).reshape(tk, tq)
                    s_ref[g, :, cols] = s
                    mx_ref[g, :, cols] = jnp.max(s, axis=0, keepdims=True)
                    continue
                late = slice(h * tq + hq, (h + 1) * tq)
                k_top = k_ref[0, pl.ds(pl.multiple_of(j * tk, tk), hk), klanes]
                k_bot = k_ref[0, pl.ds(pl.multiple_of(j * tk + hk, hk), hk), klanes]
                s = jnp.dot(k_top, qt_ref[0, g, qi, qrows, :], preferred_element_type=F32)
                s = (s.reshape(hb, CHUNK, tq) + bias[:hb, :, cols]).reshape(hk, tq)
                s_ref[g, :hk, cols] = s
                mx = jnp.max(s, axis=0, keepdims=True)
                s = jnp.dot(k_bot, qt_ref[0, g, qi, qrows, hq:], preferred_element_type=F32)
                s = (s.reshape(hb, CHUNK, hq) + bias[hb:, :, late]).reshape(hk, hq)
                s_ref[g, hk:, late] = s
                mx_ref[g, :, cols] = jnp.concatenate(
                    [mx[:, :hq], jnp.maximum(mx[:, hq:], jnp.max(s, axis=0, keepdims=True))], axis=1)

    def process(s_ref, mx_ref, j, diagonal=False):
        split = diagonal and tq == tk
        hk, hq = tk // 2, tq // 2
        for g in range(PAIRS_PER_STEP):
            m_old = m_ref[g]
            m_new = jnp.maximum(m_old, mx_ref[g])
            p = jnp.exp2((s_ref[g, :hk, :] if split else s_ref[g]) - m_new).astype(BF16)
            alpha = jnp.exp2(m_old - m_new)
            m_ref[g] = m_new
            for h in range(HEADS_PER_STEP):
                cols = slice(h * tq, (h + 1) * tq)
                vrows = slice(h * VH_ROWS, (h + 1) * VH_ROWS)
                if split:
                    late = slice(h * tq + hq, (h + 1) * tq)
                    p_late = jnp.exp2(s_ref[g, hk:, late] - m_new[:, late]).astype(BF16)
                    pv = jnp.dot(vt_ref[0, g, j, vrows, :hk], p[:, cols], preferred_element_type=F32)
                    pv_late = jnp.dot(vt_ref[0, g, j, vrows, hk:], p_late, preferred_element_type=F32)
                    pv = jnp.concatenate([pv[:, :hq], pv[:, hq:] + pv_late], axis=1)
                else:
                    pv = jnp.dot(vt_ref[0, g, j, vrows, :], p[:, cols],
                                 preferred_element_type=F32)
                l_ref[g, :, cols] = alpha[:, cols] * l_ref[g, :, cols] + pv[VDIM:VDIM + 1, :]
                acc_ref[g, h] = alpha[:, cols] * acc_ref[g, h] + pv[:VDIM, :]

    score_tile(sa_ref, mxa_ref, 0, 0, 0, diagonal=True)

    def query_tile(i, carry):
        last = ((i + 1) * tq + tk - 1) // tk - 1
        i_next = jnp.minimum(i + 1, n_qt - 1)
        below_mask = None if tq == tk else i
        next_mask = None if tq == tk else i_next

        def pair(j, scores_last):
            score_tile(sb_ref, mxb_ref, i, j + 1, below_mask)
            process(sa_ref, mxa_ref, j)
            score_tile(sa_ref, mxa_ref, i, j + 2, i if scores_last else below_mask, diagonal=scores_last)
            process(sb_ref, mxb_ref, j + 1)

        even = last % 2 == 0
        peeled = jnp.minimum(last // 2, CLOSING_PAIRS)

        def pair_body(jj, c):
            pair(2 * jj, False)
            return c

        lax.fori_loop(0, last // 2 - peeled, pair_body, 0)

        def close_even(n):
            for q in range(n):
                pair(jnp.maximum(last - 2 * (n - q), 0), q == n - 1)
            process(sa_ref, mxa_ref, last, diagonal=True)
            score_tile(sa_ref, mxa_ref, i_next, 0, next_mask)
            finish(i)

        def close_odd(n):
            for q in range(n):
                pair(jnp.maximum(last - 1 - 2 * (n - q), 0), False)
            score_tile(sb_ref, mxb_ref, i, last, i, diagonal=True)
            process(sa_ref, mxa_ref, jnp.maximum(last - 1, 0))
            score_tile(sa_ref, mxa_ref, i_next, 0, next_mask)
            process(sb_ref, mxb_ref, last, diagonal=True)
            finish(i)

        for n in range(CLOSING_PAIRS + 1):
            pl.when(even & (peeled == n))(functools.partial(close_even, n))
            pl.when(~even & (peeled == n))(functools.partial(close_odd, n))

        return carry

    def reset():
        m_ref[...] = jnp.full_like(m_ref, -jnp.inf)
        l_ref[...] = jnp.zeros_like(l_ref)
        acc_ref[...] = jnp.zeros_like(acc_ref)

    def finish(i):
        rows = pl.ds(pl.multiple_of(i * tq, tq), tq)
        for g in range(PAIRS_PER_STEP):
            inv_l = 1.0 / l_ref[g]
            ot = jnp.concatenate([acc_ref[g, h] * inv_l[:, h * tq:(h + 1) * tq]
                                  for h in range(HEADS_PER_STEP)], axis=0)
            cols = slice(g * LANES, (g + 1) * LANES)
            o_ref[0, rows, cols] = (ot.T * sga_ref[0, rows, cols].astype(F32)).astype(BF16)
        reset()

    reset()
    lax.fori_loop(0, n_qt, query_tile, 0)


def _merge_kernel(x_ref, ya_ref, ypool_ref, gates_ref, wa_ref, wp_ref, wo_ref, nf_ref, out_ref):
    a = jnp.dot(ya_ref[0], wa_ref[...], preferred_element_type=F32)
    p = jnp.dot(ypool_ref[0], wp_ref[...], preferred_element_type=F32)
    ga = gates_ref[0, :, :D_MODEL].astype(F32)
    gp = gates_ref[0, :, D_MODEL:].astype(F32)
    merged = (ga * a + gp * p).astype(BF16)
    h = x_ref[0] + jnp.dot(merged, wo_ref[...], preferred_element_type=F32)
    out_ref[0] = _rms(h, nf_ref[...])


def _rope_tables(seq):
    half = ROPE // 2
    inv_freq = ROPE_THETA ** (-jnp.arange(half, dtype=F32) / half)
    ang = jnp.arange(seq, dtype=F32)[None, :] * inv_freq[:, None]
    return jnp.cos(ang), jnp.sin(ang)


def _const_spec(shape):
    return pl.BlockSpec(shape, lambda *_: (0,) * len(shape), pipeline_mode=pl.Buffered(1))


def kernel(x, norm_in, w_in, q_norm, w_uq, kv_norm, w_ukv, pool_w, pool_scale,
           w_branch_attn, w_branch_pool, w_out, norm_final):
    b, s, d = x.shape
    tq, tk, t1, t3 = ATTN_TQ, ATTN_TK, PROJ_TILE, MERGE_TILE
    assert d == D_MODEL and s % t1 == 0 and t1 % tk == 0 and t1 % tq == 0 and s % t3 == 0
    hp = HEADS // HEADS_PER_STEP
    nq = HEADS_PER_STEP * tq
    g2 = PAIRS_PER_STEP

    n_gate = w_in.shape[1] - (Q_RANK + KV_RANK + ROPE)
    n_prep = d // PREP_ROWS
    rows_spec = lambda rows: pl.BlockSpec((rows // n_prep, d), lambda i: (i, 0))
    assert NOPE + VDIM == SLAB
    whole = lambda shape: pl.BlockSpec(shape, lambda i: (0,) * len(shape))
    small = [(HEADS * SLAB, Q_RANK), (KV_RANK, HEADS * SLAB), (MLA_WIDTH, KV_RANK), pool_w.shape]
    w_a, w_g, w_ba16, w_bp16, w_o16, w_uqt, w_uk, w_uvt, pool_w16 = pl.pallas_call(
        _weight_prep_kernel,
        grid=(n_prep,),
        in_specs=[pl.BlockSpec((w_in.shape[1], PREP_ROWS), lambda i: (0, i)),
                  rows_spec(MLA_WIDTH), rows_spec(POOL_WIDTH), rows_spec(d),
                  whole(w_uq.shape), whole(w_ukv.shape), whole(pool_w.shape)],
        out_specs=[pl.BlockSpec((PREP_ROWS, LATENT_COLS), lambda i: (i, 0)),
                   pl.BlockSpec((PREP_ROWS, n_gate), lambda i: (i, 0)),
                   rows_spec(MLA_WIDTH), rows_spec(POOL_WIDTH), rows_spec(d)]
                  + [whole(shape) for shape in small],
        out_shape=[jax.ShapeDtypeStruct((d, LATENT_COLS), BF16),
                   jax.ShapeDtypeStruct((d, n_gate), BF16),
                   jax.ShapeDtypeStruct((MLA_WIDTH, d), BF16),
                   jax.ShapeDtypeStruct((POOL_WIDTH, d), BF16),
                   jax.ShapeDtypeStruct((d, d), BF16)]
                  + [jax.ShapeDtypeStruct(shape, BF16) for shape in small],
        name="weight_prep",
    )(w_in.T, w_branch_attn, w_branch_pool, w_out,
      w_uq, w_ukv, pool_w)
    cos_t, sin_t = _rope_tables(s)

    row = lambda v: v.reshape(1, -1).astype(F32)
    tok = lambda width: pl.BlockSpec((1, t1, width), lambda bi, si: (bi, si, 0))
    tabt = pl.BlockSpec((ROPE // 2, t1), lambda bi, si: (0, si))
    qt, k, vt, sga, ypool, gates = pl.pallas_call(
        _proj_kernel,
        grid=(b, s // t1),
        in_specs=[tok(d), _const_spec((1, d)), _const_spec(w_a.shape), _const_spec(w_g.shape),
                  _const_spec((1, Q_RANK)), _const_spec(w_uqt.shape), _const_spec((1, KV_RANK)),
                  _const_spec(w_uk.shape), _const_spec(w_uvt.shape),
                  _const_spec(pool_w.shape), _const_spec((1, POOL_WIDTH)),
                  tabt, tabt],
        out_specs=[pl.BlockSpec((1, hp, t1 // tq, HEADS_PER_STEP * SLAB, tq),
                                lambda bi, si: (bi, 0, si, 0, 0)),
                   tok(HEADS * SLAB),
                   pl.BlockSpec((1, hp, t1 // tk, VT_ROWS, tk), lambda bi, si: (bi, 0, si, 0, 0)),
                   tok(MLA_WIDTH), tok(POOL_WIDTH), tok(2 * d)],
        out_shape=[jax.ShapeDtypeStruct((b, hp, s // tq, HEADS_PER_STEP * SLAB, tq), BF16),
                   jax.ShapeDtypeStruct((b, s, HEADS * SLAB), BF16),
                   jax.ShapeDtypeStruct((b, hp, s // tk, VT_ROWS, tk), BF16),
                   jax.ShapeDtypeStruct((b, s, MLA_WIDTH), BF16),
                   jax.ShapeDtypeStruct((b, s, POOL_WIDTH), BF16),
                   jax.ShapeDtypeStruct((b, s, 2 * d), BF16)],
        scratch_shapes=[pltpu.VMEM((HALO, POOL_WIDTH), F32)],
        compiler_params=pltpu.CompilerParams(
            dimension_semantics=("arbitrary", "arbitrary"), vmem_limit_bytes=VMEM_LIMIT),
        name="proj",
    )(x, row(norm_in), w_a, w_g, row(q_norm), w_uqt, row(kv_norm), w_uk, w_uvt,
      pool_w16, row(pool_scale), cos_t, sin_t)

    o = pl.pallas_call(
        _attn_kernel,
        grid=(b, hp // g2),
        in_specs=[pl.BlockSpec((1, g2, s // tq, HEADS_PER_STEP * SLAB, tq), lambda bi, pi: (bi, pi, 0, 0, 0)),
                  pl.BlockSpec((1, s, g2 * HEADS_PER_STEP * SLAB), lambda bi, pi: (bi, 0, pi)),
                  pl.BlockSpec((1, g2, s // tk, VT_ROWS, tk), lambda bi, pi: (bi, pi, 0, 0, 0)),
                  pl.BlockSpec((1, s, g2 * LANES), lambda bi, pi: (bi, 0, pi))],
        out_specs=pl.BlockSpec((1, s, g2 * LANES), lambda bi, pi: (bi, 0, pi)),
        out_shape=jax.ShapeDtypeStruct((b, s, MLA_WIDTH), BF16),
        scratch_shapes=[pltpu.VMEM((g2, tk, nq), F32), pltpu.VMEM((g2, tk, nq), F32),
                        pltpu.VMEM((g2, 1, nq), F32), pltpu.VMEM((g2, 1, nq), F32),
                        pltpu.VMEM((g2, 1, nq), F32), pltpu.VMEM((g2, 1, nq), F32),
                        pltpu.VMEM((g2, HEADS_PER_STEP, VDIM, tq), F32)],
        compiler_params=pltpu.CompilerParams(
            dimension_semantics=("arbitrary", "arbitrary"), vmem_limit_bytes=VMEM_LIMIT),
        name="attn",
    )(qt, k, vt, sga)

    tok3 = lambda width: pl.BlockSpec((1, t3, width), lambda bi, si: (bi, si, 0))
    out = pl.pallas_call(
        _merge_kernel,
        grid=(b, s // t3),
        in_specs=[tok3(d), tok3(MLA_WIDTH), tok3(POOL_WIDTH), tok3(2 * d),
                  _const_spec((MLA_WIDTH, d)), _const_spec((POOL_WIDTH, d)), _const_spec((d, d)),
                  _const_spec((1, d))],
        out_specs=tok3(d),
        out_shape=jax.ShapeDtypeStruct((b, s, d), x.dtype),
        compiler_params=pltpu.CompilerParams(
            dimension_semantics=("arbitrary", "arbitrary"), vmem_limit_bytes=VMEM_LIMIT),
        name="merge",
    )(x, o, ypool, gates, w_ba16, w_bp16, w_o16, row(norm_final))
    return out
```
